```python
import math
import jax, jax.numpy as jnp
from jax import lax
import numpy as np

D_MODEL = 2048
BATCH = 4
SEQ = 2048
DEPTH = 1
DEC_BATCH = 128
DEC_SEQ = 8
PAST_LEN = 16384
PAGE_SIZE = 128

HEAD_DIM = 64
N_HEADS = D_MODEL // HEAD_DIM
N_KV_HEADS = N_HEADS // 8
GROUP = N_HEADS // N_KV_HEADS
WINDOW = 128
ATTN_WIDTH = N_HEADS * HEAD_DIM
KV_WIDTH = N_KV_HEADS * HEAD_DIM
CONV_CH = D_MODEL
CONV_W = 31
N_EXPERTS = 32
TOP_K = 4
D_FF = D_MODEL
SWIGLU_ALPHA = 1.702
SWIGLU_LIMIT = 7.0
EPS = 1e-5
IN_SPLITS = (CONV_CH, 2 * CONV_CH, 2 * CONV_CH + ATTN_WIDTH,
             2 * CONV_CH + ATTN_WIDTH + KV_WIDTH, 2 * CONV_CH + ATTN_WIDTH + 2 * KV_WIDTH,
             2 * CONV_CH + ATTN_WIDTH + 2 * KV_WIDTH + D_MODEL)
IN_WIDTH = 2 * CONV_CH + ATTN_WIDTH + 2 * KV_WIDTH + 2 * D_MODEL

kernel_name = "hybrid_conv_swa_sink_moe_adaln_step"


def rmsnorm(x, g):
    xf = x.astype(jnp.float32)
    y = xf * lax.rsqrt(jnp.mean(xf * xf, axis=-1, keepdims=True) + EPS)
    return (y * g.astype(jnp.float32)).astype(x.dtype)


def sink_softmax(scores, sinks, mask):
    sink = sinks.astype(jnp.float32).reshape(N_KV_HEADS, GROUP, 1, 1)
    s = jnp.where(mask, scores, -jnp.inf)
    m = jnp.maximum(jnp.max(s, axis=-1, keepdims=True), sink)
    p = jnp.exp(s - m)
    return p / (jnp.sum(p, axis=-1, keepdims=True) + jnp.exp(sink - m))


def banded_window_attention(q, k, v, sinks):
    b, s = q.shape[:2]
    nb = s // WINDOW
    qb = q.reshape(b, nb, WINDOW, N_KV_HEADS, GROUP, HEAD_DIM)
    kb = k.reshape(b, nb, WINDOW, N_KV_HEADS, HEAD_DIM)
    vb = v.reshape(b, nb, WINDOW, N_KV_HEADS, HEAD_DIM)
    kk = jnp.concatenate([jnp.concatenate([jnp.zeros_like(kb[:, :1]), kb[:, :-1]], axis=1), kb], axis=2)
    vv = jnp.concatenate([jnp.concatenate([jnp.zeros_like(vb[:, :1]), vb[:, :-1]], axis=1), vb], axis=2)
    scores = jnp.einsum('bnqkgd,bnskd->bnkgqs', qb, kk,
                        preferred_element_type=jnp.float32) * (HEAD_DIM ** -0.5)
    i = jnp.arange(WINDOW)[:, None]
    j = jnp.arange(2 * WINDOW)[None, :]
    diff = i + WINDOW - j
    band = (diff >= 0) & (diff < WINDOW)
    blk = jnp.arange(nb)[:, None, None]
    mask = band[None] & ((blk > 0) | (j >= WINDOW)[None])
    p = sink_softmax(scores, sinks, mask[None, :, None, None])
    out = jnp.einsum('bnkgqs,bnskd->bnqkgd', p.astype(v.dtype), vv)
    return out.reshape(b, s, ATTN_WIDTH)


def window_cache_attention(q, k_new, v_new, k_buf, v_buf, sinks):
    b, L = q.shape[:2]
    nbuf = k_buf.shape[1]
    kk = jnp.concatenate([k_buf.astype(k_new.dtype), k_new], axis=1)
    vv = jnp.concatenate([v_buf.astype(v_new.dtype), v_new], axis=1)
    qpos = PAST_LEN + jnp.arange(L)
    kpos = PAST_LEN - nbuf + jnp.arange(nbuf + L)
    diff = qpos[:, None] - kpos[None, :]
    mask = (diff >= 0) & (diff < WINDOW)
    qg = q.reshape(b, L, N_KV_HEADS, GROUP, HEAD_DIM)
    scores = jnp.einsum('bqkgd,bskd->bkgqs', qg, kk,
                        preferred_element_type=jnp.float32) * (HEAD_DIM ** -0.5)
    p = sink_softmax(scores, sinks, mask)
    out = jnp.einsum('bkgqs,bskd->bqkgd', p.astype(vv.dtype), vv).reshape(b, L, ATTN_WIDTH)
    return out, kk[:, -nbuf:], vv[:, -nbuf:]


def conformer_conv(u_a, u_b, buf, dw_w, dw_b, ln_g, ln_b, w_co, b_co):
    u = u_a * jax.nn.sigmoid(u_b)
    up = jnp.concatenate([buf.astype(u.dtype), u], axis=1)
    z = lax.conv_general_dilated(up, dw_w[:, None, :].astype(u.dtype), window_strides=(1,),
                                 padding='VALID', dimension_numbers=('NWC', 'WIO', 'NWC'),
                                 feature_group_count=CONV_CH) + dw_b
    zf = z.astype(jnp.float32)
    mu = jnp.mean(zf, axis=-1, keepdims=True)
    var = jnp.var(zf, axis=-1, keepdims=True)
    zn = (zf - mu) * lax.rsqrt(var + EPS) * ln_g.astype(jnp.float32) + ln_b.astype(jnp.float32)
    zs = jax.nn.silu(zn).astype(u.dtype)
    return zs @ w_co + b_co, up[:, -(CONV_W - 1):]


def clamped_swiglu(a):
    x_glu = jnp.minimum(a[..., ::2], SWIGLU_LIMIT)
    x_lin = jnp.clip(a[..., 1::2], -SWIGLU_LIMIT, SWIGLU_LIMIT)
    return x_glu * jax.nn.sigmoid(SWIGLU_ALPHA * x_glu) * (x_lin + 1)


def moe(h, w_router, b_router, w_mlp1, b_mlp1, w_mlp2, b_mlp2):
    logits = (h @ w_router + b_router).astype(jnp.float32)
    vals, idx = lax.top_k(logits, TOP_K)
    gates = jax.nn.softmax(vals, axis=-1)
    comb = jnp.einsum('tk,tke->te', gates,
                      jax.nn.one_hot(idx, N_EXPERTS, dtype=jnp.float32)).astype(h.dtype)
    out = jnp.zeros_like(h)
    for e in range(N_EXPERTS):
        act = clamped_swiglu(h @ w_mlp1[e] + b_mlp1[e])
        out = out + comb[:, e:e + 1] * (act @ w_mlp2[e] + b_mlp2[e])
    return out


def setup_inputs(seed: int = 0) -> dict:
    key = jax.random.key(seed)
    ks = jax.random.split(key, 32)
    f32 = jnp.float32
    nrm = lambda k, shape, s: jax.random.normal(k, shape, f32) * s
    w_buf = min(WINDOW, PAST_LEN)
    return {
        "x_prompt": nrm(ks[0], (BATCH, SEQ, D_MODEL), 1.0),
        "x_sample": nrm(ks[1], (DEC_BATCH, DEC_SEQ, D_MODEL), 1.0),
        "c_prompt": nrm(ks[2], (BATCH, D_MODEL), 1.0),
        "c_sample": nrm(ks[3], (DEC_BATCH, D_MODEL), 1.0),
        "cache_k": nrm(ks[4], (DEC_BATCH, w_buf, N_KV_HEADS, HEAD_DIM), 1.0),
        "cache_v": nrm(ks[5], (DEC_BATCH, w_buf, N_KV_HEADS, HEAD_DIM), 1.0),
        "state_conv": nrm(ks[6], (DEC_BATCH, CONV_W - 1, CONV_CH), 1.0),
        "w_ada": nrm(ks[7], (D_MODEL, 6 * D_MODEL), 0.5 * D_MODEL ** -0.5),
        "b_ada": nrm(ks[8], (6 * D_MODEL,), 0.02),
        "norm1_g": 1.0 + nrm(ks[9], (D_MODEL,), 0.02),
        "w_in": nrm(ks[10], (D_MODEL, IN_WIDTH), D_MODEL ** -0.5),
        "b_in": nrm(ks[11], (IN_WIDTH,), 0.02),
        "conv_dw_w": nrm(ks[12], (CONV_W, CONV_CH), CONV_W ** -0.5),
        "conv_dw_b": nrm(ks[13], (CONV_CH,), 0.02),
        "conv_ln_g": 1.0 + nrm(ks[14], (CONV_CH,), 0.02),
        "conv_ln_b": nrm(ks[15], (CONV_CH,), 0.02),
        "w_conv_out": nrm(ks[16], (CONV_CH, D_MODEL), CONV_CH ** -0.5),
        "b_conv_out": nrm(ks[17], (D_MODEL,), 0.02),
        "sinks": nrm(ks[18], (N_HEADS,), 1.0),
        "w_attn_out": nrm(ks[19], (ATTN_WIDTH, D_MODEL), ATTN_WIDTH ** -0.5),
        "b_attn_out": nrm(ks[20], (D_MODEL,), 0.02),
        "w_out": nrm(ks[21], (D_MODEL, D_MODEL), D_MODEL ** -0.5),
        "norm2_g": 1.0 + nrm(ks[22], (D_MODEL,), 0.02),
        "w_router": nrm(ks[23], (D_MODEL, N_EXPERTS), D_MODEL ** -0.5),
        "b_router": nrm(ks[24], (N_EXPERTS,), 0.01),
        "w_mlp1": nrm(ks[25], (N_EXPERTS, D_MODEL, 2 * D_FF), D_MODEL ** -0.5),
        "b_mlp1": nrm(ks[26], (N_EXPERTS, 2 * D_FF), 0.02),
        "w_mlp2": nrm(ks[27], (N_EXPERTS, D_FF, D_MODEL), D_FF ** -0.5),
        "b_mlp2": nrm(ks[28], (N_EXPERTS, D_MODEL), 0.02),
        "norm_f_g": 1.0 + nrm(ks[29], (D_MODEL,), 0.02),
    }


def reference(x_prompt, x_sample, c_prompt, c_sample, cache_k, cache_v, state_conv,
              w_ada, b_ada, norm1_g, w_in, b_in, conv_dw_w, conv_dw_b, conv_ln_g, conv_ln_b,
              w_conv_out, b_conv_out, sinks, w_attn_out, b_attn_out, w_out, norm2_g,
              w_router, b_router, w_mlp1, b_mlp1, w_mlp2, b_mlp2, norm_f_g):
    def modulations(c):
        m = jax.nn.silu(c) @ w_ada + b_ada
        return [t[:, None, :] for t in jnp.split(m, 6, axis=-1)]

    def mixer_sublayer(x, mod, conv_buf, k_buf, v_buf, prompt):
        b, L = x.shape[:2]
        h = rmsnorm(x, norm1_g) * (1 + mod[1]) + mod[0]
        z = h @ w_in + b_in
        u_a, u_b, q, k, v, g_c, g_a = jnp.split(z, IN_SPLITS, axis=-1)
        q = q.reshape(b, L, N_HEADS, HEAD_DIM)
        k = k.reshape(b, L, N_KV_HEADS, HEAD_DIM)
        v = v.reshape(b, L, N_KV_HEADS, HEAD_DIM)
        conv_out, new_conv = conformer_conv(u_a, u_b, conv_buf, conv_dw_w, conv_dw_b,
                                            conv_ln_g, conv_ln_b, w_conv_out, b_conv_out)
        nbuf = k_buf.shape[1]
        if prompt:
            attn = banded_window_attention(q, k, v, sinks)
            new_k, new_v = k[:, -nbuf:], v[:, -nbuf:]
        else:
            attn, new_k, new_v = window_cache_attention(q, k, v, k_buf, v_buf, sinks)
        attn_out = attn @ w_attn_out + b_attn_out
        merged = jax.nn.sigmoid(g_c) * conv_out + jax.nn.sigmoid(g_a) * attn_out
        return x + mod[2] * (merged @ w_out), new_conv, new_k, new_v

    bp, sp = x_prompt.shape[:2]
    bs, ss = x_sample.shape[:2]
    mp = modulations(c_prompt)
    ms = modulations(c_sample)
    xp, xs = x_prompt, x_sample
    for _ in range(DEPTH):
        zero_conv = jnp.zeros((bp, CONV_W - 1, CONV_CH), xp.dtype)
        xp, conv_p, k_p, v_p = mixer_sublayer(xp, mp, zero_conv, cache_k[:bp], cache_v[:bp], True)
        xs, conv_s, k_s, v_s = mixer_sublayer(xs, ms, state_conv, cache_k, cache_v, False)
        h2p = rmsnorm(xp, norm2_g) * (1 + mp[4]) + mp[3]
        h2s = rmsnorm(xs, norm2_g) * (1 + ms[4]) + ms[3]
        tokens = jnp.concatenate([h2p.reshape(-1, D_MODEL), h2s.reshape(-1, D_MODEL)], axis=0)
        f = moe(tokens, w_router, b_router, w_mlp1, b_mlp1, w_mlp2, b_mlp2)
        xp = xp + mp[5] * f[:bp * sp].reshape(bp, sp, D_MODEL)
        xs = xs + ms[5] * f[bp * sp:].reshape(bs, ss, D_MODEL)
    y_prompt = rmsnorm(xp, norm_f_g)
    y_sample = rmsnorm(xs, norm_f_g)
    return (y_prompt, y_sample, k_p, v_p, conv_p, k_s, v_s, conv_s)
```

```python
import functools

import jax
import jax.numpy as jnp
from jax import lax
from jax.experimental import pallas as pl
from jax.experimental.pallas import tpu as pltpu

F32 = jnp.float32
BF16 = jnp.bfloat16

D = 2048
BP, SP = 4, 2048
BS, SS = 128, 8
TP = BP * SP
TS = BS * SS
T = TP + TS
HD = 64
NH = 32
NKV = 4
GRP = NH // NKV
WIN = 128
KVW = NKV * HD
CW = 31
NE = 32
TOPK = 4
DFF = 2048
ALPHA = 1.702
LIMIT = 7.0
EPS = 1e-5
OFF_GLU_A, OFF_GLU_B, OFF_Q, OFF_KV, OFF_GC = 0, D, 2 * D, 3 * D, 3 * D + 2 * KVW
IN_W = 5 * D + 2 * KVW

VMEM_LIMIT = 56 * 1024 * 1024

SUB = 256
GSUB = 8
NT_MAX = (T * TOPK) // SUB + NE
P_MAX = NT_MAX * SUB
G_MAX = -(-((T * TOPK) // SUB) // GSUB) + NE
TF = 256
NC = DFF // TF


def _cparams(sem):
    return pltpu.CompilerParams(dimension_semantics=sem, vmem_limit_bytes=VMEM_LIMIT)


def _sigmoid(x):
    return 1.0 / (1.0 + jnp.exp(-x))


def _mod_kernel(c_ref, w_ref, b_ref, o_ref):
    c = c_ref[...]
    sc = (c * _sigmoid(c)).astype(BF16)
    o_ref[...] = jnp.dot(sc, w_ref[...].astype(BF16), preferred_element_type=F32) + b_ref[...]


def _modulations(c_all, w_ada, b_ada):
    nb = c_all.shape[0]
    tn = 1024
    return pl.pallas_call(
        _mod_kernel,
        out_shape=jax.ShapeDtypeStruct((nb, 6 * D), F32),
        grid=(6 * D // tn,),
        in_specs=[pl.BlockSpec((nb, D), lambda j: (0, 0)),
                  pl.BlockSpec((D, tn), lambda j: (0, j)),
                  pl.BlockSpec((1, tn), lambda j: (0, j))],
        out_specs=pl.BlockSpec((nb, tn), lambda j: (0, j)),
        compiler_params=_cparams(("parallel",)),
        name="modulations",
    )(c_all, w_ada, b_ada.reshape(1, -1))


class _Path:
    def __init__(self, nb, r, steps, per_batch, row_block0):
        self.nb, self.r, self.steps = nb, r, steps
        self.per_batch = per_batch
        self.row_block0 = row_block0
        self.tm = nb * r

    def x_index(self, i):
        if self.per_batch:
            return (i // self.per_batch, i % self.per_batch, 0)
        return (i, 0, 0)

    def mod_index(self, i):
        if self.per_batch:
            return (i // self.per_batch, 0, 0)
        return (i, 0, 0)


def _prompt_path(tm):
    return _Path(1, tm, TP // tm, SP // tm, 0)


def _sample_path(tm):
    return _Path(tm // SS, SS, TS // tm, 0, TP // tm)


def _norm_mod(x, g, shift, scale):
    ms = jnp.mean(x * x, axis=-1, keepdims=True)
    return (x * lax.rsqrt(ms + EPS) * g) * (1.0 + scale) + shift


def _in_proj_kernel(*refs, mode, out_scale, aliased):
    if aliased:
        refs = refs[1:]
    if mode == "glu":
        x_ref, g_ref, sh_ref, sc_ref, wa_ref, wb_ref, ba_ref, bb_ref, o_ref, h_scr = refs
    else:
        x_ref, g_ref, sh_ref, sc_ref, wa_ref, ba_ref, o_ref, h_scr = refs

    @pl.when(pl.program_id(1) == 0)
    def _():
        h = _norm_mod(x_ref[...], g_ref[...], sh_ref[...], sc_ref[...])
        h_scr[...] = h.reshape(h_scr.shape).astype(BF16)

    h = h_scr[...]
    a = jnp.dot(h, wa_ref[...], preferred_element_type=F32) + ba_ref[...]
    if mode == "glu":
        b = jnp.dot(h, wb_ref[...], preferred_element_type=F32) + bb_ref[...]
        a = a * _sigmoid(b)
    elif mode == "sigmoid":
        a = _sigmoid(a)
    elif out_scale != 1.0:
        a = a * out_scale
    o_ref[...] = a.astype(o_ref.dtype)


def _in_proj(path, x3, g, shift, scale, w_b, b_in2, col_a, col_b, n_cols, mode, out_dtype,
             out_scale=1.0, prev=None):
    tn = 512
    tm = path.tm
    ca, cb = col_a // tn, (col_b // tn if col_b is not None else 0)
    aliased = prev is not None
    in_specs = []
    args = []
    if aliased:
        in_specs.append(pl.BlockSpec(memory_space=pl.ANY))
        args.append(prev)
    in_specs += [pl.BlockSpec((path.nb, path.r, D), lambda i, j: path.x_index(i)),
                 pl.BlockSpec((1, 1, D), lambda i, j: (0, 0, 0)),
                 pl.BlockSpec((path.nb, 1, D), lambda i, j: path.mod_index(i)),
                 pl.BlockSpec((path.nb, 1, D), lambda i, j: path.mod_index(i)),
                 pl.BlockSpec((D, tn), lambda i, j: (0, ca + j))]
    args += [x3, g.reshape(1, 1, D), shift, scale, w_b]
    if mode == "glu":
        in_specs.append(pl.BlockSpec((D, tn), lambda i, j: (0, cb + j)))
        args.append(w_b)
    in_specs.append(pl.BlockSpec((1, tn), lambda i, j: (0, ca + j)))
    args.append(b_in2)
    if mode == "glu":
        in_specs.append(pl.BlockSpec((1, tn), lambda i, j: (0, cb + j)))
        args.append(b_in2)
    rb0 = path.row_block0
    return pl.pallas_call(
        functools.partial(_in_proj_kernel, mode=mode, out_scale=out_scale, aliased=aliased),
        out_shape=jax.ShapeDtypeStruct((T, n_cols), out_dtype),
        grid=(path.steps, n_cols // tn),
        in_specs=in_specs,
        out_specs=pl.BlockSpec((tm, tn), lambda i, j: (rb0 + i, j)),
        scratch_shapes=[pltpu.VMEM((tm, D), BF16)],
        input_output_aliases=({0: 0} if aliased else {}),
        compiler_params=_cparams(("parallel", "arbitrary")),
        name="in_proj_" + mode,
    )(*args)


CONV_TL = 128
HALO = 32


def _ln_swish(z, g, b):
    mu = jnp.mean(z, axis=-1, keepdims=True)
    d = z - mu
    var = jnp.mean(d * d, axis=-1, keepdims=True)
    zn = d * lax.rsqrt(var + EPS) * g + b
    return zn * _sigmoid(zn)


def _conv_prompt_kernel(u_ref, halo_ref, w_ref, dwb_ref, g_ref, b_ref, o_ref, win, z_scr):
    t = pl.program_id(1)
    win[0:HALO, :] = jnp.where(t == 0, 0.0, halo_ref[...])
    win[HALO:HALO + CONV_TL, :] = u_ref[...]
    base = HALO - (CW - 1)
    for s in range(D // 128):
        cs = slice(s * 128, (s + 1) * 128)
        acc = None
        for j in range(CW):
            term = win[base + j:base + j + CONV_TL, cs] * w_ref[j:j + 1, cs]
            acc = term if acc is None else acc + term
        z_scr[:, cs] = acc + dwb_ref[:, cs]
    o_ref[...] = _ln_swish(z_scr[...], g_ref[...], b_ref[...]).astype(BF16)


def _conv_prompt(u_all, dw_w, dw_b, ln_g, ln_b):
    per_b = SP // CONV_TL
    hb = CONV_TL // HALO
    return pl.pallas_call(
        _conv_prompt_kernel,
        out_shape=jax.ShapeDtypeStruct((T, D), BF16),
        grid=(BP, per_b),
        in_specs=[pl.BlockSpec((CONV_TL, D), lambda b, t: (b * per_b + t, 0)),
                  pl.BlockSpec((HALO, D), lambda b, t: (jnp.maximum((b * per_b + t) * hb - 1, 0), 0)),
                  pl.BlockSpec((CW, D), lambda b, t: (0, 0)),
                  pl.BlockSpec((1, D), lambda b, t: (0, 0)),
                  pl.BlockSpec((1, D), lambda b, t: (0, 0)),
                  pl.BlockSpec((1, D), lambda b, t: (0, 0))],
        out_specs=pl.BlockSpec((CONV_TL, D), lambda b, t: (b * per_b + t, 0)),
        scratch_shapes=[pltpu.VMEM((HALO + CONV_TL, D), F32), pltpu.VMEM((CONV_TL, D), F32)],
        compiler_params=_cparams(("parallel", "arbitrary")),
        name="conv_prompt",
    )(u_all, u_all, dw_w, dw_b.reshape(1, D), ln_g.reshape(1, D), ln_b.reshape(1, D))


CONV_BB = 8


def _conv_sample_kernel(prev_ref, up_ref, w_ref, dwb_ref, g_ref, b_ref, o_ref, z_scr):
    del prev_ref
    for s in range(D // 128):
        cs = slice(s * 128, (s + 1) * 128)
        acc = None
        for j in range(CW):
            term = up_ref[:, j:j + SS, cs] * w_ref[j:j + 1, cs][None]
            acc = term if acc is None else acc + term
        z_scr[:, :, cs] = acc + dwb_ref[:, cs][None]
    zs = _ln_swish(z_scr[...], g_ref[...][None], b_ref[...][None])
    o_ref[...] = zs.reshape(CONV_BB * SS, D).astype(BF16)


def _conv_sample(prev, up_s, dw_w, dw_b, ln_g, ln_b):
    rb0 = TP // (CONV_BB * SS)
    return pl.pallas_call(
        _conv_sample_kernel,
        out_shape=jax.ShapeDtypeStruct((T, D), BF16),
        grid=(BS // CONV_BB,),
        in_specs=[pl.BlockSpec(memory_space=pl.ANY),
                  pl.BlockSpec((CONV_BB, CW - 1 + SS, D), lambda i: (i, 0, 0)),
                  pl.BlockSpec((CW, D), lambda i: (0, 0)),
                  pl.BlockSpec((1, D), lambda i: (0, 0)),
                  pl.BlockSpec((1, D), lambda i: (0, 0)),
                  pl.BlockSpec((1, D), lambda i: (0, 0))],
        out_specs=pl.BlockSpec((CONV_BB * SS, D), lambda i: (rb0 + i, 0)),
        scratch_shapes=[pltpu.VMEM((CONV_BB, SS, D), F32)],
        input_output_aliases={0: 0},
        compiler_params=_cparams(("parallel",)),
        name="conv_sample",
    )(prev, up_s, dw_w, dw_b.reshape(1, D), ln_g.reshape(1, D), ln_b.reshape(1, D))


def _attn_prompt_kernel(sink_ref, q_ref, kvc_ref, kvp_ref, o_ref):
    n = pl.program_id(1)
    row = lax.broadcasted_iota(jnp.int32, (WIN, 2 * WIN), 0)
    col = lax.broadcasted_iota(jnp.int32, (WIN, 2 * WIN), 1)
    diff = row + WIN - col
    first_key = jnp.where(n > 0, 0, WIN)
    mask = (diff >= 0) & (diff < WIN) & (col >= first_key)
    kvc = kvc_ref[...].astype(BF16)
    kvp = kvp_ref[...].astype(BF16)
    outs = []
    for kh in range(NKV):
        k2 = jnp.concatenate([kvp[:, kh * HD:(kh + 1) * HD], kvc[:, kh * HD:(kh + 1) * HD]], axis=0)
        v2 = jnp.concatenate([kvp[:, KVW + kh * HD:KVW + (kh + 1) * HD],
                              kvc[:, KVW + kh * HD:KVW + (kh + 1) * HD]], axis=0)
        for g in range(GRP):
            h = kh * GRP + g
            sink = sink_ref[h]
            qh = q_ref[:, h * HD:(h + 1) * HD]
            s = lax.dot_general(qh, k2, (((1,), (1,)), ((), ())), preferred_element_type=F32)
            s = jnp.where(mask, s, -jnp.inf)
            m = jnp.maximum(jnp.max(s, axis=-1, keepdims=True), sink)
            p = jnp.exp(s - m)
            den = jnp.sum(p, axis=-1, keepdims=True) + jnp.exp(sink - m)
            o = jnp.dot(p.astype(BF16), v2, preferred_element_type=F32)
            outs.append(o / den)
    o_ref[...] = jnp.concatenate(outs, axis=-1).astype(BF16)


def _attn_prompt(sinks, q_all, kv_all):
    nblk = SP // WIN
    return pl.pallas_call(
        _attn_prompt_kernel,
        out_shape=jax.ShapeDtypeStruct((T, D), BF16),
        grid=(BP, nblk),
        in_specs=[pl.BlockSpec(memory_space=pltpu.SMEM),
                  pl.BlockSpec((WIN, D), lambda b, n: (b * nblk + n, 0)),
                  pl.BlockSpec((WIN, 2 * KVW), lambda b, n: (b * nblk + n, 0)),
                  pl.BlockSpec((WIN, 2 * KVW), lambda b, n: (b * nblk + jnp.maximum(n - 1, 0), 0))],
        out_specs=pl.BlockSpec((WIN, D), lambda b, n: (b * nblk + n, 0)),
        compiler_params=_cparams(("parallel", "arbitrary")),
        name="attn_prompt",
    )(sinks, q_all, kv_all, kv_all)


ATT_BB = 8
QG = SS * GRP


def _attn_sample_kernel(sink_ref, q_ref, kvn_ref, ck_ref, cv_ref, o_ref):
    rr = lax.broadcasted_iota(jnp.int32, (QG, WIN), 0)
    qi_c = rr % SS
    key_c = lax.broadcasted_iota(jnp.int32, (QG, WIN), 1)
    mask_c = (key_c > qi_c)[None]
    rr_n = lax.broadcasted_iota(jnp.int32, (QG, SS), 0)
    key_n = lax.broadcasted_iota(jnp.int32, (QG, SS), 1)
    mask_n = (key_n <= rr_n % SS)[None]
    grow = lax.broadcasted_iota(jnp.int32, (QG, 1), 0) // SS
    for kh in range(NKV):
        sink = jnp.zeros((QG, 1), F32)
        for g in range(GRP):
            sink = jnp.where(grow == g, sink_ref[kh * GRP + g], sink)
        sink = sink[None]
        q = q_ref[:, kh]
        kc = ck_ref[:, :, kh * HD:(kh + 1) * HD].astype(BF16)
        vc = cv_ref[:, :, kh * HD:(kh + 1) * HD].astype(BF16)
        kn = kvn_ref[:, :, kh * HD:(kh + 1) * HD].astype(BF16)
        vn = kvn_ref[:, :, KVW + kh * HD:KVW + (kh + 1) * HD].astype(BF16)
        s1 = jnp.einsum("bqd,bsd->bqs", q, kc, preferred_element_type=F32)
        s2 = jnp.einsum("bqd,bsd->bqs", q, kn, preferred_element_type=F32)
        s1 = jnp.where(mask_c, s1, -jnp.inf)
        s2 = jnp.where(mask_n, s2, -jnp.inf)
        m = jnp.maximum(jnp.maximum(jnp.max(s1, axis=-1, keepdims=True),
                                    jnp.max(s2, axis=-1, keepdims=True)), sink)
        p1 = jnp.exp(s1 - m)
        p2 = jnp.exp(s2 - m)
        den = (jnp.sum(p1, axis=-1, keepdims=True) + jnp.sum(p2, axis=-1, keepdims=True)
               + jnp.exp(sink - m))
        o = (jnp.einsum("bqs,bsd->bqd", p1.astype(BF16), vc, preferred_element_type=F32)
             + jnp.einsum("bqs,bsd->bqd", p2.astype(BF16), vn, preferred_element_type=F32))
        o_ref[:, kh] = (o / den).astype(BF16)


def _attn_sample(sinks, q_g, kv_new, ck, cv):
    return pl.pallas_call(
        _attn_sample_kernel,
        out_shape=jax.ShapeDtypeStruct((BS, NKV, QG, HD), BF16),
        grid=(BS // ATT_BB,),
        in_specs=[pl.BlockSpec(memory_space=pltpu.SMEM),
                  pl.BlockSpec((ATT_BB, NKV, QG, HD), lambda i: (i, 0, 0, 0)),
                  pl.BlockSpec((ATT_BB, SS, 2 * KVW), lambda i: (i, 0, 0)),
                  pl.BlockSpec((ATT_BB, WIN, KVW), lambda i: (i, 0, 0)),
                  pl.BlockSpec((ATT_BB, WIN, KVW), lambda i: (i, 0, 0))],
        out_specs=pl.BlockSpec((ATT_BB, NKV, QG, HD), lambda i: (i, 0, 0, 0)),
        compiler_params=_cparams(("parallel",)),
        name="attn_sample",
    )(sinks, q_g, kv_new, ck, cv)


def _merge_kernel(zs_ref, at_ref, wc_ref, wa_ref, bc_ref, ba_ref, gc_ref, ga_ref, o_ref):
    conv = jnp.dot(zs_ref[...], wc_ref[...], preferred_element_type=F32) + bc_ref[...]
    attn = jnp.dot(at_ref[...], wa_ref[...], preferred_element_type=F32) + ba_ref[...]
    o_ref[...] = (gc_ref[...] * conv + ga_ref[...] * attn).astype(BF16)


def _merge(zs, attn, w_co, w_ao, b_co, b_ao, gates):
    tm, tn = 1024, 512
    ga_off = D // tn
    return pl.pallas_call(
        _merge_kernel,
        out_shape=jax.ShapeDtypeStruct((T, D), BF16),
        grid=(T // tm, D // tn),
        in_specs=[pl.BlockSpec((tm, D), lambda i, j: (i, 0)),
                  pl.BlockSpec((tm, D), lambda i, j: (i, 0)),
                  pl.BlockSpec((D, tn), lambda i, j: (0, j)),
                  pl.BlockSpec((D, tn), lambda i, j: (0, j)),
                  pl.BlockSpec((1, tn), lambda i, j: (0, j)),
                  pl.BlockSpec((1, tn), lambda i, j: (0, j)),
                  pl.BlockSpec((tm, tn), lambda i, j: (i, j)),
                  pl.BlockSpec((tm, tn), lambda i, j: (i, ga_off + j))],
        out_specs=pl.BlockSpec((tm, tn), lambda i, j: (i, j)),
        compiler_params=_cparams(("parallel", "arbitrary")),
        name="mixer_merge",
    )(zs, attn, w_co, w_ao, b_co.reshape(1, D), b_ao.reshape(1, D), gates, gates)


def _out_proj_kernel(*refs, aliased):
    if aliased:
        refs = refs[2:]
    mg_ref, x_ref, w_ref, m2_ref, g_ref, m3_ref, m4_ref, x1_ref, h2_ref = refs
    nb, r, _ = x_ref.shape
    y = jnp.dot(mg_ref[...], w_ref[...], preferred_element_type=F32).reshape(nb, r, D)
    x1 = x_ref[...] + m2_ref[...] * y
    x1_ref[...] = x1.reshape(nb * r, D)
    h2_ref[...] = _norm_mod(x1, g_ref[...], m3_ref[...], m4_ref[...]).reshape(nb * r, D)


def _out_proj(path, merged, x3, w_out_b, m2, g2, m3, m4, prev=None):
    tm = path.tm
    aliased = prev is not None
    in_specs, args = [], []
    if aliased:
        in_specs += [pl.BlockSpec(memory_space=pl.ANY)] * 2
        args += list(prev)
    rb0 = path.row_block0
    in_specs += [pl.BlockSpec((tm, D), lambda i: (rb0 + i, 0)),
                 pl.BlockSpec((path.nb, path.r, D), lambda i: path.x_index(i)),
                 pl.BlockSpec((D, D), lambda i: (0, 0)),
                 pl.BlockSpec((path.nb, 1, D), lambda i: path.mod_index(i)),
                 pl.BlockSpec((1, 1, D), lambda i: (0, 0, 0)),
                 pl.BlockSpec((path.nb, 1, D), lambda i: path.mod_index(i)),
                 pl.BlockSpec((path.nb, 1, D), lambda i: path.mod_index(i))]
    args += [merged, x3, w_out_b, m2, g2.reshape(1, 1, D), m3, m4]
    return pl.pallas_call(
        functools.partial(_out_proj_kernel, aliased=aliased),
        out_shape=(jax.ShapeDtypeStruct((T, D), F32), jax.ShapeDtypeStruct((T, D), F32)),
        grid=(path.steps,),
        in_specs=in_specs,
        out_specs=(pl.BlockSpec((tm, D), lambda i: (rb0 + i, 0)),
                   pl.BlockSpec((tm, D), lambda i: (rb0 + i, 0))),
        input_output_aliases=({0: 0, 1: 1} if aliased else {}),
        compiler_params=_cparams(("parallel",)),
        name="out_proj_norm2",
    )(*args)


ROUTE_TR = 512


def _router_kernel(h_ref, wt_ref, b_ref, tri_ref, idx_ref, gate_ref, rank_ref, cnt_ref, carry):
    @pl.when(pl.program_id(0) == 0)
    def _():
        carry[...] = jnp.zeros_like(carry)

    h = h_ref[...]
    w = wt_ref[...]
    h_hi = h.astype(BF16)
    h_lo = (h - h_hi.astype(F32)).astype(BF16)
    w_hi = w.astype(BF16)
    w_lo = (w - w_hi.astype(F32)).astype(BF16)
    dn = (((1,), (1,)), ((), ()))
    logits = (lax.dot_general(w_hi, h_hi, dn, preferred_element_type=F32)
              + lax.dot_general(w_hi, h_lo, dn, preferred_element_type=F32)
              + lax.dot_general(w_lo, h_hi, dn, preferred_element_type=F32)) + b_ref[...]
    eid = lax.broadcasted_iota(jnp.int32, (NE, ROUTE_TR), 0).astype(F32)
    work = logits
    vals, ids = [], []
    onehot = jnp.zeros((NE, ROUTE_TR), F32)
    for _ in range(TOPK):
        m = jnp.max(work, axis=0, keepdims=True)
        sel = jnp.min(jnp.where(work == m, eid, float(NE)), axis=0, keepdims=True)
        hit = eid == sel
        work = jnp.where(hit, -jnp.inf, work)
        onehot = jnp.where(hit, 1.0, onehot)
        vals.append(m)
        ids.append(sel)
    es = [jnp.exp(v - vals[0]) for v in vals]
    den = es[0] + es[1] + es[2] + es[3]
    before = jnp.dot(onehot.astype(BF16), tri_ref[...], preferred_element_type=F32) + carry[:, 0:1]
    for k in range(TOPK):
        idx_ref[k:k + 1, :] = ids[k].astype(jnp.int32)
        gate_ref[k:k + 1, :] = es[k] / den
        rank_ref[k:k + 1, :] = jnp.sum(jnp.where(eid == ids[k], before, 0.0), axis=0,
                                       keepdims=True).astype(jnp.int32)
    carry[...] = carry[...] + jnp.sum(onehot, axis=1, keepdims=True)
    cnt_ref[...] = carry[...]


def _router(h2, w_router, b_router):
    tri = jnp.triu(jnp.ones((ROUTE_TR, ROUTE_TR), F32), 1).astype(BF16)
    row = lambda i: (0, i)
    return pl.pallas_call(
        _router_kernel,
        out_shape=(jax.ShapeDtypeStruct((TOPK, T), jnp.int32),
                   jax.ShapeDtypeStruct((TOPK, T), F32),
                   jax.ShapeDtypeStruct((TOPK, T), jnp.int32),
                   jax.ShapeDtypeStruct((NE, 128), F32)),
        grid=(T // ROUTE_TR,),
        in_specs=[pl.BlockSpec((ROUTE_TR, D), lambda i: (i, 0)),
                  pl.BlockSpec((NE, D), lambda i: (0, 0)),
                  pl.BlockSpec((NE, 1), lambda i: (0, 0)),
                  pl.BlockSpec((ROUTE_TR, ROUTE_TR), lambda i: (0, 0))],
        out_specs=(pl.BlockSpec((TOPK, ROUTE_TR), row),
                   pl.BlockSpec((TOPK, ROUTE_TR), row),
                   pl.BlockSpec((TOPK, ROUTE_TR), row),
                   pl.BlockSpec((NE, 128), lambda i: (0, 0))),
        scratch_shapes=[pltpu.VMEM((NE, 128), F32)],
        compiler_params=_cparams(("arbitrary",)),
        name="router_top4",
    )(h2, w_router.T, b_router.reshape(NE, 1), tri)


def _row_gather_copy(src_hbm, idx_ref, base, r, buf, slot, sem):
    return pltpu.make_async_copy(src_hbm.at[pl.ds(idx_ref[base + r], 1)],
                                 buf.at[slot, pl.ds(r, 1)], sem.at[slot])


def _dispatch_kernel(tok_ref, nt_ref, h_hbm, o_ref, buf, sem):
    i = pl.program_id(0)
    nt = nt_ref[0]

    def issue(tile, slot):
        def body(r, c):
            _row_gather_copy(h_hbm, tok_ref, tile * SUB, r, buf, slot, sem).start()
            return c
        lax.fori_loop(0, SUB, body, 0)

    @pl.when(i == 0)
    def _():
        issue(0, 0)

    @pl.when(i + 1 < nt)
    def _():
        issue(i + 1, (i + 1) % 2)

    @pl.when(i < nt)
    def _():
        slot = i % 2
        pltpu.make_async_copy(h_hbm.at[pl.ds(0, SUB)], buf.at[slot], sem.at[slot]).wait()
        o_ref[...] = buf[slot].astype(BF16)

    @pl.when(i >= nt)
    def _():
        o_ref[...] = jnp.zeros_like(o_ref)


def _dispatch(tok_of_slot, nt_used, h2):
    return pl.pallas_call(
        _dispatch_kernel,
        out_shape=jax.ShapeDtypeStruct((P_MAX, D), BF16),
        grid_spec=pltpu.PrefetchScalarGridSpec(
            num_scalar_prefetch=2,
            grid=(NT_MAX,),
            in_specs=[pl.BlockSpec(memory_space=pl.ANY)],
            out_specs=pl.BlockSpec((SUB, D), lambda i, tok, nt: (i, 0)),
            scratch_shapes=[pltpu.VMEM((2, SUB, D), F32), pltpu.SemaphoreType.DMA((2,))]),
        compiler_params=_cparams(("arbitrary",)),
        name="moe_dispatch",
    )(tok_of_slot, nt_used, h2)


def _moe_kernel(ge_ref, gs_ref, gn_ref, w1_ref, b1_ref, w2_ref, b2_ref, sel_ref, xs_hbm, y_hbm,
                x_buf, acc, w1b, w2b, sem_in, sem_out):
    del ge_ref
    g = pl.program_id(0)
    c = pl.program_id(1)
    nsub = gn_ref[g]
    start = gs_ref[g]

    def x_copy(r):
        return pltpu.make_async_copy(xs_hbm.at[pl.ds((start + r) * SUB, SUB)],
                                     x_buf.at[pl.ds(r * SUB, SUB)], sem_in)

    def y_copy(r):
        return pltpu.make_async_copy(acc.at[pl.ds(r * SUB, SUB)],
                                     y_hbm.at[pl.ds((start + r) * SUB, SUB)], sem_out)

    def for_subtiles(fn):
        def body(r, carry):
            fn(r)
            return carry
        lax.fori_loop(0, nsub, body, 0)

    @pl.when(nsub > 0)
    def _():
        @pl.when(c == 0)
        def _():
            for_subtiles(lambda r: x_copy(r).start())
            for_subtiles(lambda r: x_copy(r).wait())

        w1b[...] = w1_ref[0].astype(BF16)
        w2b[...] = w2_ref[0].astype(BF16)

        def sub(r):
            rows = pl.ds(pl.multiple_of(r * SUB, SUB), SUB)
            h = jnp.dot(x_buf[rows, :], w1b[...], preferred_element_type=F32) + b1_ref[0]
            hn = jnp.concatenate(
                [pltpu.roll(h[:, k * 128:(k + 1) * 128], 127, 1) for k in range(2 * TF // 128)], axis=1)
            glu = jnp.minimum(h, LIMIT)
            lin = jnp.clip(hn, -LIMIT, LIMIT)
            act = (glu * _sigmoid(ALPHA * glu) * (lin + 1.0)).astype(BF16)
            act = jnp.dot(act, sel_ref[...], preferred_element_type=F32).astype(BF16)
            contrib = jnp.dot(act, w2b[...], preferred_element_type=F32)

            @pl.when(c == 0)
            def _():
                acc[rows, :] = contrib + b2_ref[0]

            @pl.when(c > 0)
            def _():
                acc[rows, :] = acc[rows, :] + contrib

        for_subtiles(sub)

        @pl.when(c == NC - 1)
        def _():
            for_subtiles(lambda r: y_copy(r).start())
            for_subtiles(lambda r: y_copy(r).wait())


def _moe(g_expert, g_start, g_nsub, w1, b1, w2, b2, xs):
    sel = (jnp.arange(2 * TF)[:, None] == 2 * jnp.arange(TF)[None, :]).astype(BF16)

    def cidx(g, c, gn):
        return jnp.where(gn[g] > 0, c, NC - 1)

    return pl.pallas_call(
        _moe_kernel,
        out_shape=jax.ShapeDtypeStruct((P_MAX, D), F32),
        grid_spec=pltpu.PrefetchScalarGridSpec(
            num_scalar_prefetch=3,
            grid=(G_MAX, NC),
            in_specs=[pl.BlockSpec((1, D, 2 * TF), lambda g, c, ge, gs, gn: (ge[g], 0, cidx(g, c, gn))),
                      pl.BlockSpec((1, 1, 2 * TF), lambda g, c, ge, gs, gn: (ge[g], 0, cidx(g, c, gn))),
                      pl.BlockSpec((1, TF, D), lambda g, c, ge, gs, gn: (ge[g], cidx(g, c, gn), 0)),
                      pl.BlockSpec((1, 1, D), lambda g, c, ge, gs, gn: (ge[g], 0, 0)),
                      pl.BlockSpec((2 * TF, TF), lambda g, c, ge, gs, gn: (0, 0)),
                      pl.BlockSpec(memory_space=pl.ANY)],
            out_specs=pl.BlockSpec(memory_space=pl.ANY),
            scratch_shapes=[pltpu.VMEM((GSUB * SUB, D), BF16),
                            pltpu.VMEM((GSUB * SUB, D), F32),
                            pltpu.VMEM((D, 2 * TF), BF16),
                            pltpu.VMEM((TF, D), BF16),
                            pltpu.SemaphoreType.DMA,
                            pltpu.SemaphoreType.DMA]),
        compiler_params=_cparams(("arbitrary", "arbitrary")),
        name="moe_experts",
    )(g_expert, g_start, g_nsub, w1, b1.reshape(NE, 1, 2 * DFF), w2, b2.reshape(NE, 1, D), sel, xs)


COMB_TM = 128


def _combine_kernel(slot_ref, x1_ref, gate_ref, m5_ref, g_ref, y_hbm, o_ref, buf, sem, *, tok0):
    i = pl.program_id(0)
    n = pl.num_programs(0)

    def issue(tile, slot):
        for k in range(TOPK):
            def body(r, c, k=k):
                s = slot_ref[k * T + tok0 + tile * COMB_TM + r]
                pltpu.make_async_copy(y_hbm.at[pl.ds(s, 1)], buf.at[slot, k, pl.ds(r, 1)],
                                      sem.at[slot]).start()
                return c
            lax.fori_loop(0, COMB_TM, body, 0)

    @pl.when(i == 0)
    def _():
        issue(0, 0)

    @pl.when(i + 1 < n)
    def _():
        issue(i + 1, (i + 1) % 2)

    slot = i % 2
    for k in range(TOPK):
        pltpu.make_async_copy(y_hbm.at[pl.ds(0, COMB_TM)], buf.at[slot, k], sem.at[slot]).wait()
    gates = gate_ref[...]
    f = None
    for k in range(TOPK):
        term = gates[:, k:k + 1] * buf[slot, k]
        f = term if f is None else f + term
    nb, r, _ = o_ref.shape
    x2 = x1_ref[...].reshape(nb, r, D) + m5_ref[...] * f.reshape(nb, r, D)
    ms = jnp.mean(x2 * x2, axis=-1, keepdims=True)
    o_ref[...] = x2 * lax.rsqrt(ms + EPS) * g_ref[...]


def _combine(path, slots, x1, gates_t, m5, gf, y, out_shape3):
    tm = path.tm
    rb0 = path.row_block0
    return pl.pallas_call(
        functools.partial(_combine_kernel, tok0=rb0 * tm),
        out_shape=jax.ShapeDtypeStruct(out_shape3, F32),
        grid_spec=pltpu.PrefetchScalarGridSpec(
            num_scalar_prefetch=1,
            grid=(path.steps,),
            in_specs=[pl.BlockSpec((tm, D), lambda i, s: (rb0 + i, 0)),
                      pl.BlockSpec((tm, TOPK), lambda i, s: (rb0 + i, 0)),
                      pl.BlockSpec((path.nb, 1, D), lambda i, s: path.mod_index(i)),
                      pl.BlockSpec((1, 1, D), lambda i, s: (0, 0, 0)),
                      pl.BlockSpec(memory_space=pl.ANY)],
            out_specs=pl.BlockSpec((path.nb, path.r, D), lambda i, s: path.x_index(i)),
            scratch_shapes=[pltpu.VMEM((2, TOPK, tm, D), F32), pltpu.SemaphoreType.DMA((2,))]),
        compiler_params=_cparams(("arbitrary",)),
        name="moe_combine_norm",
    )(slots, x1, gates_t, m5, gf.reshape(1, 1, D), y)


def _routing_tables(idx_t, rank_t, counts):
    tiles = (counts + SUB - 1) // SUB
    tile_end = jnp.cumsum(tiles)
    tile_start = tile_end - tiles
    nt_used = tile_end[-1:]
    slot = tile_start[idx_t] * SUB + rank_t
    tok = jnp.tile(jnp.arange(T, dtype=jnp.int32), TOPK)
    tok_of_slot = jnp.zeros((P_MAX,), jnp.int32).at[slot.reshape(-1)].set(tok)
    ngrp = (tiles + GSUB - 1) // GSUB
    grp_end = jnp.cumsum(ngrp)
    grp_start = grp_end - ngrp
    n_groups = grp_end[-1]
    gid = jnp.arange(G_MAX, dtype=jnp.int32)
    last = jnp.maximum(n_groups - 1, 0)
    gclamp = jnp.minimum(gid, last)
    g_expert = jnp.searchsorted(grp_end, gclamp, side="right").astype(jnp.int32)
    local = gclamp - grp_start[g_expert]
    g_start = tile_start[g_expert] + local * GSUB
    g_nsub = jnp.where(gid < n_groups, jnp.minimum(GSUB, tiles[g_expert] - local * GSUB), 0)
    return (slot.reshape(-1).astype(jnp.int32), tok_of_slot, nt_used.astype(jnp.int32),
            g_expert, g_start.astype(jnp.int32), g_nsub.astype(jnp.int32))


def kernel(x_prompt, x_sample, c_prompt, c_sample, cache_k, cache_v, state_conv, w_ada, b_ada, norm1_g,
           w_in, b_in, conv_dw_w, conv_dw_b, conv_ln_g, conv_ln_b, w_conv_out, b_conv_out, sinks,
           w_attn_out, b_attn_out, w_out, norm2_g, w_router, b_router, w_mlp1, b_mlp1, w_mlp2, b_mlp2,
           norm_f_g):
    mods = _modulations(jnp.concatenate([c_prompt, c_sample], axis=0), w_ada, b_ada)
    mp = [mods[:BP, k * D:(k + 1) * D].reshape(BP, 1, D) for k in range(6)]
    ms = [mods[BP:, k * D:(k + 1) * D].reshape(BS, 1, D) for k in range(6)]

    w_in_b = w_in.astype(BF16)
    b_in2 = b_in.reshape(1, IN_W)
    pp, ps = _prompt_path(1024), _sample_path(1024)

    def proj(col_a, col_b, n_cols, mode, dtype, scale=1.0):
        o = _in_proj(pp, x_prompt, norm1_g, mp[0], mp[1], w_in_b, b_in2, col_a, col_b, n_cols, mode,
                     dtype, scale)
        return _in_proj(ps, x_sample, norm1_g, ms[0], ms[1], w_in_b, b_in2, col_a, col_b, n_cols, mode,
                        dtype, scale, prev=o)

    u_all = proj(OFF_GLU_A, OFF_GLU_B, D, "glu", F32)
    q_all = proj(OFF_Q, None, D, "plain", BF16, HD ** -0.5)
    kv_all = proj(OFF_KV, None, 2 * KVW, "plain", F32)
    gates = proj(OFF_GC, None, 2 * D, "sigmoid", F32)

    u_p = u_all[:TP].reshape(BP, SP, D)
    u_s = u_all[TP:].reshape(BS, SS, D)
    up_s = jnp.concatenate([state_conv, u_s], axis=1)
    zs = _conv_prompt(u_all, conv_dw_w, conv_dw_b, conv_ln_g, conv_ln_b)
    zs = _conv_sample(zs, up_s, conv_dw_w, conv_dw_b, conv_ln_g, conv_ln_b)
    conv_p = u_p[:, SP - (CW - 1):]
    conv_s = up_s[:, SS:]

    attn = _attn_prompt(sinks, q_all, kv_all)
    q_g = (q_all[TP:].reshape(BS, SS, NKV, GRP, HD).transpose(0, 2, 3, 1, 4)
           .reshape(BS, NKV, QG, HD))
    kv_s = kv_all[TP:].reshape(BS, SS, 2 * KVW)
    ck = cache_k.reshape(BS, WIN, KVW)
    cv = cache_v.reshape(BS, WIN, KVW)
    attn_s = _attn_sample(sinks, q_g, kv_s, ck, cv)
    attn_s = (attn_s.reshape(BS, NKV, GRP, SS, HD).transpose(0, 3, 1, 2, 4).reshape(TS, D))
    attn = lax.dynamic_update_slice(attn, attn_s, (TP, 0))
    kv_p = kv_all[:TP].reshape(BP, SP, 2 * KVW)
    k_p = kv_p[:, SP - WIN:, :KVW].reshape(BP, WIN, NKV, HD)
    v_p = kv_p[:, SP - WIN:, KVW:].reshape(BP, WIN, NKV, HD)
    k_s = jnp.concatenate([ck[:, SS:], kv_s[:, :, :KVW]], axis=1).reshape(BS, WIN, NKV, HD)
    v_s = jnp.concatenate([cv[:, SS:], kv_s[:, :, KVW:]], axis=1).reshape(BS, WIN, NKV, HD)

    merged = _merge(zs, attn, w_conv_out.astype(BF16), w_attn_out.astype(BF16), b_conv_out, b_attn_out,
                    gates)

    op, os_ = _prompt_path(256), _sample_path(256)
    w_out_b = w_out.astype(BF16)
    x1h2 = _out_proj(op, merged, x_prompt, w_out_b, mp[2], norm2_g, mp[3], mp[4])
    x1, h2 = _out_proj(os_, merged, x_sample, w_out_b, ms[2], norm2_g, ms[3], ms[4], prev=x1h2)

    idx_t, gate_t, rank_t, cnt = _router(h2, w_router, b_router)
    counts = cnt[:, 0].astype(jnp.int32)
    slots, tok_of_slot, nt_used, g_expert, g_start, g_nsub = _routing_tables(idx_t, rank_t, counts)

    xs = _dispatch(tok_of_slot, nt_used, h2)
    y = _moe(g_expert, g_start, g_nsub, w_mlp1, b_mlp1, w_mlp2, b_mlp2, xs)

    gates_tok = gate_t.T
    cp, cs = _prompt_path(COMB_TM), _sample_path(COMB_TM)
    y_prompt = _combine(cp, slots, x1, gates_tok, mp[5], norm_f_g, y, (BP, SP, D))
    y_sample = _combine(cs, slots, x1, gates_tok, ms[5], norm_f_g, y, (BS, SS, D))
    return (y_prompt, y_sample, k_p, v_p, conv_p, k_s, v_s, conv_s)
```

```python
import functools

import jax
import jax.numpy as jnp
from jax import lax
from jax.experimental import pallas as pl
from jax.experimental.pallas import tpu as pltpu

F32 = jnp.float32
BF16 = jnp.bfloat16

D = 2048
BP, SP = 4, 2048
BS, SS = 128, 8
TP = BP * SP
TS = BS * SS
T = TP + TS
HD = 64
NH = 32
NKV = 4
GRP = NH // NKV
WIN = 128
KVW = NKV * HD
CW = 31
NE = 32
TOPK = 4
DFF = 2048
ALPHA = 1.702
LIMIT = 7.0
EPS = 1e-5
OFF_GLU_A, OFF_GLU_B, OFF_Q, OFF_KV, OFF_GC = 0, D, 2 * D, 3 * D, 3 * D + 2 * KVW
IN_W = 5 * D + 2 * KVW

VMEM_LIMIT = 56 * 1024 * 1024

SUB = 256
GSUB = 8
NT_MAX = (T * TOPK) // SUB + NE
P_MAX = NT_MAX * SUB
G_MAX = -(-((T * TOPK) // SUB) // GSUB) + NE
TF = 256
NC = DFF // TF
UNROLL = 8


def _cparams(sem):
    return pltpu.CompilerParams(dimension_semantics=sem, vmem_limit_bytes=VMEM_LIMIT)


def _sigmoid(x):
    return 1.0 / (1.0 + jnp.exp(-x))


def _norm_mod(x, g, shift, scale):
    ms = jnp.mean(x * x, axis=-1, keepdims=True)
    return (x * lax.rsqrt(ms + EPS) * g) * (1.0 + scale) + shift


class _Tiles:
    def __init__(self, tm):
        self.tm = tm
        self.per_b = SP // tm
        self.n_prompt = TP // tm
        self.n = T // tm
        self.sb = tm // SS

    def prompt_x(self, i):
        ip = jnp.minimum(i, self.n_prompt - 1)
        return (ip // self.per_b, ip % self.per_b, 0)

    def prompt_mod(self, i):
        return (jnp.minimum(i, self.n_prompt - 1) // self.per_b, 0, 0)

    def sample_x(self, i):
        return (jnp.maximum(i - self.n_prompt, 0), 0, 0)

    def specs(self, tail=()):
        wrap = lambda f: (lambda i, *_: f(i))
        xp = pl.BlockSpec((1, self.tm, D), wrap(self.prompt_x))
        xs = pl.BlockSpec((self.sb, SS, D), wrap(self.sample_x))
        mp = pl.BlockSpec((1, 1, D), wrap(self.prompt_mod))
        ms = pl.BlockSpec((self.sb, 1, D), wrap(self.sample_x))
        return xp, xs, mp, ms


def _mod_kernel(c_ref, w_ref, b_ref, o_ref):
    c = c_ref[...]
    sc = (c * _sigmoid(c)).astype(BF16)
    o_ref[...] = jnp.dot(sc, w_ref[...].astype(BF16), preferred_element_type=F32) + b_ref[...]


def _modulations(c_all, w_ada, b_ada):
    nb = c_all.shape[0]
    tn = 1024
    return pl.pallas_call(
        _mod_kernel,
        out_shape=jax.ShapeDtypeStruct((nb, 6 * D), F32),
        grid=(6 * D // tn,),
        in_specs=[pl.BlockSpec((nb, D), lambda j: (0, 0)),
                  pl.BlockSpec((D, tn), lambda j: (0, j)),
                  pl.BlockSpec((1, tn), lambda j: (0, j))],
        out_specs=pl.BlockSpec((nb, tn), lambda j: (0, j)),
        compiler_params=_cparams(("parallel",)),
        name="modulations",
    )(c_all, w_ada, b_ada.reshape(1, -1))


def _norm1_kernel(xp_ref, xs_ref, g_ref, shp_ref, scp_ref, shs_ref, scs_ref, o_ref, *, n_prompt):
    i = pl.program_id(0)

    @pl.when(i < n_prompt)
    def _():
        h = _norm_mod(xp_ref[...], g_ref[...], shp_ref[...], scp_ref[...])
        o_ref[...] = h.reshape(o_ref.shape).astype(BF16)

    @pl.when(i >= n_prompt)
    def _():
        h = _norm_mod(xs_ref[...], g_ref[...], shs_ref[...], scs_ref[...])
        o_ref[...] = h.reshape(o_ref.shape).astype(BF16)


def _norm1(x_prompt, x_sample, g, mp, ms):
    tl = _Tiles(256)
    xp, xs, mps, mss = tl.specs()
    return pl.pallas_call(
        functools.partial(_norm1_kernel, n_prompt=tl.n_prompt),
        out_shape=jax.ShapeDtypeStruct((T, D), BF16),
        grid=(tl.n,),
        in_specs=[xp, xs, pl.BlockSpec((1, 1, D), lambda i: (0, 0, 0)), mps, mps, mss, mss],
        out_specs=pl.BlockSpec((tl.tm, D), lambda i: (i, 0)),
        compiler_params=_cparams(("parallel",)),
        name="norm1_adaln",
    )(x_prompt, x_sample, g.reshape(1, 1, D), mp[0], mp[1], ms[0], ms[1])


def _in_proj_kernel(*refs, mode, out_scale):
    if mode == "glu":
        h_ref, wa_ref, wb_ref, ba_ref, bb_ref, o_ref = refs
    else:
        h_ref, wa_ref, ba_ref, o_ref = refs
    h = h_ref[...]
    a = jnp.dot(h, wa_ref[...], preferred_element_type=F32) + ba_ref[...]
    if mode == "glu":
        b = jnp.dot(h, wb_ref[...], preferred_element_type=F32) + bb_ref[...]
        a = a * _sigmoid(b)
    elif mode == "sigmoid":
        a = _sigmoid(a)
    elif out_scale != 1.0:
        a = a * out_scale
    o_ref[...] = a.astype(o_ref.dtype)


def _in_proj(h, w_b, b_in2, col_a, col_b, n_cols, mode, out_dtype, out_scale=1.0):
    tm, tn = 1024, 512
    ca, cb = col_a // tn, (col_b // tn if col_b is not None else 0)
    in_specs = [pl.BlockSpec((tm, D), lambda i, j: (i, 0)),
                pl.BlockSpec((D, tn), lambda i, j: (0, ca + j))]
    args = [h, w_b]
    if mode == "glu":
        in_specs.append(pl.BlockSpec((D, tn), lambda i, j: (0, cb + j)))
        args.append(w_b)
    in_specs.append(pl.BlockSpec((1, tn), lambda i, j: (0, ca + j)))
    args.append(b_in2)
    if mode == "glu":
        in_specs.append(pl.BlockSpec((1, tn), lambda i, j: (0, cb + j)))
        args.append(b_in2)
    return pl.pallas_call(
        functools.partial(_in_proj_kernel, mode=mode, out_scale=out_scale),
        out_shape=jax.ShapeDtypeStruct((T, n_cols), out_dtype),
        grid=(T // tm, n_cols // tn),
        in_specs=in_specs,
        out_specs=pl.BlockSpec((tm, tn), lambda i, j: (i, j)),
        compiler_params=_cparams(("parallel", "arbitrary")),
        name="in_proj_" + mode,
    )(*args)


CONV_TL = 128
HALO = 32
CONV_STEPS = TP // CONV_TL


def _ln_swish(z, g, b):
    mu = jnp.mean(z, axis=-1, keepdims=True)
    d = z - mu
    var = jnp.mean(d * d, axis=-1, keepdims=True)
    zn = d * lax.rsqrt(var + EPS) * g + b
    return zn * _sigmoid(zn)


def _conv_prompt_kernel(u_ref, halo_ref, w_ref, dwb_ref, g_ref, b_ref, o_ref, win, z_scr):
    i = pl.program_id(0)

    @pl.when(i < CONV_STEPS)
    def _():
        first = (i % (SP // CONV_TL)) == 0
        win[0:HALO, :] = jnp.where(first, 0.0, halo_ref[...])
        win[HALO:HALO + CONV_TL, :] = u_ref[...]
        base = HALO - (CW - 1)
        for s in range(D // 128):
            cs = slice(s * 128, (s + 1) * 128)
            acc = None
            for j in range(CW):
                term = win[base + j:base + j + CONV_TL, cs] * w_ref[j:j + 1, cs]
                acc = term if acc is None else acc + term
            z_scr[:, cs] = acc + dwb_ref[:, cs]
        o_ref[...] = _ln_swish(z_scr[...], g_ref[...], b_ref[...]).astype(BF16)

    @pl.when(i >= CONV_STEPS)
    def _():
        o_ref[...] = jnp.zeros_like(o_ref)


def _conv_prompt(u_all, dw_w, dw_b, ln_g, ln_b):
    hb = CONV_TL // HALO
    last = CONV_STEPS - 1
    const = lambda i: (0, 0)
    return pl.pallas_call(
        _conv_prompt_kernel,
        out_shape=jax.ShapeDtypeStruct((T, D), BF16),
        grid=(T // CONV_TL,),
        in_specs=[pl.BlockSpec((CONV_TL, D), lambda i: (jnp.minimum(i, last), 0)),
                  pl.BlockSpec((HALO, D), lambda i: (jnp.maximum(jnp.minimum(i, last) * hb - 1, 0), 0)),
                  pl.BlockSpec((CW, D), const),
                  pl.BlockSpec((1, D), const),
                  pl.BlockSpec((1, D), const),
                  pl.BlockSpec((1, D), const)],
        out_specs=pl.BlockSpec((CONV_TL, D), lambda i: (i, 0)),
        scratch_shapes=[pltpu.VMEM((HALO + CONV_TL, D), F32), pltpu.VMEM((CONV_TL, D), F32)],
        compiler_params=_cparams(("parallel",)),
        name="conv_prompt",
    )(u_all, u_all, dw_w, dw_b.reshape(1, D), ln_g.reshape(1, D), ln_b.reshape(1, D))


CONV_BB = 8


def _conv_sample_kernel(prev_ref, up_ref, w_ref, dwb_ref, g_ref, b_ref, o_ref, z_scr):
    del prev_ref
    for s in range(D // 128):
        cs = slice(s * 128, (s + 1) * 128)
        acc = None
        for j in range(CW):
            term = up_ref[:, j:j + SS, cs] * w_ref[j:j + 1, cs][None]
            acc = term if acc is None else acc + term
        z_scr[:, :, cs] = acc + dwb_ref[:, cs][None]
    zs = _ln_swish(z_scr[...], g_ref[...][None], b_ref[...][None])
    o_ref[...] = zs.reshape(CONV_BB * SS, D).astype(BF16)


def _conv_sample(prev, up_s, dw_w, dw_b, ln_g, ln_b):
    rb0 = TP // (CONV_BB * SS)
    return pl.pallas_call(
        _conv_sample_kernel,
        out_shape=jax.ShapeDtypeStruct((T, D), BF16),
        grid=(BS // CONV_BB,),
        in_specs=[pl.BlockSpec(memory_space=pl.ANY),
                  pl.BlockSpec((CONV_BB, CW - 1 + SS, D), lambda i: (i, 0, 0)),
                  pl.BlockSpec((CW, D), lambda i: (0, 0)),
                  pl.BlockSpec((1, D), lambda i: (0, 0)),
                  pl.BlockSpec((1, D), lambda i: (0, 0)),
                  pl.BlockSpec((1, D), lambda i: (0, 0))],
        out_specs=pl.BlockSpec((CONV_BB * SS, D), lambda i: (rb0 + i, 0)),
        scratch_shapes=[pltpu.VMEM((CONV_BB, SS, D), F32)],
        input_output_aliases={0: 0},
        compiler_params=_cparams(("parallel",)),
        name="conv_sample",
    )(prev, up_s, dw_w, dw_b.reshape(1, D), ln_g.reshape(1, D), ln_b.reshape(1, D))


ATT_STEPS = TP // WIN


def _attn_prompt_kernel(sink_ref, q_ref, kvc_ref, kvp_ref, o_ref):
    i = pl.program_id(0)

    @pl.when(i < ATT_STEPS)
    def _():
        n = i % (SP // WIN)
        row = lax.broadcasted_iota(jnp.int32, (WIN, 2 * WIN), 0)
        col = lax.broadcasted_iota(jnp.int32, (WIN, 2 * WIN), 1)
        diff = row + WIN - col
        first_key = jnp.where(n > 0, 0, WIN)
        mask = (diff >= 0) & (diff < WIN) & (col >= first_key)
        kvc = kvc_ref[...].astype(BF16)
        kvp = kvp_ref[...].astype(BF16)
        outs = []
        for kh in range(NKV):
            k2 = jnp.concatenate([kvp[:, kh * HD:(kh + 1) * HD], kvc[:, kh * HD:(kh + 1) * HD]], axis=0)
            v2 = jnp.concatenate([kvp[:, KVW + kh * HD:KVW + (kh + 1) * HD],
                                  kvc[:, KVW + kh * HD:KVW + (kh + 1) * HD]], axis=0)
            for g in range(GRP):
                h = kh * GRP + g
                sink = sink_ref[h]
                qh = q_ref[:, h * HD:(h + 1) * HD]
                s = lax.dot_general(qh, k2, (((1,), (1,)), ((), ())), preferred_element_type=F32)
                s = jnp.where(mask, s, -jnp.inf)
                m = jnp.maximum(jnp.max(s, axis=-1, keepdims=True), sink)
                p = jnp.exp(s - m)
                den = jnp.sum(p, axis=-1, keepdims=True) + jnp.exp(sink - m)
                o = jnp.dot(p.astype(BF16), v2, preferred_element_type=F32)
                outs.append(o / den)
        o_ref[...] = jnp.concatenate(outs, axis=-1).astype(BF16)

    @pl.when(i >= ATT_STEPS)
    def _():
        o_ref[...] = jnp.zeros_like(o_ref)


def _attn_prompt(sinks, q_all, kv_all):
    last = ATT_STEPS - 1
    nblk = SP // WIN

    def prev_block(i):
        ic = jnp.minimum(i, last)
        return (jnp.where(ic % nblk == 0, ic, ic - 1), 0)

    return pl.pallas_call(
        _attn_prompt_kernel,
        out_shape=jax.ShapeDtypeStruct((T, D), BF16),
        grid=(T // WIN,),
        in_specs=[pl.BlockSpec(memory_space=pltpu.SMEM),
                  pl.BlockSpec((WIN, D), lambda i: (jnp.minimum(i, last), 0)),
                  pl.BlockSpec((WIN, 2 * KVW), lambda i: (jnp.minimum(i, last), 0)),
                  pl.BlockSpec((WIN, 2 * KVW), prev_block)],
        out_specs=pl.BlockSpec((WIN, D), lambda i: (i, 0)),
        compiler_params=_cparams(("parallel",)),
        name="attn_prompt",
    )(sinks, q_all, kv_all, kv_all)


ATT_BB = 8
QG = SS * GRP


def _attn_sample_kernel(sink_ref, q_ref, kvn_ref, ck_ref, cv_ref, o_ref):
    rr = lax.broadcasted_iota(jnp.int32, (QG, WIN), 0)
    qi_c = rr % SS
    key_c = lax.broadcasted_iota(jnp.int32, (QG, WIN), 1)
    mask_c = (key_c > qi_c)[None]
    rr_n = lax.broadcasted_iota(jnp.int32, (QG, SS), 0)
    key_n = lax.broadcasted_iota(jnp.int32, (QG, SS), 1)
    mask_n = (key_n <= rr_n % SS)[None]
    grow = lax.broadcasted_iota(jnp.int32, (QG, 1), 0) // SS
    for kh in range(NKV):
        sink = jnp.zeros((QG, 1), F32)
        for g in range(GRP):
            sink = jnp.where(grow == g, sink_ref[kh * GRP + g], sink)
        sink = sink[None]
        q = q_ref[:, kh]
        kc = ck_ref[:, :, kh * HD:(kh + 1) * HD].astype(BF16)
        vc = cv_ref[:, :, kh * HD:(kh + 1) * HD].astype(BF16)
        kn = kvn_ref[:, :, kh * HD:(kh + 1) * HD].astype(BF16)
        vn = kvn_ref[:, :, KVW + kh * HD:KVW + (kh + 1) * HD].astype(BF16)
        s1 = jnp.einsum("bqd,bsd->bqs", q, kc, preferred_element_type=F32)
        s2 = jnp.einsum("bqd,bsd->bqs", q, kn, preferred_element_type=F32)
        s1 = jnp.where(mask_c, s1, -jnp.inf)
        s2 = jnp.where(mask_n, s2, -jnp.inf)
        m = jnp.maximum(jnp.maximum(jnp.max(s1, axis=-1, keepdims=True),
                                    jnp.max(s2, axis=-1, keepdims=True)), sink)
        p1 = jnp.exp(s1 - m)
        p2 = jnp.exp(s2 - m)
        den = (jnp.sum(p1, axis=-1, keepdims=True) + jnp.sum(p2, axis=-1, keepdims=True)
               + jnp.exp(sink - m))
        o = (jnp.einsum("bqs,bsd->bqd", p1.astype(BF16), vc, preferred_element_type=F32)
             + jnp.einsum("bqs,bsd->bqd", p2.astype(BF16), vn, preferred_element_type=F32))
        o_ref[:, kh] = (o / den).astype(BF16)


def _attn_sample(sinks, q_g, kv_new, ck, cv):
    return pl.pallas_call(
        _attn_sample_kernel,
        out_shape=jax.ShapeDtypeStruct((BS, NKV, QG, HD), BF16),
        grid=(BS // ATT_BB,),
        in_specs=[pl.BlockSpec(memory_space=pltpu.SMEM),
                  pl.BlockSpec((ATT_BB, NKV, QG, HD), lambda i: (i, 0, 0, 0)),
                  pl.BlockSpec((ATT_BB, SS, 2 * KVW), lambda i: (i, 0, 0)),
                  pl.BlockSpec((ATT_BB, WIN, KVW), lambda i: (i, 0, 0)),
                  pl.BlockSpec((ATT_BB, WIN, KVW), lambda i: (i, 0, 0))],
        out_specs=pl.BlockSpec((ATT_BB, NKV, QG, HD), lambda i: (i, 0, 0, 0)),
        compiler_params=_cparams(("parallel",)),
        name="attn_sample",
    )(sinks, q_g, kv_new, ck, cv)


def _merge_kernel(zs_ref, at_ref, wc_ref, wa_ref, bc_ref, ba_ref, gc_ref, ga_ref, o_ref):
    conv = jnp.dot(zs_ref[...], wc_ref[...], preferred_element_type=F32) + bc_ref[...]
    attn = jnp.dot(at_ref[...], wa_ref[...], preferred_element_type=F32) + ba_ref[...]
    o_ref[...] = (gc_ref[...] * conv + ga_ref[...] * attn).astype(BF16)


def _merge(zs, attn, w_co, w_ao, b_co, b_ao, gates):
    tm, tn = 1024, 512
    ga_off = D // tn
    return pl.pallas_call(
        _merge_kernel,
        out_shape=jax.ShapeDtypeStruct((T, D), BF16),
        grid=(T // tm, D // tn),
        in_specs=[pl.BlockSpec((tm, D), lambda i, j: (i, 0)),
                  pl.BlockSpec((tm, D), lambda i, j: (i, 0)),
                  pl.BlockSpec((D, tn), lambda i, j: (0, j)),
                  pl.BlockSpec((D, tn), lambda i, j: (0, j)),
                  pl.BlockSpec((1, tn), lambda i, j: (0, j)),
                  pl.BlockSpec((1, tn), lambda i, j: (0, j)),
                  pl.BlockSpec((tm, tn), lambda i, j: (i, j)),
                  pl.BlockSpec((tm, tn), lambda i, j: (i, ga_off + j))],
        out_specs=pl.BlockSpec((tm, tn), lambda i, j: (i, j)),
        compiler_params=_cparams(("parallel", "arbitrary")),
        name="mixer_merge",
    )(zs, attn, w_co, w_ao, b_co.reshape(1, D), b_ao.reshape(1, D), gates, gates)


def _out_proj_kernel(mg_ref, xp_ref, xs_ref, w_ref, g_ref, m2p_ref, m3p_ref, m4p_ref,
                     m2s_ref, m3s_ref, m4s_ref, x1_ref, h2_ref, *, n_prompt):
    i = pl.program_id(0)
    y = jnp.dot(mg_ref[...], w_ref[...], preferred_element_type=F32)

    def finish(x_ref, m2_ref, m3_ref, m4_ref):
        x1 = x_ref[...] + m2_ref[...] * y.reshape(x_ref.shape)
        x1_ref[...] = x1.reshape(x1_ref.shape)
        h2_ref[...] = _norm_mod(x1, g_ref[...], m3_ref[...], m4_ref[...]).reshape(h2_ref.shape)

    @pl.when(i < n_prompt)
    def _():
        finish(xp_ref, m2p_ref, m3p_ref, m4p_ref)

    @pl.when(i >= n_prompt)
    def _():
        finish(xs_ref, m2s_ref, m3s_ref, m4s_ref)


def _out_proj(merged, x_prompt, x_sample, w_out_b, g2, mp, ms):
    tl = _Tiles(256)
    xp, xs, mps, mss = tl.specs()
    row = pl.BlockSpec((tl.tm, D), lambda i: (i, 0))
    return pl.pallas_call(
        functools.partial(_out_proj_kernel, n_prompt=tl.n_prompt),
        out_shape=(jax.ShapeDtypeStruct((T, D), F32), jax.ShapeDtypeStruct((T, D), F32)),
        grid=(tl.n,),
        in_specs=[row, xp, xs, pl.BlockSpec((D, D), lambda i: (0, 0)),
                  pl.BlockSpec((1, 1, D), lambda i: (0, 0, 0)), mps, mps, mps, mss, mss, mss],
        out_specs=(row, row),
        compiler_params=_cparams(("parallel",)),
        name="out_proj_norm2",
    )(merged, x_prompt, x_sample, w_out_b, g2.reshape(1, 1, D), mp[2], mp[3], mp[4], ms[2], ms[3], ms[4])


ROUTE_TR = 512


def _router_kernel(h_ref, wt_ref, b_ref, tri_ref, idx_ref, gate_ref, rank_ref, cnt_ref, carry):
    @pl.when(pl.program_id(0) == 0)
    def _():
        carry[...] = jnp.zeros_like(carry)

    h = h_ref[...]
    w = wt_ref[...]
    h_hi = h.astype(BF16)
    h_lo = (h - h_hi.astype(F32)).astype(BF16)
    w_hi = w.astype(BF16)
    w_lo = (w - w_hi.astype(F32)).astype(BF16)
    dn = (((1,), (1,)), ((), ()))
    logits = (lax.dot_general(w_hi, h_hi, dn, preferred_element_type=F32)
              + lax.dot_general(w_hi, h_lo, dn, preferred_element_type=F32)
              + lax.dot_general(w_lo, h_hi, dn, preferred_element_type=F32)) + b_ref[...]
    eid = lax.broadcasted_iota(jnp.int32, (NE, ROUTE_TR), 0).astype(F32)
    work = logits
    vals, ids = [], []
    onehot = jnp.zeros((NE, ROUTE_TR), F32)
    for _ in range(TOPK):
        m = jnp.max(work, axis=0, keepdims=True)
        sel = jnp.min(jnp.where(work == m, eid, float(NE)), axis=0, keepdims=True)
        hit = eid == sel
        work = jnp.where(hit, -jnp.inf, work)
        onehot = jnp.where(hit, 1.0, onehot)
        vals.append(m)
        ids.append(sel)
    es = [jnp.exp(v - vals[0]) for v in vals]
    den = es[0] + es[1] + es[2] + es[3]
    before = jnp.dot(onehot.astype(BF16), tri_ref[...], preferred_element_type=F32) + carry[:, 0:1]
    for k in range(TOPK):
        idx_ref[k:k + 1, :] = ids[k].astype(jnp.int32)
        gate_ref[k:k + 1, :] = es[k] / den
        rank_ref[k:k + 1, :] = jnp.sum(jnp.where(eid == ids[k], before, 0.0), axis=0,
                                       keepdims=True).astype(jnp.int32)
    carry[...] = carry[...] + jnp.sum(onehot, axis=1, keepdims=True)
    cnt_ref[...] = carry[...]


def _router(h2, w_router, b_router):
    tri = jnp.triu(jnp.ones((ROUTE_TR, ROUTE_TR), F32), 1).astype(BF16)
    row = lambda i: (0, i)
    return pl.pallas_call(
        _router_kernel,
        out_shape=(jax.ShapeDtypeStruct((TOPK, T), jnp.int32),
                   jax.ShapeDtypeStruct((TOPK, T), F32),
                   jax.ShapeDtypeStruct((TOPK, T), jnp.int32),
                   jax.ShapeDtypeStruct((NE, 128), F32)),
        grid=(T // ROUTE_TR,),
        in_specs=[pl.BlockSpec((ROUTE_TR, D), lambda i: (i, 0)),
                  pl.BlockSpec((NE, D), lambda i: (0, 0)),
                  pl.BlockSpec((NE, 1), lambda i: (0, 0)),
                  pl.BlockSpec((ROUTE_TR, ROUTE_TR), lambda i: (0, 0))],
        out_specs=(pl.BlockSpec((TOPK, ROUTE_TR), row),
                   pl.BlockSpec((TOPK, ROUTE_TR), row),
                   pl.BlockSpec((TOPK, ROUTE_TR), row),
                   pl.BlockSpec((NE, 128), lambda i: (0, 0))),
        scratch_shapes=[pltpu.VMEM((NE, 128), F32)],
        compiler_params=_cparams(("arbitrary",)),
        name="router_top4",
    )(h2, w_router.T, b_router.reshape(NE, 1), tri)


def _dispatch_kernel(tok_ref, nt_ref, h_hbm, o_ref, buf, sem):
    i = pl.program_id(0)
    nt = nt_ref[0]

    def issue(tile, slot):
        def body(r8, c):
            for u in range(UNROLL):
                r = r8 * UNROLL + u
                pltpu.make_async_copy(h_hbm.at[pl.ds(tok_ref[tile * SUB + r], 1)],
                                      buf.at[slot, pl.ds(r, 1)], sem.at[slot]).start()
            return c
        lax.fori_loop(0, SUB // UNROLL, body, 0)

    @pl.when(i == 0)
    def _():
        issue(0, 0)

    @pl.when(i + 1 < nt)
    def _():
        issue(i + 1, (i + 1) % 2)

    @pl.when(i < nt)
    def _():
        slot = i % 2
        pltpu.make_async_copy(h_hbm.at[pl.ds(0, SUB)], buf.at[slot], sem.at[slot]).wait()
        o_ref[...] = buf[slot].astype(BF16)

    @pl.when(i >= nt)
    def _():
        o_ref[...] = jnp.zeros_like(o_ref)


def _dispatch(tok_of_slot, nt_used, h2):
    return pl.pallas_call(
        _dispatch_kernel,
        out_shape=jax.ShapeDtypeStruct((P_MAX, D), BF16),
        grid_spec=pltpu.PrefetchScalarGridSpec(
            num_scalar_prefetch=2,
            grid=(NT_MAX,),
            in_specs=[pl.BlockSpec(memory_space=pl.ANY)],
            out_specs=pl.BlockSpec((SUB, D), lambda i, tok, nt: (i, 0)),
            scratch_shapes=[pltpu.VMEM((2, SUB, D), F32), pltpu.SemaphoreType.DMA((2,))]),
        compiler_params=_cparams(("arbitrary",)),
        name="moe_dispatch",
    )(tok_of_slot, nt_used, h2)


def _moe_kernel(ge_ref, gs_ref, gn_ref, nt_ref, w1_ref, b1_ref, w2_ref, b2_ref, sel_ref, xs_hbm, y_hbm,
                x_buf, acc, w1b, w2b, h_scr, sem_in, sem_out):
    del ge_ref
    g = pl.program_id(0)
    c = pl.program_id(1)
    nsub = gn_ref[g]
    start = gs_ref[g]

    def rows_of(r):
        return pl.ds(pl.multiple_of(r * SUB, SUB), SUB)

    def x_copy(r):
        return pltpu.make_async_copy(xs_hbm.at[pl.ds((start + r) * SUB, SUB)], x_buf.at[rows_of(r)], sem_in)

    def y_copy(r):
        return pltpu.make_async_copy(acc.at[rows_of(r)], y_hbm.at[pl.ds((start + r) * SUB, SUB)], sem_out)

    def for_subtiles(fn, n):
        def body(r, carry):
            fn(r)
            return carry
        lax.fori_loop(0, n, body, 0)

    def up_proj(r):
        return jnp.dot(x_buf[rows_of(r), :], w1b[...], preferred_element_type=F32) + b1_ref[0]

    def down_proj(r, h):
        hn = jnp.concatenate(
            [pltpu.roll(h[:, k * 128:(k + 1) * 128], 127, 1) for k in range(2 * TF // 128)], axis=1)
        glu = jnp.minimum(h, LIMIT)
        lin = jnp.clip(hn, -LIMIT, LIMIT)
        act = (glu * _sigmoid(ALPHA * glu) * (lin + 1.0)).astype(BF16)
        act = jnp.dot(act, sel_ref[...], preferred_element_type=F32).astype(BF16)
        rows = rows_of(r)
        acc[rows, :] = acc[rows, :] + jnp.dot(act, w2b[...], preferred_element_type=F32)

    @pl.when(nsub > 0)
    def _():
        @pl.when(c == 0)
        def _():
            for_subtiles(lambda r: x_copy(r).start(), nsub)

            def init(r):
                acc[rows_of(r), :] = jnp.broadcast_to(b2_ref[0], (SUB, D))
            for_subtiles(init, nsub)
            for_subtiles(lambda r: x_copy(r).wait(), nsub)

        w1b[...] = w1_ref[0].astype(BF16)
        w2b[...] = w2_ref[0].astype(BF16)

        h_scr[0] = up_proj(0)

        def step(r):
            h = h_scr[r % 2]
            h_scr[(r + 1) % 2] = up_proj(r + 1)
            down_proj(r, h)
        for_subtiles(step, nsub - 1)
        down_proj(nsub - 1, h_scr[(nsub - 1) % 2])

        @pl.when(c == NC - 1)
        def _():
            for_subtiles(lambda r: y_copy(r).start(), nsub)
            for_subtiles(lambda r: y_copy(r).wait(), nsub)

    @pl.when((g == G_MAX - 1) & (c == NC - 1))
    def _():
        acc[0:SUB, :] = jnp.zeros((SUB, D), F32)

        def fill(tile, carry):
            cp = pltpu.make_async_copy(acc.at[pl.ds(0, SUB)], y_hbm.at[pl.ds(tile * SUB, SUB)], sem_out)
            cp.start()
            cp.wait()
            return carry
        lax.fori_loop(nt_ref[0], NT_MAX, fill, 0)


def _moe(g_expert, g_start, g_nsub, nt_used, w1, b1, w2, b2, xs):
    sel = (jnp.arange(2 * TF)[:, None] == 2 * jnp.arange(TF)[None, :]).astype(BF16)

    def cidx(g, c, gn):
        return jnp.where(gn[g] > 0, c, NC - 1)

    return pl.pallas_call(
        _moe_kernel,
        out_shape=jax.ShapeDtypeStruct((P_MAX, D), F32),
        grid_spec=pltpu.PrefetchScalarGridSpec(
            num_scalar_prefetch=4,
            grid=(G_MAX, NC),
            in_specs=[pl.BlockSpec((1, D, 2 * TF), lambda g, c, ge, gs, gn, nt: (ge[g], 0, cidx(g, c, gn))),
                      pl.BlockSpec((1, 1, 2 * TF), lambda g, c, ge, gs, gn, nt: (ge[g], 0, cidx(g, c, gn))),
                      pl.BlockSpec((1, TF, D), lambda g, c, ge, gs, gn, nt: (ge[g], cidx(g, c, gn), 0)),
                      pl.BlockSpec((1, 1, D), lambda g, c, ge, gs, gn, nt: (ge[g], 0, 0)),
                      pl.BlockSpec((2 * TF, TF), lambda g, c, ge, gs, gn, nt: (0, 0)),
                      pl.BlockSpec(memory_space=pl.ANY)],
            out_specs=pl.BlockSpec(memory_space=pl.ANY),
            scratch_shapes=[pltpu.VMEM((GSUB * SUB, D), BF16),
                            pltpu.VMEM((GSUB * SUB, D), F32),
                            pltpu.VMEM((D, 2 * TF), BF16),
                            pltpu.VMEM((TF, D), BF16),
                            pltpu.VMEM((2, SUB, 2 * TF), F32),
                            pltpu.SemaphoreType.DMA,
                            pltpu.SemaphoreType.DMA]),
        compiler_params=_cparams(("arbitrary", "arbitrary")),
        name="moe_experts",
    )(g_expert, g_start, g_nsub, nt_used, w1, b1.reshape(NE, 1, 2 * DFF), w2, b2.reshape(NE, 1, D), sel, xs)


COMB_TM = 128


def _combine_kernel(slot_ref, x1_ref, gate_ref, m5_ref, g_ref, y_hbm, o_ref, buf, sem, *, tok0):
    i = pl.program_id(0)
    n = pl.num_programs(0)

    def issue(tile, slot):
        for k in range(TOPK):
            def body(r8, c, k=k):
                for u in range(UNROLL):
                    r = r8 * UNROLL + u
                    s = slot_ref[k * T + tok0 + tile * COMB_TM + r]
                    pltpu.make_async_copy(y_hbm.at[pl.ds(s, 1)], buf.at[slot, k, pl.ds(r, 1)],
                                          sem.at[slot]).start()
                return c
            lax.fori_loop(0, COMB_TM // UNROLL, body, 0)

    @pl.when(i == 0)
    def _():
        issue(0, 0)

    @pl.when(i + 1 < n)
    def _():
        issue(i + 1, (i + 1) % 2)

    slot = i % 2
    for k in range(TOPK):
        pltpu.make_async_copy(y_hbm.at[pl.ds(0, COMB_TM)], buf.at[slot, k], sem.at[slot]).wait()
    gates = gate_ref[...]
    f = None
    for k in range(TOPK):
        term = gates[:, k:k + 1] * buf[slot, k]
        f = term if f is None else f + term
    x2 = x1_ref[...].reshape(o_ref.shape) + m5_ref[...] * f.reshape(o_ref.shape)
    ms = jnp.mean(x2 * x2, axis=-1, keepdims=True)
    o_ref[...] = x2 * lax.rsqrt(ms + EPS) * g_ref[...]


def _combine(prompt, slots, x1, gates_t, m5, gf, y):
    tm = COMB_TM
    if prompt:
        nb, r, steps, rb0, shape3 = 1, tm, TP // tm, 0, (BP, SP, D)
        per_b = SP // tm
        x_index = lambda i, s: (i // per_b, i % per_b, 0)
        m_index = lambda i, s: (i // per_b, 0, 0)
    else:
        nb, r, steps, rb0, shape3 = tm // SS, SS, TS // tm, TP // tm, (BS, SS, D)
        x_index = lambda i, s: (i, 0, 0)
        m_index = x_index
    return pl.pallas_call(
        functools.partial(_combine_kernel, tok0=rb0 * tm),
        out_shape=jax.ShapeDtypeStruct(shape3, F32),
        grid_spec=pltpu.PrefetchScalarGridSpec(
            num_scalar_prefetch=1,
            grid=(steps,),
            in_specs=[pl.BlockSpec((tm, D), lambda i, s: (rb0 + i, 0)),
                      pl.BlockSpec((tm, TOPK), lambda i, s: (rb0 + i, 0)),
                      pl.BlockSpec((nb, 1, D), m_index),
                      pl.BlockSpec((1, 1, D), lambda i, s: (0, 0, 0)),
                      pl.BlockSpec(memory_space=pl.ANY)],
            out_specs=pl.BlockSpec((nb, r, D), x_index),
            scratch_shapes=[pltpu.VMEM((2, TOPK, tm, D), F32), pltpu.SemaphoreType.DMA((2,))]),
        compiler_params=_cparams(("arbitrary",)),
        name="moe_combine_norm",
    )(slots, x1, gates_t, m5, gf.reshape(1, 1, D), y)


def _routing_tables(idx_t, rank_t, counts):
    tiles = (counts + SUB - 1) // SUB
    tile_end = jnp.cumsum(tiles)
    tile_start = tile_end - tiles
    nt_used = tile_end[-1:]
    experts = jnp.arange(NE, dtype=jnp.int32)
    base = jnp.sum(jnp.where(idx_t[..., None] == experts, tile_start * SUB, 0), axis=-1)
    slot = base + rank_t
    tok = jnp.tile(jnp.arange(T, dtype=jnp.int32), TOPK)
    tok_of_slot = jnp.zeros((P_MAX,), jnp.int32).at[slot.reshape(-1)].set(tok)
    ngrp = (tiles + GSUB - 1) // GSUB
    grp_end = jnp.cumsum(ngrp)
    grp_start = grp_end - ngrp
    n_groups = grp_end[-1]
    gid = jnp.arange(G_MAX, dtype=jnp.int32)
    gclamp = jnp.minimum(gid, jnp.maximum(n_groups - 1, 0))
    g_expert = jnp.sum(gclamp[:, None] >= grp_end[None, :], axis=-1).astype(jnp.int32)
    pick = lambda v: jnp.sum(jnp.where(g_expert[:, None] == experts[None, :], v[None, :], 0), axis=-1)
    local = gclamp - pick(grp_start)
    g_start = pick(tile_start) + local * GSUB
    g_nsub = jnp.where(gid < n_groups, jnp.minimum(GSUB, pick(tiles) - local * GSUB), 0)
    return (slot.reshape(-1).astype(jnp.int32), tok_of_slot, nt_used.astype(jnp.int32),
            g_expert, g_start.astype(jnp.int32), g_nsub.astype(jnp.int32))


def kernel(x_prompt, x_sample, c_prompt, c_sample, cache_k, cache_v, state_conv, w_ada, b_ada, norm1_g,
           w_in, b_in, conv_dw_w, conv_dw_b, conv_ln_g, conv_ln_b, w_conv_out, b_conv_out, sinks,
           w_attn_out, b_attn_out, w_out, norm2_g, w_router, b_router, w_mlp1, b_mlp1, w_mlp2, b_mlp2,
           norm_f_g):
    mods = _modulations(jnp.concatenate([c_prompt, c_sample], axis=0), w_ada, b_ada)
    mp = [mods[:BP, k * D:(k + 1) * D].reshape(BP, 1, D) for k in range(6)]
    ms = [mods[BP:, k * D:(k + 1) * D].reshape(BS, 1, D) for k in range(6)]

    h1 = _norm1(x_prompt, x_sample, norm1_g, mp, ms)
    w_in_b = w_in.astype(BF16)
    b_in2 = b_in.reshape(1, IN_W)
    u_all = _in_proj(h1, w_in_b, b_in2, OFF_GLU_A, OFF_GLU_B, D, "glu", F32)
    q_all = _in_proj(h1, w_in_b, b_in2, OFF_Q, None, D, "plain", BF16, HD ** -0.5)
    kv_all = _in_proj(h1, w_in_b, b_in2, OFF_KV, None, 2 * KVW, "plain", F32)
    gates = _in_proj(h1, w_in_b, b_in2, OFF_GC, None, 2 * D, "sigmoid", F32)

    u_p = u_all[:TP].reshape(BP, SP, D)
    u_s = u_all[TP:].reshape(BS, SS, D)
    up_s = jnp.concatenate([state_conv, u_s], axis=1)
    zs = _conv_prompt(u_all, conv_dw_w, conv_dw_b, conv_ln_g, conv_ln_b)
    zs = _conv_sample(zs, up_s, conv_dw_w, conv_dw_b, conv_ln_g, conv_ln_b)
    conv_p = u_p[:, SP - (CW - 1):]
    conv_s = up_s[:, SS:]

    attn = _attn_prompt(sinks, q_all, kv_all)
    q_g = (q_all[TP:].reshape(BS, SS, NKV, GRP, HD).transpose(0, 2, 3, 1, 4)
           .reshape(BS, NKV, QG, HD))
    kv_s = kv_all[TP:].reshape(BS, SS, 2 * KVW)
    ck = cache_k.reshape(BS, WIN, KVW)
    cv = cache_v.reshape(BS, WIN, KVW)
    attn_s = _attn_sample(sinks, q_g, kv_s, ck, cv)
    attn_s = (attn_s.reshape(BS, NKV, GRP, SS, HD).transpose(0, 3, 1, 2, 4).reshape(TS, D))
    attn = lax.dynamic_update_slice(attn, attn_s, (TP, 0))
    kv_p = kv_all[:TP].reshape(BP, SP, 2 * KVW)
    k_p = kv_p[:, SP - WIN:, :KVW].reshape(BP, WIN, NKV, HD)
    v_p = kv_p[:, SP - WIN:, KVW:].reshape(BP, WIN, NKV, HD)
    k_s = jnp.concatenate([ck[:, SS:], kv_s[:, :, :KVW]], axis=1).reshape(BS, WIN, NKV, HD)
    v_s = jnp.concatenate([cv[:, SS:], kv_s[:, :, KVW:]], axis=1).reshape(BS, WIN, NKV, HD)

    merged = _merge(zs, attn, w_conv_out.astype(BF16), w_attn_out.astype(BF16), b_conv_out, b_attn_out,
                    gates)
    x1, h2 = _out_proj(merged, x_prompt, x_sample, w_out.astype(BF16), norm2_g, mp, ms)

    idx_t, gate_t, rank_t, cnt = _router(h2, w_router, b_router)
    counts = cnt[:, 0].astype(jnp.int32)
    slots, tok_of_slot, nt_used, g_expert, g_start, g_nsub = _routing_tables(idx_t, rank_t, counts)

    xs = _dispatch(tok_of_slot, nt_used, h2)
    y = _moe(g_expert, g_start, g_nsub, nt_used, w_mlp1, b_mlp1, w_mlp2, b_mlp2, xs)

    gates_tok = gate_t.T
    y_prompt = _combine(True, slots, x1, gates_tok, mp[5], norm_f_g, y)
    y_sample = _combine(False, slots, x1, gates_tok, ms[5], norm_f_g, y)
    return (y_prompt, y_sample, k_p, v_p, conv_p, k_s, v_s, conv_s)
```

```python
import functools

import jax
import jax.numpy as jnp
from jax import lax
from jax.experimental import pallas as pl
from jax.experimental.pallas import tpu as pltpu

F32 = jnp.float32
BF16 = jnp.bfloat16

D = 2048
BP, SP = 4, 2048
BS, SS = 128, 8
TP = BP * SP
TS = BS * SS
T = TP + TS
HD = 64
NH = 32
NKV = 4
GRP = NH // NKV
WIN = 128
KVW = NKV * HD
CW = 31
NE = 32
TOPK = 4
DFF = 2048
ALPHA = 1.702
LIMIT = 7.0
EPS = 1e-5
OFF_GLU_A, OFF_GLU_B, OFF_Q, OFF_KV, OFF_GC = 0, D, 2 * D, 3 * D, 3 * D + 2 * KVW
IN_W = 5 * D + 2 * KVW

VMEM_LIMIT = 56 * 1024 * 1024

SUB = 256
GSUB = 8
NT_MAX = (T * TOPK) // SUB + NE
P_MAX = NT_MAX * SUB
G_MAX = -(-((T * TOPK) // SUB) // GSUB) + NE
TF = 512
NC = DFF // TF
MXU_N = 256


def _cparams(sem):
    return pltpu.CompilerParams(dimension_semantics=sem, vmem_limit_bytes=VMEM_LIMIT)


def _sigmoid(x):
    return 1.0 / (1.0 + jnp.exp(-x))


def _norm_mod(x, g, shift, scale):
    ms = jnp.mean(x * x, axis=-1, keepdims=True)
    return (x * lax.rsqrt(ms + EPS) * g) * (1.0 + scale) + shift


class _Tiles:
    def __init__(self, tm):
        self.tm = tm
        self.per_b = SP // tm
        self.n_prompt = TP // tm
        self.n = T // tm
        self.sb = tm // SS

    def prompt_x(self, i):
        ip = jnp.minimum(i, self.n_prompt - 1)
        return (ip // self.per_b, ip % self.per_b, 0)

    def prompt_mod(self, i):
        return (jnp.minimum(i, self.n_prompt - 1) // self.per_b, 0, 0)

    def sample_x(self, i):
        return (jnp.maximum(i - self.n_prompt, 0), 0, 0)

    def specs(self, tail=()):
        wrap = lambda f: (lambda i, *_: f(i))
        xp = pl.BlockSpec((1, self.tm, D), wrap(self.prompt_x))
        xs = pl.BlockSpec((self.sb, SS, D), wrap(self.sample_x))
        mp = pl.BlockSpec((1, 1, D), wrap(self.prompt_mod))
        ms = pl.BlockSpec((self.sb, 1, D), wrap(self.sample_x))
        return xp, xs, mp, ms


def _mod_kernel(c_ref, w_ref, b_ref, o_ref):
    c = c_ref[...]
    sc = (c * _sigmoid(c)).astype(BF16)
    o_ref[...] = jnp.dot(sc, w_ref[...].astype(BF16), preferred_element_type=F32) + b_ref[...]


def _modulations(c_all, w_ada, b_ada):
    nb = c_all.shape[0]
    tn = 1024
    return pl.pallas_call(
        _mod_kernel,
        out_shape=jax.ShapeDtypeStruct((nb, 6 * D), F32),
        grid=(6 * D // tn,),
        in_specs=[pl.BlockSpec((nb, D), lambda j: (0, 0)),
                  pl.BlockSpec((D, tn), lambda j: (0, j)),
                  pl.BlockSpec((1, tn), lambda j: (0, j))],
        out_specs=pl.BlockSpec((nb, tn), lambda j: (0, j)),
        compiler_params=_cparams(("parallel",)),
        name="modulations",
    )(c_all, w_ada, b_ada.reshape(1, -1))


def _norm1_kernel(xp_ref, xs_ref, g_ref, shp_ref, scp_ref, shs_ref, scs_ref, o_ref, *, n_prompt):
    i = pl.program_id(0)

    @pl.when(i < n_prompt)
    def _():
        h = _norm_mod(xp_ref[...], g_ref[...], shp_ref[...], scp_ref[...])
        o_ref[...] = h.reshape(o_ref.shape).astype(BF16)

    @pl.when(i >= n_prompt)
    def _():
        h = _norm_mod(xs_ref[...], g_ref[...], shs_ref[...], scs_ref[...])
        o_ref[...] = h.reshape(o_ref.shape).astype(BF16)


def _norm1(x_prompt, x_sample, g, mp, ms):
    tl = _Tiles(256)
    xp, xs, mps, mss = tl.specs()
    return pl.pallas_call(
        functools.partial(_norm1_kernel, n_prompt=tl.n_prompt),
        out_shape=jax.ShapeDtypeStruct((T, D), BF16),
        grid=(tl.n,),
        in_specs=[xp, xs, pl.BlockSpec((1, 1, D), lambda i: (0, 0, 0)), mps, mps, mss, mss],
        out_specs=pl.BlockSpec((tl.tm, D), lambda i: (i, 0)),
        compiler_params=_cparams(("parallel",)),
        name="norm1_adaln",
    )(x_prompt, x_sample, g.reshape(1, 1, D), mp[0], mp[1], ms[0], ms[1])


def _in_proj_kernel(*refs, mode, out_scale):
    if mode == "glu":
        h_ref, wa_ref, wb_ref, ba_ref, bb_ref, o_ref = refs
    else:
        h_ref, wa_ref, ba_ref, o_ref = refs
    h = h_ref[...]
    a = jnp.dot(h, wa_ref[...], preferred_element_type=F32) + ba_ref[...]
    if mode == "glu":
        b = jnp.dot(h, wb_ref[...], preferred_element_type=F32) + bb_ref[...]
        a = a * _sigmoid(b)
    elif mode == "sigmoid":
        a = _sigmoid(a)
    elif out_scale != 1.0:
        a = a * out_scale
    o_ref[...] = a.astype(o_ref.dtype)


def _in_proj(h, w_b, b_in2, col_a, col_b, n_cols, mode, out_dtype, out_scale=1.0):
    tm, tn = 1024, 512
    ca, cb = col_a // tn, (col_b // tn if col_b is not None else 0)
    in_specs = [pl.BlockSpec((tm, D), lambda i, j: (i, 0)),
                pl.BlockSpec((D, tn), lambda i, j: (0, ca + j))]
    args = [h, w_b]
    if mode == "glu":
        in_specs.append(pl.BlockSpec((D, tn), lambda i, j: (0, cb + j)))
        args.append(w_b)
    in_specs.append(pl.BlockSpec((1, tn), lambda i, j: (0, ca + j)))
    args.append(b_in2)
    if mode == "glu":
        in_specs.append(pl.BlockSpec((1, tn), lambda i, j: (0, cb + j)))
        args.append(b_in2)
    return pl.pallas_call(
        functools.partial(_in_proj_kernel, mode=mode, out_scale=out_scale),
        out_shape=jax.ShapeDtypeStruct((T, n_cols), out_dtype),
        grid=(T // tm, n_cols // tn),
        in_specs=in_specs,
        out_specs=pl.BlockSpec((tm, tn), lambda i, j: (i, j)),
        compiler_params=_cparams(("parallel", "arbitrary")),
        name="in_proj_" + mode,
    )(*args)


CONV_TL = 128
HALO = 32
CONV_STEPS = TP // CONV_TL


def _ln_swish(z, g, b):
    mu = jnp.mean(z, axis=-1, keepdims=True)
    d = z - mu
    var = jnp.mean(d * d, axis=-1, keepdims=True)
    zn = d * lax.rsqrt(var + EPS) * g + b
    return zn * _sigmoid(zn)


def _conv_prompt_kernel(u_ref, halo_ref, w_ref, dwb_ref, g_ref, b_ref, o_ref, win, z_scr):
    i = pl.program_id(0)

    @pl.when(i < CONV_STEPS)
    def _():
        first = (i % (SP // CONV_TL)) == 0
        base = HALO - (CW - 1)
        for s in range(D // 128):
            cs = slice(s * 128, (s + 1) * 128)
            win[s, 0:HALO, :] = jnp.where(first, 0.0, halo_ref[:, cs])
            win[s, HALO:HALO + CONV_TL, :] = u_ref[:, cs]
        for s in range(D // 128):
            cs = slice(s * 128, (s + 1) * 128)
            acc = None
            for rho in range(8):
                offs = [base + j for j in range(CW) if (base + j) % 8 == rho]
                seg = win[s, offs[0]:offs[-1] + CONV_TL, :]
                for o in offs:
                    j = o - base
                    term = seg[o - offs[0]:o - offs[0] + CONV_TL] * w_ref[j:j + 1, cs]
                    acc = term if acc is None else acc + term
            z_scr[:, cs] = acc + dwb_ref[:, cs]
        o_ref[...] = _ln_swish(z_scr[...], g_ref[...], b_ref[...]).astype(BF16)

    @pl.when(i >= CONV_STEPS)
    def _():
        o_ref[...] = jnp.zeros_like(o_ref)


def _conv_prompt(u_all, dw_w, dw_b, ln_g, ln_b):
    hb = CONV_TL // HALO
    last = CONV_STEPS - 1
    const = lambda i: (0, 0)
    return pl.pallas_call(
        _conv_prompt_kernel,
        out_shape=jax.ShapeDtypeStruct((T, D), BF16),
        grid=(T // CONV_TL,),
        in_specs=[pl.BlockSpec((CONV_TL, D), lambda i: (jnp.minimum(i, last), 0)),
                  pl.BlockSpec((HALO, D), lambda i: (jnp.maximum(jnp.minimum(i, last) * hb - 1, 0), 0)),
                  pl.BlockSpec((CW, D), const),
                  pl.BlockSpec((1, D), const),
                  pl.BlockSpec((1, D), const),
                  pl.BlockSpec((1, D), const)],
        out_specs=pl.BlockSpec((CONV_TL, D), lambda i: (i, 0)),
        scratch_shapes=[pltpu.VMEM((D // 128, HALO + CONV_TL, 128), F32), pltpu.VMEM((CONV_TL, D), F32)],
        compiler_params=_cparams(("parallel",)),
        name="conv_prompt",
    )(u_all, u_all, dw_w, dw_b.reshape(1, D), ln_g.reshape(1, D), ln_b.reshape(1, D))


CONV_BB = 8


def _conv_sample_kernel(prev_ref, st_ref, u_ref, w_ref, dwb_ref, g_ref, b_ref, o_ref, ns_ref, win, z_scr):
    del prev_ref
    u = u_ref[...].reshape(CONV_BB, SS, D)
    win[:, 0:CW - 1, :] = st_ref[...]
    win[:, CW - 1:CW - 1 + SS, :] = u
    ns_ref[:, 0:CW - 1 - SS, :] = st_ref[:, SS:, :]
    ns_ref[:, CW - 1 - SS:, :] = u
    for s in range(D // 128):
        cs = slice(s * 128, (s + 1) * 128)
        acc = None
        for j in range(CW):
            term = win[:, j:j + SS, cs] * w_ref[j:j + 1, cs][None]
            acc = term if acc is None else acc + term
        z_scr[:, :, cs] = acc + dwb_ref[:, cs][None]
    zs = _ln_swish(z_scr[...], g_ref[...][None], b_ref[...][None])
    o_ref[...] = zs.reshape(CONV_BB * SS, D).astype(BF16)


def _conv_sample(prev, state, u_all, dw_w, dw_b, ln_g, ln_b):
    rb0 = TP // (CONV_BB * SS)
    const = lambda i: (0, 0)
    return pl.pallas_call(
        _conv_sample_kernel,
        out_shape=(jax.ShapeDtypeStruct((T, D), BF16), jax.ShapeDtypeStruct((BS, CW - 1, D), F32)),
        grid=(BS // CONV_BB,),
        in_specs=[pl.BlockSpec(memory_space=pl.ANY),
                  pl.BlockSpec((CONV_BB, CW - 1, D), lambda i: (i, 0, 0)),
                  pl.BlockSpec((CONV_BB * SS, D), lambda i: (rb0 + i, 0)),
                  pl.BlockSpec((CW, D), const),
                  pl.BlockSpec((1, D), const),
                  pl.BlockSpec((1, D), const),
                  pl.BlockSpec((1, D), const)],
        out_specs=(pl.BlockSpec((CONV_BB * SS, D), lambda i: (rb0 + i, 0)),
                   pl.BlockSpec((CONV_BB, CW - 1, D), lambda i: (i, 0, 0))),
        scratch_shapes=[pltpu.VMEM((CONV_BB, CW - 1 + SS + 2, D), F32), pltpu.VMEM((CONV_BB, SS, D), F32)],
        input_output_aliases={0: 0},
        compiler_params=_cparams(("parallel",)),
        name="conv_sample",
    )(prev, state, u_all, dw_w, dw_b.reshape(1, D), ln_g.reshape(1, D), ln_b.reshape(1, D))


ATT_STEPS = TP // WIN


def _attn_prompt_kernel(sink_ref, q_ref, kvc_ref, kvp_ref, o_ref):
    i = pl.program_id(0)

    @pl.when(i < ATT_STEPS)
    def _():
        n = i % (SP // WIN)
        row = lax.broadcasted_iota(jnp.int32, (WIN, 2 * WIN), 0)
        col = lax.broadcasted_iota(jnp.int32, (WIN, 2 * WIN), 1)
        diff = row + WIN - col
        first_key = jnp.where(n > 0, 0, WIN)
        mask = (diff >= 0) & (diff < WIN) & (col >= first_key)
        kvc = kvc_ref[...].astype(BF16)
        kvp = kvp_ref[...].astype(BF16)
        outs = []
        for kh in range(NKV):
            k2 = jnp.concatenate([kvp[:, kh * HD:(kh + 1) * HD], kvc[:, kh * HD:(kh + 1) * HD]], axis=0)
            v2 = jnp.concatenate([kvp[:, KVW + kh * HD:KVW + (kh + 1) * HD],
                                  kvc[:, KVW + kh * HD:KVW + (kh + 1) * HD]], axis=0)
            for g in range(GRP):
                h = kh * GRP + g
                sink = sink_ref[h]
                qh = q_ref[:, h * HD:(h + 1) * HD]
                s = lax.dot_general(qh, k2, (((1,), (1,)), ((), ())), preferred_element_type=F32)
                s = jnp.where(mask, s, -jnp.inf)
                m = jnp.maximum(jnp.max(s, axis=-1, keepdims=True), sink)
                p = jnp.exp(s - m)
                den = jnp.sum(p, axis=-1, keepdims=True) + jnp.exp(sink - m)
                o = jnp.dot(p.astype(BF16), v2, preferred_element_type=F32)
                outs.append(o / den)
        o_ref[...] = jnp.concatenate(outs, axis=-1).astype(BF16)

    @pl.when(i >= ATT_STEPS)
    def _():
        o_ref[...] = jnp.zeros_like(o_ref)


def _attn_prompt(sinks, q_all, kv_all):
    last = ATT_STEPS - 1
    nblk = SP // WIN

    def prev_block(i):
        ic = jnp.minimum(i, last)
        return (jnp.where(ic % nblk == 0, ic, ic - 1), 0)

    return pl.pallas_call(
        _attn_prompt_kernel,
        out_shape=jax.ShapeDtypeStruct((T, D), BF16),
        grid=(T // WIN,),
        in_specs=[pl.BlockSpec(memory_space=pltpu.SMEM),
                  pl.BlockSpec((WIN, D), lambda i: (jnp.minimum(i, last), 0)),
                  pl.BlockSpec((WIN, 2 * KVW), lambda i: (jnp.minimum(i, last), 0)),
                  pl.BlockSpec((WIN, 2 * KVW), prev_block)],
        out_specs=pl.BlockSpec((WIN, D), lambda i: (i, 0)),
        compiler_params=_cparams(("parallel",)),
        name="attn_prompt",
    )(sinks, q_all, kv_all, kv_all)


ATT_BB = 8
QG = SS * GRP


def _attn_sample_kernel(sink_ref, q_ref, kvn_ref, ck_ref, cv_ref, o_ref):
    rr = lax.broadcasted_iota(jnp.int32, (QG, WIN), 0)
    qi_c = rr % SS
    key_c = lax.broadcasted_iota(jnp.int32, (QG, WIN), 1)
    mask_c = (key_c > qi_c)[None]
    rr_n = lax.broadcasted_iota(jnp.int32, (QG, SS), 0)
    key_n = lax.broadcasted_iota(jnp.int32, (QG, SS), 1)
    mask_n = (key_n <= rr_n % SS)[None]
    grow = lax.broadcasted_iota(jnp.int32, (QG, 1), 0) // SS
    for kh in range(NKV):
        sink = jnp.zeros((QG, 1), F32)
        for g in range(GRP):
            sink = jnp.where(grow == g, sink_ref[kh * GRP + g], sink)
        sink = sink[None]
        q = q_ref[:, kh]
        kc = ck_ref[:, :, kh * HD:(kh + 1) * HD].astype(BF16)
        vc = cv_ref[:, :, kh * HD:(kh + 1) * HD].astype(BF16)
        kn = kvn_ref[:, :, kh * HD:(kh + 1) * HD].astype(BF16)
        vn = kvn_ref[:, :, KVW + kh * HD:KVW + (kh + 1) * HD].astype(BF16)
        s1 = jnp.einsum("bqd,bsd->bqs", q, kc, preferred_element_type=F32)
        s2 = jnp.einsum("bqd,bsd->bqs", q, kn, preferred_element_type=F32)
        s1 = jnp.where(mask_c, s1, -jnp.inf)
        s2 = jnp.where(mask_n, s2, -jnp.inf)
        m = jnp.maximum(jnp.maximum(jnp.max(s1, axis=-1, keepdims=True),
                                    jnp.max(s2, axis=-1, keepdims=True)), sink)
        p1 = jnp.exp(s1 - m)
        p2 = jnp.exp(s2 - m)
        den = (jnp.sum(p1, axis=-1, keepdims=True) + jnp.sum(p2, axis=-1, keepdims=True)
               + jnp.exp(sink - m))
        o = (jnp.einsum("bqs,bsd->bqd", p1.astype(BF16), vc, preferred_element_type=F32)
             + jnp.einsum("bqs,bsd->bqd", p2.astype(BF16), vn, preferred_element_type=F32))
        o_ref[:, kh] = (o / den).astype(BF16)


def _attn_sample(sinks, q_g, kv_new, ck, cv):
    return pl.pallas_call(
        _attn_sample_kernel,
        out_shape=jax.ShapeDtypeStruct((BS, NKV, QG, HD), BF16),
        grid=(BS // ATT_BB,),
        in_specs=[pl.BlockSpec(memory_space=pltpu.SMEM),
                  pl.BlockSpec((ATT_BB, NKV, QG, HD), lambda i: (i, 0, 0, 0)),
                  pl.BlockSpec((ATT_BB, SS, 2 * KVW), lambda i: (i, 0, 0)),
                  pl.BlockSpec((ATT_BB, WIN, KVW), lambda i: (i, 0, 0)),
                  pl.BlockSpec((ATT_BB, WIN, KVW), lambda i: (i, 0, 0))],
        out_specs=pl.BlockSpec((ATT_BB, NKV, QG, HD), lambda i: (i, 0, 0, 0)),
        compiler_params=_cparams(("parallel",)),
        name="attn_sample",
    )(sinks, q_g, kv_new, ck, cv)


def _merge_kernel(zs_ref, at_ref, wc_ref, wa_ref, bc_ref, ba_ref, gc_ref, ga_ref, o_ref):
    conv = jnp.dot(zs_ref[...], wc_ref[...], preferred_element_type=F32) + bc_ref[...]
    attn = jnp.dot(at_ref[...], wa_ref[...], preferred_element_type=F32) + ba_ref[...]
    o_ref[...] = (gc_ref[...] * conv + ga_ref[...] * attn).astype(BF16)


def _merge(zs, attn, w_co, w_ao, b_co, b_ao, gates):
    tm, tn = 1024, 512
    ga_off = D // tn
    return pl.pallas_call(
        _merge_kernel,
        out_shape=jax.ShapeDtypeStruct((T, D), BF16),
        grid=(T // tm, D // tn),
        in_specs=[pl.BlockSpec((tm, D), lambda i, j: (i, 0)),
                  pl.BlockSpec((tm, D), lambda i, j: (i, 0)),
                  pl.BlockSpec((D, tn), lambda i, j: (0, j)),
                  pl.BlockSpec((D, tn), lambda i, j: (0, j)),
                  pl.BlockSpec((1, tn), lambda i, j: (0, j)),
                  pl.BlockSpec((1, tn), lambda i, j: (0, j)),
                  pl.BlockSpec((tm, tn), lambda i, j: (i, j)),
                  pl.BlockSpec((tm, tn), lambda i, j: (i, ga_off + j))],
        out_specs=pl.BlockSpec((tm, tn), lambda i, j: (i, j)),
        compiler_params=_cparams(("parallel", "arbitrary")),
        name="mixer_merge",
    )(zs, attn, w_co, w_ao, b_co.reshape(1, D), b_ao.reshape(1, D), gates, gates)


def _out_proj_kernel(mg_ref, xp_ref, xs_ref, w_ref, g_ref, m2p_ref, m3p_ref, m4p_ref,
                     m2s_ref, m3s_ref, m4s_ref, x1_ref, h2_ref, *, n_prompt):
    i = pl.program_id(0)
    y = jnp.dot(mg_ref[...], w_ref[...], preferred_element_type=F32)

    def finish(x_ref, m2_ref, m3_ref, m4_ref):
        x1 = x_ref[...] + m2_ref[...] * y.reshape(x_ref.shape)
        x1_ref[...] = x1.reshape(x1_ref.shape)
        h2_ref[...] = _norm_mod(x1, g_ref[...], m3_ref[...], m4_ref[...]).reshape(h2_ref.shape)

    @pl.when(i < n_prompt)
    def _():
        finish(xp_ref, m2p_ref, m3p_ref, m4p_ref)

    @pl.when(i >= n_prompt)
    def _():
        finish(xs_ref, m2s_ref, m3s_ref, m4s_ref)


def _out_proj(merged, x_prompt, x_sample, w_out_b, g2, mp, ms):
    tl = _Tiles(256)
    xp, xs, mps, mss = tl.specs()
    row = pl.BlockSpec((tl.tm, D), lambda i: (i, 0))
    return pl.pallas_call(
        functools.partial(_out_proj_kernel, n_prompt=tl.n_prompt),
        out_shape=(jax.ShapeDtypeStruct((T, D), F32), jax.ShapeDtypeStruct((T, D), F32)),
        grid=(tl.n,),
        in_specs=[row, xp, xs, pl.BlockSpec((D, D), lambda i: (0, 0)),
                  pl.BlockSpec((1, 1, D), lambda i: (0, 0, 0)), mps, mps, mps, mss, mss, mss],
        out_specs=(row, row),
        compiler_params=_cparams(("parallel",)),
        name="out_proj_norm2",
    )(merged, x_prompt, x_sample, w_out_b, g2.reshape(1, 1, D), mp[2], mp[3], mp[4], ms[2], ms[3], ms[4])


ROUTE_TR = 512


def _router_kernel(h_ref, wt_ref, b_ref, tri_ref, idx_ref, gate_ref, rank_ref, cnt_ref, carry):
    @pl.when(pl.program_id(0) == 0)
    def _():
        carry[...] = jnp.zeros_like(carry)

    h = h_ref[...]
    w = wt_ref[...]
    h_hi = h.astype(BF16)
    h_lo = (h - h_hi.astype(F32)).astype(BF16)
    w_hi = w.astype(BF16)
    w_lo = (w - w_hi.astype(F32)).astype(BF16)
    dn = (((1,), (1,)), ((), ()))
    logits = (lax.dot_general(w_hi, h_hi, dn, preferred_element_type=F32)
              + lax.dot_general(w_hi, h_lo, dn, preferred_element_type=F32)
              + lax.dot_general(w_lo, h_hi, dn, preferred_element_type=F32)) + b_ref[...]
    eid = lax.broadcasted_iota(jnp.int32, (NE, ROUTE_TR), 0).astype(F32)
    work = logits
    vals, ids = [], []
    onehot = jnp.zeros((NE, ROUTE_TR), F32)
    for _ in range(TOPK):
        m = jnp.max(work, axis=0, keepdims=True)
        sel = jnp.min(jnp.where(work == m, eid, float(NE)), axis=0, keepdims=True)
        hit = eid == sel
        work = jnp.where(hit, -jnp.inf, work)
        onehot = jnp.where(hit, 1.0, onehot)
        vals.append(m)
        ids.append(sel)
    es = [jnp.exp(v - vals[0]) for v in vals]
    den = es[0] + es[1] + es[2] + es[3]
    before = jnp.dot(onehot.astype(BF16), tri_ref[...], preferred_element_type=F32) + carry[:, 0:1]
    for k in range(TOPK):
        idx_ref[k:k + 1, :] = ids[k].astype(jnp.int32)
        gate_ref[k:k + 1, :] = es[k] / den
        rank_ref[k:k + 1, :] = jnp.sum(jnp.where(eid == ids[k], before, 0.0), axis=0,
                                       keepdims=True).astype(jnp.int32)
    carry[...] = carry[...] + jnp.sum(onehot, axis=1, keepdims=True)
    cnt_ref[...] = carry[...]


def _router(h2, w_router, b_router):
    tri = jnp.triu(jnp.ones((ROUTE_TR, ROUTE_TR), F32), 1).astype(BF16)
    row = lambda i: (0, i)
    return pl.pallas_call(
        _router_kernel,
        out_shape=(jax.ShapeDtypeStruct((TOPK, T), jnp.int32),
                   jax.ShapeDtypeStruct((TOPK, T), F32),
                   jax.ShapeDtypeStruct((TOPK, T), jnp.int32),
                   jax.ShapeDtypeStruct((NE, 128), F32)),
        grid=(T // ROUTE_TR,),
        in_specs=[pl.BlockSpec((ROUTE_TR, D), lambda i: (i, 0)),
                  pl.BlockSpec((NE, D), lambda i: (0, 0)),
                  pl.BlockSpec((NE, 1), lambda i: (0, 0)),
                  pl.BlockSpec((ROUTE_TR, ROUTE_TR), lambda i: (0, 0))],
        out_specs=(pl.BlockSpec((TOPK, ROUTE_TR), row),
                   pl.BlockSpec((TOPK, ROUTE_TR), row),
                   pl.BlockSpec((TOPK, ROUTE_TR), row),
                   pl.BlockSpec((NE, 128), lambda i: (0, 0))),
        scratch_shapes=[pltpu.VMEM((NE, 128), F32)],
        compiler_params=_cparams(("arbitrary",)),
        name="router_top4",
    )(h2, w_router.T, b_router.reshape(NE, 1), tri)


def _dispatch_kernel(tok_ref, nt_ref, h_hbm, o_ref, buf, sem):
    i = pl.program_id(0)
    nt = nt_ref[0]

    def issue(tile, slot):
        for r in range(SUB):
            pltpu.make_async_copy(h_hbm.at[pl.ds(tok_ref[tile * SUB + r], 1)],
                                  buf.at[slot, pl.ds(r, 1)], sem.at[slot]).start()

    @pl.when(i == 0)
    def _():
        issue(0, 0)

    @pl.when(i + 1 < nt)
    def _():
        issue(i + 1, (i + 1) % 2)

    @pl.when(i < nt)
    def _():
        slot = i % 2
        pltpu.make_async_copy(h_hbm.at[pl.ds(0, SUB)], buf.at[slot], sem.at[slot]).wait()
        o_ref[...] = buf[slot].astype(BF16)

    @pl.when(i >= nt)
    def _():
        o_ref[...] = jnp.zeros_like(o_ref)


def _dispatch(tok_of_slot, nt_used, h2):
    return pl.pallas_call(
        _dispatch_kernel,
        out_shape=jax.ShapeDtypeStruct((P_MAX, D), BF16),
        grid_spec=pltpu.PrefetchScalarGridSpec(
            num_scalar_prefetch=2,
            grid=(NT_MAX,),
            in_specs=[pl.BlockSpec(memory_space=pl.ANY)],
            out_specs=pl.BlockSpec((SUB, D), lambda i, tok, nt: (i, 0)),
            scratch_shapes=[pltpu.VMEM((2, SUB, D), F32), pltpu.SemaphoreType.DMA((2,))]),
        compiler_params=_cparams(("arbitrary",)),
        name="moe_dispatch",
    )(tok_of_slot, nt_used, h2)


def _moe_kernel(ge_ref, gs_ref, gn_ref, nt_ref, w1_ref, b1_ref, w2_ref, b2_ref, sel_ref, xs_hbm, y_hbm,
                x_buf, acc, h_scr, sem_in, sem_out):
    del ge_ref
    g = pl.program_id(0)
    c = pl.program_id(1)
    nsub = gn_ref[g]
    start = gs_ref[g]

    def rows_of(r):
        return pl.ds(pl.multiple_of(r * SUB, SUB), SUB)

    def x_copy(r):
        return pltpu.make_async_copy(xs_hbm.at[pl.ds((start + r) * SUB, SUB)], x_buf.at[rows_of(r)],
                                     sem_in.at[r])

    def y_copy(r):
        return pltpu.make_async_copy(acc.at[rows_of(r)], y_hbm.at[pl.ds((start + r) * SUB, SUB)], sem_out)

    def for_subtiles(fn, n):
        def body(r, carry):
            fn(r)
            return carry
        lax.fori_loop(0, n, body, 0)

    def up_proj(r):
        return (jnp.dot(x_buf[rows_of(r), :], w1_ref[0].astype(BF16), preferred_element_type=F32)
                + b1_ref[0])

    def down_proj(r, h):
        hn = jnp.concatenate(
            [pltpu.roll(h[:, k * 128:(k + 1) * 128], 127, 1) for k in range(2 * TF // 128)], axis=1)
        glu = jnp.minimum(h, LIMIT)
        lin = jnp.clip(hn, -LIMIT, LIMIT)
        act = (glu * _sigmoid(ALPHA * glu) * (lin + 1.0)).astype(BF16)
        sel = sel_ref[...]
        act = jnp.concatenate(
            [jnp.dot(act[:, k * MXU_N:(k + 1) * MXU_N], sel, preferred_element_type=F32)
             for k in range(2 * TF // MXU_N)], axis=1).astype(BF16)
        rows = rows_of(r)
        acc[rows, :] = acc[rows, :] + jnp.dot(act, w2_ref[0].astype(BF16), preferred_element_type=F32)

    @pl.when(nsub > 0)
    def _():
        @pl.when(c == 0)
        def _():
            for_subtiles(lambda r: x_copy(r).start(), nsub)

            def init(r):
                acc[rows_of(r), :] = jnp.broadcast_to(b2_ref[0], (SUB, D))
            for_subtiles(init, nsub)
            x_copy(0).wait()

        h_scr[0] = up_proj(0)

        def step(r):
            @pl.when(c == 0)
            def _():
                x_copy(r + 1).wait()

            h = h_scr[r % 2]
            h_scr[(r + 1) % 2] = up_proj(r + 1)
            down_proj(r, h)

            @pl.when(c == NC - 1)
            def _():
                y_copy(r).start()
        for_subtiles(step, nsub - 1)
        down_proj(nsub - 1, h_scr[(nsub - 1) % 2])

        @pl.when(c == NC - 1)
        def _():
            y_copy(nsub - 1).start()
            for_subtiles(lambda r: y_copy(r).wait(), nsub)

    @pl.when((g == G_MAX - 1) & (c == NC - 1))
    def _():
        acc[0:SUB, :] = jnp.zeros((SUB, D), F32)

        def fill(tile, carry):
            cp = pltpu.make_async_copy(acc.at[pl.ds(0, SUB)], y_hbm.at[pl.ds(tile * SUB, SUB)], sem_out)
            cp.start()
            cp.wait()
            return carry
        lax.fori_loop(nt_ref[0], NT_MAX, fill, 0)


def _moe(g_expert, g_start, g_nsub, nt_used, w1, b1, w2, b2, xs):
    sel = (jnp.arange(MXU_N)[:, None] == 2 * jnp.arange(MXU_N // 2)[None, :]).astype(BF16)

    def cidx(g, c, gn):
        return jnp.where(gn[g] > 0, c, NC - 1)

    return pl.pallas_call(
        _moe_kernel,
        out_shape=jax.ShapeDtypeStruct((P_MAX, D), F32),
        grid_spec=pltpu.PrefetchScalarGridSpec(
            num_scalar_prefetch=4,
            grid=(G_MAX, NC),
            in_specs=[pl.BlockSpec((1, D, 2 * TF), lambda g, c, ge, gs, gn, nt: (ge[g], 0, cidx(g, c, gn))),
                      pl.BlockSpec((1, 1, 2 * TF), lambda g, c, ge, gs, gn, nt: (ge[g], 0, cidx(g, c, gn))),
                      pl.BlockSpec((1, TF, D), lambda g, c, ge, gs, gn, nt: (ge[g], cidx(g, c, gn), 0)),
                      pl.BlockSpec((1, 1, D), lambda g, c, ge, gs, gn, nt: (ge[g], 0, 0)),
                      pl.BlockSpec((MXU_N, MXU_N // 2), lambda g, c, ge, gs, gn, nt: (0, 0)),
                      pl.BlockSpec(memory_space=pl.ANY)],
            out_specs=pl.BlockSpec(memory_space=pl.ANY),
            scratch_shapes=[pltpu.VMEM((GSUB * SUB, D), BF16),
                            pltpu.VMEM((GSUB * SUB, D), F32),
                            pltpu.VMEM((2, SUB, 2 * TF), F32),
                            pltpu.SemaphoreType.DMA((GSUB,)),
                            pltpu.SemaphoreType.DMA]),
        compiler_params=_cparams(("arbitrary", "arbitrary")),
        name="moe_experts",
    )(g_expert, g_start, g_nsub, nt_used, w1, b1.reshape(NE, 1, 2 * DFF), w2, b2.reshape(NE, 1, D), sel, xs)


COMB_TM = 128


def _combine_kernel(slot_ref, x1_ref, gate_ref, m5_ref, g_ref, y_hbm, o_ref, buf, sem, *, tok0):
    i = pl.program_id(0)
    n = pl.num_programs(0)

    def issue(tile, slot):
        for k in range(TOPK):
            for r in range(COMB_TM):
                s = slot_ref[k * T + tok0 + tile * COMB_TM + r]
                pltpu.make_async_copy(y_hbm.at[pl.ds(s, 1)], buf.at[slot, k, pl.ds(r, 1)],
                                      sem.at[slot]).start()

    @pl.when(i == 0)
    def _():
        issue(0, 0)

    @pl.when(i + 1 < n)
    def _():
        issue(i + 1, (i + 1) % 2)

    slot = i % 2
    for k in range(TOPK):
        pltpu.make_async_copy(y_hbm.at[pl.ds(0, COMB_TM)], buf.at[slot, k], sem.at[slot]).wait()
    gates = gate_ref[...]
    f = None
    for k in range(TOPK):
        term = gates[:, k:k + 1] * buf[slot, k]
        f = term if f is None else f + term
    x2 = x1_ref[...].reshape(o_ref.shape) + m5_ref[...] * f.reshape(o_ref.shape)
    ms = jnp.mean(x2 * x2, axis=-1, keepdims=True)
    o_ref[...] = x2 * lax.rsqrt(ms + EPS) * g_ref[...]


def _combine(prompt, slots, x1, gates_t, m5, gf, y):
    tm = COMB_TM
    if prompt:
        nb, r, steps, rb0, shape3 = 1, tm, TP // tm, 0, (BP, SP, D)
        per_b = SP // tm
        x_index = lambda i, s: (i // per_b, i % per_b, 0)
        m_index = lambda i, s: (i // per_b, 0, 0)
    else:
        nb, r, steps, rb0, shape3 = tm // SS, SS, TS // tm, TP // tm, (BS, SS, D)
        x_index = lambda i, s: (i, 0, 0)
        m_index = x_index
    return pl.pallas_call(
        functools.partial(_combine_kernel, tok0=rb0 * tm),
        out_shape=jax.ShapeDtypeStruct(shape3, F32),
        grid_spec=pltpu.PrefetchScalarGridSpec(
            num_scalar_prefetch=1,
            grid=(steps,),
            in_specs=[pl.BlockSpec((tm, D), lambda i, s: (rb0 + i, 0)),
                      pl.BlockSpec((tm, TOPK), lambda i, s: (rb0 + i, 0)),
                      pl.BlockSpec((nb, 1, D), m_index),
                      pl.BlockSpec((1, 1, D), lambda i, s: (0, 0, 0)),
                      pl.BlockSpec(memory_space=pl.ANY)],
            out_specs=pl.BlockSpec((nb, r, D), x_index),
            scratch_shapes=[pltpu.VMEM((2, TOPK, tm, D), F32), pltpu.SemaphoreType.DMA((2,))]),
        compiler_params=_cparams(("arbitrary",)),
        name="moe_combine_norm",
    )(slots, x1, gates_t, m5, gf.reshape(1, 1, D), y)


def _routing_tables(idx_t, rank_t, counts):
    tiles = (counts + SUB - 1) // SUB
    tile_end = jnp.cumsum(tiles)
    tile_start = tile_end - tiles
    nt_used = tile_end[-1:]
    experts = jnp.arange(NE, dtype=jnp.int32)
    base = jnp.sum(jnp.where(idx_t[..., None] == experts, tile_start * SUB, 0), axis=-1)
    slot = base + rank_t
    tok = jnp.tile(jnp.arange(T, dtype=jnp.int32), TOPK)
    tok_of_slot = jnp.zeros((P_MAX,), jnp.int32).at[slot.reshape(-1)].set(tok)
    ngrp = (tiles + GSUB - 1) // GSUB
    grp_end = jnp.cumsum(ngrp)
    grp_start = grp_end - ngrp
    n_groups = grp_end[-1]
    gid = jnp.arange(G_MAX, dtype=jnp.int32)
    gclamp = jnp.minimum(gid, jnp.maximum(n_groups - 1, 0))
    g_expert = jnp.sum(gclamp[:, None] >= grp_end[None, :], axis=-1).astype(jnp.int32)
    pick = lambda v: jnp.sum(jnp.where(g_expert[:, None] == experts[None, :], v[None, :], 0), axis=-1)
    local = gclamp - pick(grp_start)
    g_start = pick(tile_start) + local * GSUB
    g_nsub = jnp.where(gid < n_groups, jnp.minimum(GSUB, pick(tiles) - local * GSUB), 0)
    return (slot.reshape(-1).astype(jnp.int32), tok_of_slot, nt_used.astype(jnp.int32),
            g_expert, g_start.astype(jnp.int32), g_nsub.astype(jnp.int32))


def kernel(x_prompt, x_sample, c_prompt, c_sample, cache_k, cache_v, state_conv, w_ada, b_ada, norm1_g,
           w_in, b_in, conv_dw_w, conv_dw_b, conv_ln_g, conv_ln_b, w_conv_out, b_conv_out, sinks,
           w_attn_out, b_attn_out, w_out, norm2_g, w_router, b_router, w_mlp1, b_mlp1, w_mlp2, b_mlp2,
           norm_f_g):
    mods = _modulations(jnp.concatenate([c_prompt, c_sample], axis=0), w_ada, b_ada)
    mp = [mods[:BP, k * D:(k + 1) * D].reshape(BP, 1, D) for k in range(6)]
    ms = [mods[BP:, k * D:(k + 1) * D].reshape(BS, 1, D) for k in range(6)]

    h1 = _norm1(x_prompt, x_sample, norm1_g, mp, ms)
    w_in_b = w_in.astype(BF16)
    b_in2 = b_in.reshape(1, IN_W)
    u_all = _in_proj(h1, w_in_b, b_in2, OFF_GLU_A, OFF_GLU_B, D, "glu", F32)
    q_all = _in_proj(h1, w_in_b, b_in2, OFF_Q, None, D, "plain", BF16, HD ** -0.5)
    kv_all = _in_proj(h1, w_in_b, b_in2, OFF_KV, None, 2 * KVW, "plain", F32)
    gates = _in_proj(h1, w_in_b, b_in2, OFF_GC, None, 2 * D, "sigmoid", F32)

    zs = _conv_prompt(u_all, conv_dw_w, conv_dw_b, conv_ln_g, conv_ln_b)
    zs, conv_s = _conv_sample(zs, state_conv, u_all, conv_dw_w, conv_dw_b, conv_ln_g, conv_ln_b)
    conv_p = jnp.stack([u_all[(b + 1) * SP - (CW - 1):(b + 1) * SP] for b in range(BP)])

    attn = _attn_prompt(sinks, q_all, kv_all)
    q_g = (q_all[TP:].reshape(BS, SS, NKV, GRP, HD).transpose(0, 2, 3, 1, 4)
           .reshape(BS, NKV, QG, HD))
    kv_s = kv_all[TP:].reshape(BS, SS, 2 * KVW)
    ck = cache_k.reshape(BS, WIN, KVW)
    cv = cache_v.reshape(BS, WIN, KVW)
    attn_s = _attn_sample(sinks, q_g, kv_s, ck, cv)
    attn_s = (attn_s.reshape(BS, NKV, GRP, SS, HD).transpose(0, 3, 1, 2, 4).reshape(TS, D))
    attn = lax.dynamic_update_slice(attn, attn_s, (TP, 0))
    kv_tail = jnp.stack([kv_all[(b + 1) * SP - WIN:(b + 1) * SP] for b in range(BP)])
    k_p = kv_tail[:, :, :KVW].reshape(BP, WIN, NKV, HD)
    v_p = kv_tail[:, :, KVW:].reshape(BP, WIN, NKV, HD)
    k_s = jnp.concatenate([ck[:, SS:], kv_s[:, :, :KVW]], axis=1).reshape(BS, WIN, NKV, HD)
    v_s = jnp.concatenate([cv[:, SS:], kv_s[:, :, KVW:]], axis=1).reshape(BS, WIN, NKV, HD)

    merged = _merge(zs, attn, w_conv_out.astype(BF16), w_attn_out.astype(BF16), b_conv_out, b_attn_out,
                    gates)
    x1, h2 = _out_proj(merged, x_prompt, x_sample, w_out.astype(BF16), norm2_g, mp, ms)

    idx_t, gate_t, rank_t, cnt = _router(h2, w_router, b_router)
    counts = cnt[:, 0].astype(jnp.int32)
    slots, tok_of_slot, nt_used, g_expert, g_start, g_nsub = _routing_tables(idx_t, rank_t, counts)

    xs = _dispatch(tok_of_slot, nt_used, h2)
    y = _moe(g_expert, g_start, g_nsub, nt_used, w_mlp1, b_mlp1, w_mlp2, b_mlp2, xs)

    gates_tok = gate_t.T
    y_prompt = _combine(True, slots, x1, gates_tok, mp[5], norm_f_g, y)
    y_sample = _combine(False, slots, x1, gates_tok, ms[5], norm_f_g, y)
    return (y_prompt, y_sample, k_p, v_p, conv_p, k_s, v_s, conv_s)
```

```python
import functools

import jax
import jax.numpy as jnp
from jax import lax
from jax.experimental import pallas as pl
from jax.experimental.pallas import tpu as pltpu

F32 = jnp.float32
BF16 = jnp.bfloat16

D = 2048
BP, SP = 4, 2048
BS, SS = 128, 8
TP = BP * SP
TS = BS * SS
T = TP + TS
HD = 64
NH = 32
NKV = 4
GRP = NH // NKV
WIN = 128
KVW = NKV * HD
CW = 31
NE = 32
TOPK = 4
DFF = 2048
ALPHA = 1.702
LIMIT = 7.0
EPS = 1e-5
OFF_GLU_A, OFF_GLU_B, OFF_Q, OFF_KV, OFF_GC = 0, D, 2 * D, 3 * D, 3 * D + 2 * KVW
IN_W = 5 * D + 2 * KVW

VMEM_LIMIT = 56 * 1024 * 1024

SUB = 256
GSUB = 8
NT_MAX = (T * TOPK) // SUB + NE
P_MAX = NT_MAX * SUB
G_MAX = -(-((T * TOPK) // SUB) // GSUB) + NE
TF = 512
NC = DFF // TF
MXU_N = 256


def _cparams(sem):
    return pltpu.CompilerParams(dimension_semantics=sem, vmem_limit_bytes=VMEM_LIMIT)


def _sigmoid(x):
    return 1.0 / (1.0 + jnp.exp(-x))


def _norm_mod(x, g, shift, scale):
    ms = jnp.mean(x * x, axis=-1, keepdims=True)
    return (x * lax.rsqrt(ms + EPS) * g) * (1.0 + scale) + shift


class _Tiles:
    def __init__(self, tm):
        self.tm = tm
        self.per_b = SP // tm
        self.n_prompt = TP // tm
        self.n = T // tm
        self.sb = tm // SS

    def prompt_x(self, i):
        ip = jnp.minimum(i, self.n_prompt - 1)
        return (ip // self.per_b, ip % self.per_b, 0)

    def prompt_mod(self, i):
        return (jnp.minimum(i, self.n_prompt - 1) // self.per_b, 0, 0)

    def sample_x(self, i):
        return (jnp.maximum(i - self.n_prompt, 0), 0, 0)

    def specs(self, tail=()):
        wrap = lambda f: (lambda i, *_: f(i))
        xp = pl.BlockSpec((1, self.tm, D), wrap(self.prompt_x))
        xs = pl.BlockSpec((self.sb, SS, D), wrap(self.sample_x))
        mp = pl.BlockSpec((1, 1, D), wrap(self.prompt_mod))
        ms = pl.BlockSpec((self.sb, 1, D), wrap(self.sample_x))
        return xp, xs, mp, ms


def _mod_kernel(c_ref, w_ref, b_ref, o_ref):
    c = c_ref[...]
    sc = (c * _sigmoid(c)).astype(BF16)
    o_ref[...] = jnp.dot(sc, w_ref[...].astype(BF16), preferred_element_type=F32) + b_ref[...]


def _modulations(c_all, w_ada, b_ada):
    nb = c_all.shape[0]
    tn = 1024
    return pl.pallas_call(
        _mod_kernel,
        out_shape=jax.ShapeDtypeStruct((nb, 6 * D), F32),
        grid=(6 * D // tn,),
        in_specs=[pl.BlockSpec((nb, D), lambda j: (0, 0)),
                  pl.BlockSpec((D, tn), lambda j: (0, j)),
                  pl.BlockSpec((1, tn), lambda j: (0, j))],
        out_specs=pl.BlockSpec((nb, tn), lambda j: (0, j)),
        compiler_params=_cparams(("parallel",)),
        name="modulations",
    )(c_all, w_ada, b_ada.reshape(1, -1))


def _norm1_kernel(xp_ref, xs_ref, g_ref, shp_ref, scp_ref, shs_ref, scs_ref, o_ref, *, n_prompt):
    i = pl.program_id(0)

    @pl.when(i < n_prompt)
    def _():
        h = _norm_mod(xp_ref[...], g_ref[...], shp_ref[...], scp_ref[...])
        o_ref[...] = h.reshape(o_ref.shape).astype(BF16)

    @pl.when(i >= n_prompt)
    def _():
        h = _norm_mod(xs_ref[...], g_ref[...], shs_ref[...], scs_ref[...])
        o_ref[...] = h.reshape(o_ref.shape).astype(BF16)


def _norm1(x_prompt, x_sample, g, mp, ms):
    tl = _Tiles(256)
    xp, xs, mps, mss = tl.specs()
    return pl.pallas_call(
        functools.partial(_norm1_kernel, n_prompt=tl.n_prompt),
        out_shape=jax.ShapeDtypeStruct((T, D), BF16),
        grid=(tl.n,),
        in_specs=[xp, xs, pl.BlockSpec((1, 1, D), lambda i: (0, 0, 0)), mps, mps, mss, mss],
        out_specs=pl.BlockSpec((tl.tm, D), lambda i: (i, 0)),
        compiler_params=_cparams(("parallel",)),
        name="norm1_adaln",
    )(x_prompt, x_sample, g.reshape(1, 1, D), mp[0], mp[1], ms[0], ms[1])


def _in_proj_kernel(*refs, mode, out_scale):
    if mode == "glu":
        h_ref, wa_ref, wb_ref, ba_ref, bb_ref, o_ref = refs
    else:
        h_ref, wa_ref, ba_ref, o_ref = refs
    h = h_ref[...]
    a = jnp.dot(h, wa_ref[...], preferred_element_type=F32) + ba_ref[...]
    if mode == "glu":
        b = jnp.dot(h, wb_ref[...], preferred_element_type=F32) + bb_ref[...]
        a = a * _sigmoid(b)
    elif mode == "sigmoid":
        a = _sigmoid(a)
    elif out_scale != 1.0:
        a = a * out_scale
    o_ref[...] = a.astype(o_ref.dtype)


def _in_proj(h, w_b, b_in2, col_a, col_b, n_cols, mode, out_dtype, out_scale=1.0):
    tm, tn = 1024, 512
    ca, cb = col_a // tn, (col_b // tn if col_b is not None else 0)
    in_specs = [pl.BlockSpec((tm, D), lambda i, j: (i, 0)),
                pl.BlockSpec((D, tn), lambda i, j: (0, ca + j))]
    args = [h, w_b]
    if mode == "glu":
        in_specs.append(pl.BlockSpec((D, tn), lambda i, j: (0, cb + j)))
        args.append(w_b)
    in_specs.append(pl.BlockSpec((1, tn), lambda i, j: (0, ca + j)))
    args.append(b_in2)
    if mode == "glu":
        in_specs.append(pl.BlockSpec((1, tn), lambda i, j: (0, cb + j)))
        args.append(b_in2)
    return pl.pallas_call(
        functools.partial(_in_proj_kernel, mode=mode, out_scale=out_scale),
        out_shape=jax.ShapeDtypeStruct((T, n_cols), out_dtype),
        grid=(T // tm, n_cols // tn),
        in_specs=in_specs,
        out_specs=pl.BlockSpec((tm, tn), lambda i, j: (i, j)),
        compiler_params=_cparams(("parallel", "arbitrary")),
        name="in_proj_" + mode,
    )(*args)


CONV_TL = 128
HALO = 32
CONV_STEPS = TP // CONV_TL


def _ln_swish(z, g, b):
    mu = jnp.mean(z, axis=-1, keepdims=True)
    d = z - mu
    var = jnp.mean(d * d, axis=-1, keepdims=True)
    zn = d * lax.rsqrt(var + EPS) * g + b
    return zn * _sigmoid(zn)


def _conv_prompt_kernel(u_ref, halo_ref, w_ref, dwb_ref, g_ref, b_ref, o_ref, win, z_scr):
    i = pl.program_id(0)

    @pl.when(i < CONV_STEPS)
    def _():
        first = (i % (SP // CONV_TL)) == 0
        base = HALO - (CW - 1)
        for s in range(D // 128):
            cs = slice(s * 128, (s + 1) * 128)
            win[s, 0:HALO, :] = jnp.where(first, 0.0, halo_ref[:, cs])
            win[s, HALO:HALO + CONV_TL, :] = u_ref[:, cs]
        for s in range(D // 128):
            cs = slice(s * 128, (s + 1) * 128)
            acc = None
            for rho in range(8):
                offs = [base + j for j in range(CW) if (base + j) % 8 == rho]
                seg = win[s, offs[0]:offs[-1] + CONV_TL, :]
                for o in offs:
                    j = o - base
                    term = seg[o - offs[0]:o - offs[0] + CONV_TL] * w_ref[j:j + 1, cs]
                    acc = term if acc is None else acc + term
            z_scr[:, cs] = acc + dwb_ref[:, cs]
        o_ref[...] = _ln_swish(z_scr[...], g_ref[...], b_ref[...]).astype(BF16)

    @pl.when(i >= CONV_STEPS)
    def _():
        o_ref[...] = jnp.zeros_like(o_ref)


def _conv_prompt(u_all, dw_w, dw_b, ln_g, ln_b):
    hb = CONV_TL // HALO
    last = CONV_STEPS - 1
    const = lambda i: (0, 0)
    return pl.pallas_call(
        _conv_prompt_kernel,
        out_shape=jax.ShapeDtypeStruct((T, D), BF16),
        grid=(T // CONV_TL,),
        in_specs=[pl.BlockSpec((CONV_TL, D), lambda i: (jnp.minimum(i, last), 0)),
                  pl.BlockSpec((HALO, D), lambda i: (jnp.maximum(jnp.minimum(i, last) * hb - 1, 0), 0)),
                  pl.BlockSpec((CW, D), const),
                  pl.BlockSpec((1, D), const),
                  pl.BlockSpec((1, D), const),
                  pl.BlockSpec((1, D), const)],
        out_specs=pl.BlockSpec((CONV_TL, D), lambda i: (i, 0)),
        scratch_shapes=[pltpu.VMEM((D // 128, HALO + CONV_TL, 128), F32), pltpu.VMEM((CONV_TL, D), F32)],
        compiler_params=_cparams(("parallel",)),
        name="conv_prompt",
    )(u_all, u_all, dw_w, dw_b.reshape(1, D), ln_g.reshape(1, D), ln_b.reshape(1, D))


CONV_BB = 16


def _conv_sample_kernel(st_ref, u_ref, w_ref, dwb_ref, g_ref, b_ref, o_ref, ns_ref, win, z_scr):
    win[0:CW - 1] = st_ref[...]
    win[CW - 1:CW - 1 + SS] = u_ref[...]
    ns_ref[0:CW - 1 - SS] = st_ref[SS:CW - 1]
    ns_ref[CW - 1 - SS:CW - 1] = u_ref[...]
    for s in range(D // 128):
        cs = slice(s * 128, (s + 1) * 128)
        acc = None
        for j in range(CW):
            term = win[j:j + SS, :, cs] * w_ref[j:j + 1, cs][None]
            acc = term if acc is None else acc + term
        z_scr[:, :, cs] = acc + dwb_ref[:, cs][None]
    o_ref[...] = _ln_swish(z_scr[...], g_ref[...][None], b_ref[...][None]).astype(BF16)


def _conv_sample(state_t, u_t, dw_w, dw_b, ln_g, ln_b):
    const = lambda i: (0, 0)
    batch = lambda i: (0, i, 0)
    return pl.pallas_call(
        _conv_sample_kernel,
        out_shape=(jax.ShapeDtypeStruct((SS, BS, D), BF16), jax.ShapeDtypeStruct((CW - 1, BS, D), F32)),
        grid=(BS // CONV_BB,),
        in_specs=[pl.BlockSpec((CW - 1, CONV_BB, D), batch),
                  pl.BlockSpec((SS, CONV_BB, D), batch),
                  pl.BlockSpec((CW, D), const),
                  pl.BlockSpec((1, D), const),
                  pl.BlockSpec((1, D), const),
                  pl.BlockSpec((1, D), const)],
        out_specs=(pl.BlockSpec((SS, CONV_BB, D), batch),
                   pl.BlockSpec((CW - 1, CONV_BB, D), batch)),
        scratch_shapes=[pltpu.VMEM((CW - 1 + SS, CONV_BB, D), F32), pltpu.VMEM((SS, CONV_BB, D), F32)],
        compiler_params=_cparams(("parallel",)),
        name="conv_sample",
    )(state_t, u_t, dw_w, dw_b.reshape(1, D), ln_g.reshape(1, D), ln_b.reshape(1, D))


ATT_STEPS = TP // WIN


def _attn_prompt_kernel(sink_ref, q_ref, kvc_ref, kvp_ref, o_ref):
    i = pl.program_id(0)

    @pl.when(i < ATT_STEPS)
    def _():
        n = i % (SP // WIN)
        row = lax.broadcasted_iota(jnp.int32, (WIN, 2 * WIN), 0)
        col = lax.broadcasted_iota(jnp.int32, (WIN, 2 * WIN), 1)
        diff = row + WIN - col
        first_key = jnp.where(n > 0, 0, WIN)
        mask = (diff >= 0) & (diff < WIN) & (col >= first_key)
        kvc = kvc_ref[...].astype(BF16)
        kvp = kvp_ref[...].astype(BF16)
        outs = []
        for kh in range(NKV):
            k2 = jnp.concatenate([kvp[:, kh * HD:(kh + 1) * HD], kvc[:, kh * HD:(kh + 1) * HD]], axis=0)
            v2 = jnp.concatenate([kvp[:, KVW + kh * HD:KVW + (kh + 1) * HD],
                                  kvc[:, KVW + kh * HD:KVW + (kh + 1) * HD]], axis=0)
            for g in range(GRP):
                h = kh * GRP + g
                sink = sink_ref[h]
                qh = q_ref[:, h * HD:(h + 1) * HD]
                s = lax.dot_general(qh, k2, (((1,), (1,)), ((), ())), preferred_element_type=F32)
                s = jnp.where(mask, s, -jnp.inf)
                m = jnp.maximum(jnp.max(s, axis=-1, keepdims=True), sink)
                p = jnp.exp(s - m)
                den = jnp.sum(p, axis=-1, keepdims=True) + jnp.exp(sink - m)
                o = jnp.dot(p.astype(BF16), v2, preferred_element_type=F32)
                outs.append(o / den)
        o_ref[...] = jnp.concatenate(outs, axis=-1).astype(BF16)

    @pl.when(i >= ATT_STEPS)
    def _():
        o_ref[...] = jnp.zeros_like(o_ref)


def _attn_prompt(sinks, q_all, kv_all):
    last = ATT_STEPS - 1
    nblk = SP // WIN

    def prev_block(i):
        ic = jnp.minimum(i, last)
        return (jnp.where(ic % nblk == 0, ic, ic - 1), 0)

    return pl.pallas_call(
        _attn_prompt_kernel,
        out_shape=jax.ShapeDtypeStruct((T, D), BF16),
        grid=(T // WIN,),
        in_specs=[pl.BlockSpec(memory_space=pltpu.SMEM),
                  pl.BlockSpec((WIN, D), lambda i: (jnp.minimum(i, last), 0)),
                  pl.BlockSpec((WIN, 2 * KVW), lambda i: (jnp.minimum(i, last), 0)),
                  pl.BlockSpec((WIN, 2 * KVW), prev_block)],
        out_specs=pl.BlockSpec((WIN, D), lambda i: (i, 0)),
        compiler_params=_cparams(("parallel",)),
        name="attn_prompt",
    )(sinks, q_all, kv_all, kv_all)


ATT_BB = 8
QG = SS * GRP


def _attn_sample_kernel(sink_ref, q_ref, kvn_ref, ck_ref, cv_ref, o_ref):
    rr = lax.broadcasted_iota(jnp.int32, (QG, WIN), 0)
    qi_c = rr % SS
    key_c = lax.broadcasted_iota(jnp.int32, (QG, WIN), 1)
    mask_c = (key_c > qi_c)[None]
    rr_n = lax.broadcasted_iota(jnp.int32, (QG, SS), 0)
    key_n = lax.broadcasted_iota(jnp.int32, (QG, SS), 1)
    mask_n = (key_n <= rr_n % SS)[None]
    grow = lax.broadcasted_iota(jnp.int32, (QG, 1), 0) // SS
    for kh in range(NKV):
        sink = jnp.zeros((QG, 1), F32)
        for g in range(GRP):
            sink = jnp.where(grow == g, sink_ref[kh * GRP + g], sink)
        sink = sink[None]
        q = q_ref[:, kh]
        kc = ck_ref[:, kh].astype(BF16)
        vc = cv_ref[:, kh].astype(BF16)
        kn = kvn_ref[:, :, kh * HD:(kh + 1) * HD].astype(BF16)
        vn = kvn_ref[:, :, KVW + kh * HD:KVW + (kh + 1) * HD].astype(BF16)
        s1 = jnp.einsum("bqd,bds->bqs", q, kc, preferred_element_type=F32)
        s2 = jnp.einsum("bqd,bsd->bqs", q, kn, preferred_element_type=F32)
        s1 = jnp.where(mask_c, s1, -jnp.inf)
        s2 = jnp.where(mask_n, s2, -jnp.inf)
        m = jnp.maximum(jnp.maximum(jnp.max(s1, axis=-1, keepdims=True),
                                    jnp.max(s2, axis=-1, keepdims=True)), sink)
        p1 = jnp.exp(s1 - m)
        p2 = jnp.exp(s2 - m)
        den = (jnp.sum(p1, axis=-1, keepdims=True) + jnp.sum(p2, axis=-1, keepdims=True)
               + jnp.exp(sink - m))
        o = (jnp.einsum("bqs,bds->bqd", p1.astype(BF16), vc, preferred_element_type=F32)
             + jnp.einsum("bqs,bsd->bqd", p2.astype(BF16), vn, preferred_element_type=F32))
        o_ref[:, kh] = (o / den).astype(BF16)


def _attn_sample(sinks, q_g, kv_new, ck, cv):
    return pl.pallas_call(
        _attn_sample_kernel,
        out_shape=jax.ShapeDtypeStruct((BS, NKV, QG, HD), BF16),
        grid=(BS // ATT_BB,),
        in_specs=[pl.BlockSpec(memory_space=pltpu.SMEM),
                  pl.BlockSpec((ATT_BB, NKV, QG, HD), lambda i: (i, 0, 0, 0)),
                  pl.BlockSpec((ATT_BB, SS, 2 * KVW), lambda i: (i, 0, 0)),
                  pl.BlockSpec((ATT_BB, NKV, HD, WIN), lambda i: (i, 0, 0, 0)),
                  pl.BlockSpec((ATT_BB, NKV, HD, WIN), lambda i: (i, 0, 0, 0))],
        out_specs=pl.BlockSpec((ATT_BB, NKV, QG, HD), lambda i: (i, 0, 0, 0)),
        compiler_params=_cparams(("parallel",)),
        name="attn_sample",
    )(sinks, q_g, kv_new, ck, cv)


def _merge_kernel(zs_ref, at_ref, wc_ref, wa_ref, bc_ref, ba_ref, gc_ref, ga_ref, o_ref):
    conv = jnp.dot(zs_ref[...], wc_ref[...], preferred_element_type=F32) + bc_ref[...]
    attn = jnp.dot(at_ref[...], wa_ref[...], preferred_element_type=F32) + ba_ref[...]
    o_ref[...] = (gc_ref[...] * conv + ga_ref[...] * attn).astype(BF16)


def _merge(zs, attn, w_co, w_ao, b_co, b_ao, gates):
    tm, tn = 1024, 512
    ga_off = D // tn
    return pl.pallas_call(
        _merge_kernel,
        out_shape=jax.ShapeDtypeStruct((T, D), BF16),
        grid=(T // tm, D // tn),
        in_specs=[pl.BlockSpec((tm, D), lambda i, j: (i, 0)),
                  pl.BlockSpec((tm, D), lambda i, j: (i, 0)),
                  pl.BlockSpec((D, tn), lambda i, j: (0, j)),
                  pl.BlockSpec((D, tn), lambda i, j: (0, j)),
                  pl.BlockSpec((1, tn), lambda i, j: (0, j)),
                  pl.BlockSpec((1, tn), lambda i, j: (0, j)),
                  pl.BlockSpec((tm, tn), lambda i, j: (i, j)),
                  pl.BlockSpec((tm, tn), lambda i, j: (i, ga_off + j))],
        out_specs=pl.BlockSpec((tm, tn), lambda i, j: (i, j)),
        compiler_params=_cparams(("parallel", "arbitrary")),
        name="mixer_merge",
    )(zs, attn, w_co, w_ao, b_co.reshape(1, D), b_ao.reshape(1, D), gates, gates)


def _out_proj_kernel(mg_ref, xp_ref, xs_ref, w_ref, g_ref, m2p_ref, m3p_ref, m4p_ref,
                     m2s_ref, m3s_ref, m4s_ref, x1_ref, h2_ref, *, n_prompt):
    i = pl.program_id(0)
    y = jnp.dot(mg_ref[...], w_ref[...], preferred_element_type=F32)

    def finish(x_ref, m2_ref, m3_ref, m4_ref):
        x1 = x_ref[...] + m2_ref[...] * y.reshape(x_ref.shape)
        x1_ref[...] = x1.reshape(x1_ref.shape)
        h2_ref[...] = _norm_mod(x1, g_ref[...], m3_ref[...], m4_ref[...]).reshape(h2_ref.shape)

    @pl.when(i < n_prompt)
    def _():
        finish(xp_ref, m2p_ref, m3p_ref, m4p_ref)

    @pl.when(i >= n_prompt)
    def _():
        finish(xs_ref, m2s_ref, m3s_ref, m4s_ref)


def _out_proj(merged, x_prompt, x_sample, w_out_b, g2, mp, ms):
    tl = _Tiles(256)
    xp, xs, mps, mss = tl.specs()
    row = pl.BlockSpec((tl.tm, D), lambda i: (i, 0))
    return pl.pallas_call(
        functools.partial(_out_proj_kernel, n_prompt=tl.n_prompt),
        out_shape=(jax.ShapeDtypeStruct((T, D), F32), jax.ShapeDtypeStruct((T, D), F32)),
        grid=(tl.n,),
        in_specs=[row, xp, xs, pl.BlockSpec((D, D), lambda i: (0, 0)),
                  pl.BlockSpec((1, 1, D), lambda i: (0, 0, 0)), mps, mps, mps, mss, mss, mss],
        out_specs=(row, row),
        compiler_params=_cparams(("parallel",)),
        name="out_proj_norm2",
    )(merged, x_prompt, x_sample, w_out_b, g2.reshape(1, 1, D), mp[2], mp[3], mp[4], ms[2], ms[3], ms[4])


ROUTE_TR = 512


def _router_kernel(h_ref, wt_ref, b_ref, tri_ref, idx_ref, gate_ref, rank_ref, cnt_ref, carry):
    @pl.when(pl.program_id(0) == 0)
    def _():
        carry[...] = jnp.zeros_like(carry)

    h = h_ref[...]
    w = wt_ref[...]
    h_hi = h.astype(BF16)
    h_lo = (h - h_hi.astype(F32)).astype(BF16)
    w_hi = w.astype(BF16)
    w_lo = (w - w_hi.astype(F32)).astype(BF16)
    dn = (((1,), (1,)), ((), ()))
    logits = (lax.dot_general(w_hi, h_hi, dn, preferred_element_type=F32)
              + lax.dot_general(w_hi, h_lo, dn, preferred_element_type=F32)
              + lax.dot_general(w_lo, h_hi, dn, preferred_element_type=F32)) + b_ref[...]
    eid = lax.broadcasted_iota(jnp.int32, (NE, ROUTE_TR), 0).astype(F32)
    work = logits
    vals, ids = [], []
    onehot = jnp.zeros((NE, ROUTE_TR), F32)
    for _ in range(TOPK):
        m = jnp.max(work, axis=0, keepdims=True)
        sel = jnp.min(jnp.where(work == m, eid, float(NE)), axis=0, keepdims=True)
        hit = eid == sel
        work = jnp.where(hit, -jnp.inf, work)
        onehot = jnp.where(hit, 1.0, onehot)
        vals.append(m)
        ids.append(sel)
    es = [jnp.exp(v - vals[0]) for v in vals]
    den = es[0] + es[1] + es[2] + es[3]
    before = jnp.dot(onehot.astype(BF16), tri_ref[...], preferred_element_type=F32) + carry[:, 0:1]
    for k in range(TOPK):
        idx_ref[k:k + 1, :] = ids[k].astype(jnp.int32)
        gate_ref[k:k + 1, :] = es[k] / den
        rank_ref[k:k + 1, :] = jnp.sum(jnp.where(eid == ids[k], before, 0.0), axis=0,
                                       keepdims=True).astype(jnp.int32)
    carry[...] = carry[...] + jnp.sum(onehot, axis=1, keepdims=True)
    cnt_ref[...] = carry[...]


def _router(h2, w_router, b_router):
    tri = jnp.triu(jnp.ones((ROUTE_TR, ROUTE_TR), F32), 1).astype(BF16)
    row = lambda i: (0, i)
    return pl.pallas_call(
        _router_kernel,
        out_shape=(jax.ShapeDtypeStruct((TOPK, T), jnp.int32),
                   jax.ShapeDtypeStruct((TOPK, T), F32),
                   jax.ShapeDtypeStruct((TOPK, T), jnp.int32),
                   jax.ShapeDtypeStruct((NE, 128), F32)),
        grid=(T // ROUTE_TR,),
        in_specs=[pl.BlockSpec((ROUTE_TR, D), lambda i: (i, 0)),
                  pl.BlockSpec((NE, D), lambda i: (0, 0)),
                  pl.BlockSpec((NE, 1), lambda i: (0, 0)),
                  pl.BlockSpec((ROUTE_TR, ROUTE_TR), lambda i: (0, 0))],
        out_specs=(pl.BlockSpec((TOPK, ROUTE_TR), row),
                   pl.BlockSpec((TOPK, ROUTE_TR), row),
                   pl.BlockSpec((TOPK, ROUTE_TR), row),
                   pl.BlockSpec((NE, 128), lambda i: (0, 0))),
        scratch_shapes=[pltpu.VMEM((NE, 128), F32)],
        compiler_params=_cparams(("arbitrary",)),
        name="router_top4",
    )(h2, w_router.T, b_router.reshape(NE, 1), tri)


DISPATCH_AHEAD = 2
DISPATCH_SLOTS = DISPATCH_AHEAD + 1


def _dispatch_kernel(tok_ref, nt_ref, h_hbm, o_ref, buf, sem):
    i = pl.program_id(0)
    nt = nt_ref[0]

    def issue(tile, slot):
        for r in range(SUB):
            pltpu.make_async_copy(h_hbm.at[pl.ds(tok_ref[tile * SUB + r], 1)],
                                  buf.at[slot, pl.ds(r, 1)], sem.at[slot]).start()

    @pl.when(i == 0)
    def _():
        for a in range(DISPATCH_AHEAD):
            @pl.when(a < nt)
            def _(a=a):
                issue(a, a)

    @pl.when(i + DISPATCH_AHEAD < nt)
    def _():
        issue(i + DISPATCH_AHEAD, (i + DISPATCH_AHEAD) % DISPATCH_SLOTS)

    @pl.when(i < nt)
    def _():
        slot = i % DISPATCH_SLOTS
        pltpu.make_async_copy(h_hbm.at[pl.ds(0, SUB)], buf.at[slot], sem.at[slot]).wait()
        o_ref[...] = buf[slot].astype(BF16)

    @pl.when(i >= nt)
    def _():
        o_ref[...] = jnp.zeros_like(o_ref)


def _dispatch(tok_of_slot, nt_used, h2):
    return pl.pallas_call(
        _dispatch_kernel,
        out_shape=jax.ShapeDtypeStruct((P_MAX, D), BF16),
        grid_spec=pltpu.PrefetchScalarGridSpec(
            num_scalar_prefetch=2,
            grid=(NT_MAX,),
            in_specs=[pl.BlockSpec(memory_space=pl.ANY)],
            out_specs=pl.BlockSpec((SUB, D), lambda i, tok, nt: (i, 0)),
            scratch_shapes=[pltpu.VMEM((DISPATCH_SLOTS, SUB, D), F32),
                            pltpu.SemaphoreType.DMA((DISPATCH_SLOTS,))]),
        compiler_params=_cparams(("arbitrary",)),
        name="moe_dispatch",
    )(tok_of_slot, nt_used, h2)


def _moe_kernel(ge_ref, gs_ref, gn_ref, nt_ref, w1_ref, b1_ref, w2_ref, b2_ref, sel_ref, xs_hbm, y_hbm,
                x_buf, acc, h_scr, sem_in, sem_out):
    del ge_ref
    g = pl.program_id(0)
    c = pl.program_id(1)
    nsub = gn_ref[g]
    start = gs_ref[g]

    def rows_of(r):
        return pl.ds(pl.multiple_of(r * SUB, SUB), SUB)

    def x_copy(r):
        return pltpu.make_async_copy(xs_hbm.at[pl.ds((start + r) * SUB, SUB)], x_buf.at[rows_of(r)],
                                     sem_in.at[r])

    def y_copy(r):
        return pltpu.make_async_copy(acc.at[rows_of(r)], y_hbm.at[pl.ds((start + r) * SUB, SUB)], sem_out)

    def for_subtiles(fn, n):
        def body(r, carry):
            fn(r)
            return carry
        lax.fori_loop(0, n, body, 0)

    def up_proj(r):
        return (jnp.dot(x_buf[rows_of(r), :], w1_ref[0].astype(BF16), preferred_element_type=F32)
                + b1_ref[0])

    def down_proj(r, h):
        hn = jnp.concatenate(
            [pltpu.roll(h[:, k * 128:(k + 1) * 128], 127, 1) for k in range(2 * TF // 128)], axis=1)
        glu = jnp.minimum(h, LIMIT)
        lin = jnp.clip(hn, -LIMIT, LIMIT)
        act = (glu * _sigmoid(ALPHA * glu) * (lin + 1.0)).astype(BF16)
        sel = sel_ref[...]
        act = jnp.concatenate(
            [jnp.dot(act[:, k * MXU_N:(k + 1) * MXU_N], sel, preferred_element_type=F32)
             for k in range(2 * TF // MXU_N)], axis=1).astype(BF16)
        rows = rows_of(r)
        acc[rows, :] = acc[rows, :] + jnp.dot(act, w2_ref[0].astype(BF16), preferred_element_type=F32)

    @pl.when(nsub > 0)
    def _():
        @pl.when(c == 0)
        def _():
            for_subtiles(lambda r: x_copy(r).start(), nsub)

            def init(r):
                acc[rows_of(r), :] = jnp.broadcast_to(b2_ref[0], (SUB, D))
            for_subtiles(init, nsub)
            x_copy(0).wait()

        h_scr[0] = up_proj(0)

        def step(r):
            @pl.when(c == 0)
            def _():
                x_copy(r + 1).wait()

            h = h_scr[r % 2]
            h_scr[(r + 1) % 2] = up_proj(r + 1)
            down_proj(r, h)

            @pl.when(c == NC - 1)
            def _():
                y_copy(r).start()
        for_subtiles(step, nsub - 1)
        down_proj(nsub - 1, h_scr[(nsub - 1) % 2])

        @pl.when(c == NC - 1)
        def _():
            y_copy(nsub - 1).start()
            for_subtiles(lambda r: y_copy(r).wait(), nsub)

    @pl.when((g == G_MAX - 1) & (c == NC - 1))
    def _():
        acc[0:SUB, :] = jnp.zeros((SUB, D), F32)

        def fill(tile, carry):
            cp = pltpu.make_async_copy(acc.at[pl.ds(0, SUB)], y_hbm.at[pl.ds(tile * SUB, SUB)], sem_out)
            cp.start()
            cp.wait()
            return carry
        lax.fori_loop(nt_ref[0], NT_MAX, fill, 0)


def _moe(g_expert, g_start, g_nsub, nt_used, w1, b1, w2, b2, xs):
    sel = (jnp.arange(MXU_N)[:, None] == 2 * jnp.arange(MXU_N // 2)[None, :]).astype(BF16)

    def cidx(g, c, gn):
        return jnp.where(gn[g] > 0, c, NC - 1)

    return pl.pallas_call(
        _moe_kernel,
        out_shape=jax.ShapeDtypeStruct((P_MAX, D), F32),
        grid_spec=pltpu.PrefetchScalarGridSpec(
            num_scalar_prefetch=4,
            grid=(G_MAX, NC),
            in_specs=[pl.BlockSpec((1, D, 2 * TF), lambda g, c, ge, gs, gn, nt: (ge[g], 0, cidx(g, c, gn))),
                      pl.BlockSpec((1, 1, 2 * TF), lambda g, c, ge, gs, gn, nt: (ge[g], 0, cidx(g, c, gn))),
                      pl.BlockSpec((1, TF, D), lambda g, c, ge, gs, gn, nt: (ge[g], cidx(g, c, gn), 0)),
                      pl.BlockSpec((1, 1, D), lambda g, c, ge, gs, gn, nt: (ge[g], 0, 0)),
                      pl.BlockSpec((MXU_N, MXU_N // 2), lambda g, c, ge, gs, gn, nt: (0, 0)),
                      pl.BlockSpec(memory_space=pl.ANY)],
            out_specs=pl.BlockSpec(memory_space=pl.ANY),
            scratch_shapes=[pltpu.VMEM((GSUB * SUB, D), BF16),
                            pltpu.VMEM((GSUB * SUB, D), F32),
                            pltpu.VMEM((2, SUB, 2 * TF), F32),
                            pltpu.SemaphoreType.DMA((GSUB,)),
                            pltpu.SemaphoreType.DMA]),
        compiler_params=_cparams(("arbitrary", "arbitrary")),
        name="moe_experts",
    )(g_expert, g_start, g_nsub, nt_used, w1, b1.reshape(NE, 1, 2 * DFF), w2, b2.reshape(NE, 1, D), sel, xs)


COMB_TM = 128


def _combine_kernel(slot_ref, x1_ref, gate_ref, m5_ref, g_ref, y_hbm, o_ref, buf, sem, *, tok0):
    i = pl.program_id(0)
    n = pl.num_programs(0)

    def issue(tile, slot):
        for k in range(TOPK):
            for r in range(COMB_TM):
                s = slot_ref[k * T + tok0 + tile * COMB_TM + r]
                pltpu.make_async_copy(y_hbm.at[pl.ds(s, 1)], buf.at[slot, k, pl.ds(r, 1)],
                                      sem.at[slot]).start()

    @pl.when(i == 0)
    def _():
        issue(0, 0)

    @pl.when(i + 1 < n)
    def _():
        issue(i + 1, (i + 1) % 2)

    slot = i % 2
    for k in range(TOPK):
        pltpu.make_async_copy(y_hbm.at[pl.ds(0, COMB_TM)], buf.at[slot, k], sem.at[slot]).wait()
    gates = gate_ref[...]
    f = None
    for k in range(TOPK):
        term = gates[:, k:k + 1] * buf[slot, k]
        f = term if f is None else f + term
    x2 = x1_ref[...].reshape(o_ref.shape) + m5_ref[...] * f.reshape(o_ref.shape)
    ms = jnp.mean(x2 * x2, axis=-1, keepdims=True)
    o_ref[...] = x2 * lax.rsqrt(ms + EPS) * g_ref[...]


def _combine(prompt, slots, x1, gates_t, m5, gf, y):
    tm = COMB_TM
    if prompt:
        nb, r, steps, rb0, shape3 = 1, tm, TP // tm, 0, (BP, SP, D)
        per_b = SP // tm
        x_index = lambda i, s: (i // per_b, i % per_b, 0)
        m_index = lambda i, s: (i // per_b, 0, 0)
    else:
        nb, r, steps, rb0, shape3 = tm // SS, SS, TS // tm, TP // tm, (BS, SS, D)
        x_index = lambda i, s: (i, 0, 0)
        m_index = x_index
    return pl.pallas_call(
        functools.partial(_combine_kernel, tok0=rb0 * tm),
        out_shape=jax.ShapeDtypeStruct(shape3, F32),
        grid_spec=pltpu.PrefetchScalarGridSpec(
            num_scalar_prefetch=1,
            grid=(steps,),
            in_specs=[pl.BlockSpec((tm, D), lambda i, s: (rb0 + i, 0)),
                      pl.BlockSpec((tm, TOPK), lambda i, s: (rb0 + i, 0)),
                      pl.BlockSpec((nb, 1, D), m_index),
                      pl.BlockSpec((1, 1, D), lambda i, s: (0, 0, 0)),
                      pl.BlockSpec(memory_space=pl.ANY)],
            out_specs=pl.BlockSpec((nb, r, D), x_index),
            scratch_shapes=[pltpu.VMEM((2, TOPK, tm, D), F32), pltpu.SemaphoreType.DMA((2,))]),
        compiler_params=_cparams(("arbitrary",)),
        name="moe_combine_norm",
    )(slots, x1, gates_t, m5, gf.reshape(1, 1, D), y)


def _routing_tables(idx_t, rank_t, counts):
    tiles = (counts + SUB - 1) // SUB
    tile_end = jnp.cumsum(tiles)
    tile_start = tile_end - tiles
    nt_used = tile_end[-1:]
    experts = jnp.arange(NE, dtype=jnp.int32)
    base = jnp.sum(jnp.where(idx_t[..., None] == experts, tile_start * SUB, 0), axis=-1)
    slot = base + rank_t
    tok = jnp.tile(jnp.arange(T, dtype=jnp.int32), TOPK)
    tok_of_slot = jnp.zeros((P_MAX,), jnp.int32).at[slot.reshape(-1)].set(tok)
    ngrp = (tiles + GSUB - 1) // GSUB
    grp_end = jnp.cumsum(ngrp)
    grp_start = grp_end - ngrp
    n_groups = grp_end[-1]
    gid = jnp.arange(G_MAX, dtype=jnp.int32)
    gclamp = jnp.minimum(gid, jnp.maximum(n_groups - 1, 0))
    g_expert = jnp.sum(gclamp[:, None] >= grp_end[None, :], axis=-1).astype(jnp.int32)
    pick = lambda v: jnp.sum(jnp.where(g_expert[:, None] == experts[None, :], v[None, :], 0), axis=-1)
    local = gclamp - pick(grp_start)
    g_start = pick(tile_start) + local * GSUB
    g_nsub = jnp.where(gid < n_groups, jnp.minimum(GSUB, pick(tiles) - local * GSUB), 0)
    return (slot.reshape(-1).astype(jnp.int32), tok_of_slot, nt_used.astype(jnp.int32),
            g_expert, g_start.astype(jnp.int32), g_nsub.astype(jnp.int32))


def kernel(x_prompt, x_sample, c_prompt, c_sample, cache_k, cache_v, state_conv, w_ada, b_ada, norm1_g,
           w_in, b_in, conv_dw_w, conv_dw_b, conv_ln_g, conv_ln_b, w_conv_out, b_conv_out, sinks,
           w_attn_out, b_attn_out, w_out, norm2_g, w_router, b_router, w_mlp1, b_mlp1, w_mlp2, b_mlp2,
           norm_f_g):
    mods = _modulations(jnp.concatenate([c_prompt, c_sample], axis=0), w_ada, b_ada)
    mp = [mods[:BP, k * D:(k + 1) * D].reshape(BP, 1, D) for k in range(6)]
    ms = [mods[BP:, k * D:(k + 1) * D].reshape(BS, 1, D) for k in range(6)]

    h1 = _norm1(x_prompt, x_sample, norm1_g, mp, ms)
    w_in_b = w_in.astype(BF16)
    b_in2 = b_in.reshape(1, IN_W)
    u_all = _in_proj(h1, w_in_b, b_in2, OFF_GLU_A, OFF_GLU_B, D, "glu", F32)
    q_all = _in_proj(h1, w_in_b, b_in2, OFF_Q, None, D, "plain", BF16, HD ** -0.5)
    kv_all = _in_proj(h1, w_in_b, b_in2, OFF_KV, None, 2 * KVW, "plain", F32)
    gates = _in_proj(h1, w_in_b, b_in2, OFF_GC, None, 2 * D, "sigmoid", F32)

    zs = _conv_prompt(u_all, conv_dw_w, conv_dw_b, conv_ln_g, conv_ln_b)
    u_s_t = u_all[TP:].reshape(BS, SS, D).transpose(1, 0, 2)
    zs_s_t, conv_s_t = _conv_sample(state_conv.transpose(1, 0, 2), u_s_t, conv_dw_w, conv_dw_b,
                                    conv_ln_g, conv_ln_b)
    zs = lax.dynamic_update_slice(zs, zs_s_t.transpose(1, 0, 2).reshape(TS, D), (TP, 0))
    conv_s = conv_s_t.transpose(1, 0, 2)
    conv_p = jnp.stack([u_all[(b + 1) * SP - (CW - 1):(b + 1) * SP] for b in range(BP)])

    attn = _attn_prompt(sinks, q_all, kv_all)
    q_g = (q_all[TP:].reshape(BS, SS, NKV, GRP, HD).transpose(0, 2, 3, 1, 4)
           .reshape(BS, NKV, QG, HD))
    kv_s = kv_all[TP:].reshape(BS, SS, 2 * KVW)
    attn_s = _attn_sample(sinks, q_g, kv_s, cache_k.transpose(0, 2, 3, 1), cache_v.transpose(0, 2, 3, 1))
    attn_s = (attn_s.reshape(BS, NKV, GRP, SS, HD).transpose(0, 3, 1, 2, 4).reshape(TS, D))
    attn = lax.dynamic_update_slice(attn, attn_s, (TP, 0))
    kv_tail = jnp.stack([kv_all[(b + 1) * SP - WIN:(b + 1) * SP] for b in range(BP)])
    k_p = kv_tail[:, :, :KVW].reshape(BP, WIN, NKV, HD)
    v_p = kv_tail[:, :, KVW:].reshape(BP, WIN, NKV, HD)
    k_s = jnp.concatenate([cache_k[:, SS:], kv_s[:, :, :KVW].reshape(BS, SS, NKV, HD)], axis=1)
    v_s = jnp.concatenate([cache_v[:, SS:], kv_s[:, :, KVW:].reshape(BS, SS, NKV, HD)], axis=1)

    merged = _merge(zs, attn, w_conv_out.astype(BF16), w_attn_out.astype(BF16), b_conv_out, b_attn_out,
                    gates)
    x1, h2 = _out_proj(merged, x_prompt, x_sample, w_out.astype(BF16), norm2_g, mp, ms)

    idx_t, gate_t, rank_t, cnt = _router(h2, w_router, b_router)
    counts = cnt[:, 0].astype(jnp.int32)
    slots, tok_of_slot, nt_used, g_expert, g_start, g_nsub = _routing_tables(idx_t, rank_t, counts)

    xs = _dispatch(tok_of_slot, nt_used, h2)
    y = _moe(g_expert, g_start, g_nsub, nt_used, w_mlp1, b_mlp1, w_mlp2, b_mlp2, xs)

    gates_tok = gate_t.T
    y_prompt = _combine(True, slots, x1, gates_tok, mp[5], norm_f_g, y)
    y_sample = _combine(False, slots, x1, gates_tok, ms[5], norm_f_g, y)
    return (y_prompt, y_sample, k_p, v_p, conv_p, k_s, v_s, conv_s)
```

```python
import functools

import jax
import jax.numpy as jnp
from jax import lax
from jax.experimental import pallas as pl
from jax.experimental.pallas import tpu as pltpu

F32 = jnp.float32
BF16 = jnp.bfloat16

D = 2048
BP, SP = 4, 2048
BS, SS = 128, 8
TP = BP * SP
TS = BS * SS
T = TP + TS
HD = 64
NH = 32
NKV = 4
GRP = NH // NKV
WIN = 128
KVW = NKV * HD
CW = 31
NE = 32
TOPK = 4
DFF = 2048
ALPHA = 1.702
LIMIT = 7.0
EPS = 1e-5
OFF_GLU_A, OFF_GLU_B, OFF_Q, OFF_KV, OFF_GC = 0, D, 2 * D, 3 * D, 3 * D + 2 * KVW
IN_W = 5 * D + 2 * KVW

VMEM_LIMIT = 56 * 1024 * 1024

SUB = 256
GSUB = 8
NT_MAX = (T * TOPK) // SUB + NE
P_MAX = NT_MAX * SUB
G_MAX = -(-((T * TOPK) // SUB) // GSUB) + NE
TF = 512
NC = DFF // TF
MXU_N = 256


def _cparams(sem):
    return pltpu.CompilerParams(dimension_semantics=sem, vmem_limit_bytes=VMEM_LIMIT)


def _sigmoid(x):
    return 1.0 / (1.0 + jnp.exp(-x))


def _norm_mod(x, g, shift, scale):
    ms = jnp.mean(x * x, axis=-1, keepdims=True)
    return (x * lax.rsqrt(ms + EPS) * g) * (1.0 + scale) + shift


class _Tiles:
    def __init__(self, tm):
        self.tm = tm
        self.per_b = SP // tm
        self.n_prompt = TP // tm
        self.n = T // tm
        self.sb = tm // SS

    def prompt_x(self, i):
        ip = jnp.minimum(i, self.n_prompt - 1)
        return (ip // self.per_b, ip % self.per_b, 0)

    def prompt_mod(self, i):
        return (jnp.minimum(i, self.n_prompt - 1) // self.per_b, 0, 0)

    def sample_x(self, i):
        return (jnp.maximum(i - self.n_prompt, 0), 0, 0)

    def specs(self, tail=()):
        wrap = lambda f: (lambda i, *_: f(i))
        xp = pl.BlockSpec((1, self.tm, D), wrap(self.prompt_x))
        xs = pl.BlockSpec((self.sb, SS, D), wrap(self.sample_x))
        mp = pl.BlockSpec((1, 1, D), wrap(self.prompt_mod))
        ms = pl.BlockSpec((self.sb, 1, D), wrap(self.sample_x))
        return xp, xs, mp, ms


def _mod_kernel(c_ref, w_ref, b_ref, o_ref):
    c = c_ref[...]
    sc = (c * _sigmoid(c)).astype(BF16)
    o_ref[...] = jnp.dot(sc, w_ref[...].astype(BF16), preferred_element_type=F32) + b_ref[...]


def _modulations(c_all, w_ada, b_ada):
    nb = c_all.shape[0]
    tn = 1024
    return pl.pallas_call(
        _mod_kernel,
        out_shape=jax.ShapeDtypeStruct((nb, 6 * D), F32),
        grid=(6 * D // tn,),
        in_specs=[pl.BlockSpec((nb, D), lambda j: (0, 0)),
                  pl.BlockSpec((D, tn), lambda j: (0, j)),
                  pl.BlockSpec((1, tn), lambda j: (0, j))],
        out_specs=pl.BlockSpec((nb, tn), lambda j: (0, j)),
        compiler_params=_cparams(("parallel",)),
        name="modulations",
    )(c_all, w_ada, b_ada.reshape(1, -1))


def _norm1_kernel(xp_ref, xs_ref, g_ref, shp_ref, scp_ref, shs_ref, scs_ref, o_ref, *, n_prompt):
    i = pl.program_id(0)

    @pl.when(i < n_prompt)
    def _():
        h = _norm_mod(xp_ref[...], g_ref[...], shp_ref[...], scp_ref[...])
        o_ref[...] = h.reshape(o_ref.shape).astype(BF16)

    @pl.when(i >= n_prompt)
    def _():
        h = _norm_mod(xs_ref[...], g_ref[...], shs_ref[...], scs_ref[...])
        o_ref[...] = h.reshape(o_ref.shape).astype(BF16)


def _norm1(x_prompt, x_sample, g, mp, ms):
    tl = _Tiles(256)
    xp, xs, mps, mss = tl.specs()
    return pl.pallas_call(
        functools.partial(_norm1_kernel, n_prompt=tl.n_prompt),
        out_shape=jax.ShapeDtypeStruct((T, D), BF16),
        grid=(tl.n,),
        in_specs=[xp, xs, pl.BlockSpec((1, 1, D), lambda i: (0, 0, 0)), mps, mps, mss, mss],
        out_specs=pl.BlockSpec((tl.tm, D), lambda i: (i, 0)),
        compiler_params=_cparams(("parallel",)),
        name="norm1_adaln",
    )(x_prompt, x_sample, g.reshape(1, 1, D), mp[0], mp[1], ms[0], ms[1])


def _in_proj_kernel(*refs, mode, out_scale):
    if mode == "glu":
        h_ref, wa_ref, wb_ref, ba_ref, bb_ref, o_ref = refs
    else:
        h_ref, wa_ref, ba_ref, o_ref = refs
    h = h_ref[...]
    a = jnp.dot(h, wa_ref[...], preferred_element_type=F32) + ba_ref[...]
    if mode == "glu":
        b = jnp.dot(h, wb_ref[...], preferred_element_type=F32) + bb_ref[...]
        a = a * _sigmoid(b)
    elif mode == "sigmoid":
        a = _sigmoid(a)
    elif out_scale != 1.0:
        a = a * out_scale
    o_ref[...] = a.astype(o_ref.dtype)


def _in_proj(h, w_b, b_in2, col_a, col_b, n_cols, mode, out_dtype, out_scale=1.0):
    tm, tn = 1024, 512
    ca, cb = col_a // tn, (col_b // tn if col_b is not None else 0)
    in_specs = [pl.BlockSpec((tm, D), lambda i, j: (i, 0)),
                pl.BlockSpec((D, tn), lambda i, j: (0, ca + j))]
    args = [h, w_b]
    if mode == "glu":
        in_specs.append(pl.BlockSpec((D, tn), lambda i, j: (0, cb + j)))
        args.append(w_b)
    in_specs.append(pl.BlockSpec((1, tn), lambda i, j: (0, ca + j)))
    args.append(b_in2)
    if mode == "glu":
        in_specs.append(pl.BlockSpec((1, tn), lambda i, j: (0, cb + j)))
        args.append(b_in2)
    return pl.pallas_call(
        functools.partial(_in_proj_kernel, mode=mode, out_scale=out_scale),
        out_shape=jax.ShapeDtypeStruct((T, n_cols), out_dtype),
        grid=(T // tm, n_cols // tn),
        in_specs=in_specs,
        out_specs=pl.BlockSpec((tm, tn), lambda i, j: (i, j)),
        compiler_params=_cparams(("parallel", "arbitrary")),
        name="in_proj_" + mode,
    )(*args)


CONV_TL = 128
HALO = 32
CONV_STEPS = TP // CONV_TL


def _ln_swish(z, g, b):
    mu = jnp.mean(z, axis=-1, keepdims=True)
    d = z - mu
    var = jnp.mean(d * d, axis=-1, keepdims=True)
    zn = d * lax.rsqrt(var + EPS) * g + b
    return zn * _sigmoid(zn)


def _conv_prompt_kernel(u_ref, halo_ref, w_ref, dwb_ref, g_ref, b_ref, o_ref, win, z_scr):
    i = pl.program_id(0)

    @pl.when(i < CONV_STEPS)
    def _():
        first = (i % (SP // CONV_TL)) == 0
        base = HALO - (CW - 1)
        for s in range(D // 128):
            cs = slice(s * 128, (s + 1) * 128)
            win[s, 0:HALO, :] = jnp.where(first, 0.0, halo_ref[:, cs])
            win[s, HALO:HALO + CONV_TL, :] = u_ref[:, cs]
        for s in range(D // 128):
            cs = slice(s * 128, (s + 1) * 128)
            acc = None
            for rho in range(8):
                offs = [base + j for j in range(CW) if (base + j) % 8 == rho]
                seg = win[s, offs[0]:offs[-1] + CONV_TL, :]
                for o in offs:
                    j = o - base
                    term = seg[o - offs[0]:o - offs[0] + CONV_TL] * w_ref[j:j + 1, cs]
                    acc = term if acc is None else acc + term
            z_scr[:, cs] = acc + dwb_ref[:, cs]
        o_ref[...] = _ln_swish(z_scr[...], g_ref[...], b_ref[...]).astype(BF16)

    @pl.when(i >= CONV_STEPS)
    def _():
        o_ref[...] = jnp.zeros_like(o_ref)


def _conv_prompt(u_all, dw_w, dw_b, ln_g, ln_b):
    hb = CONV_TL // HALO
    last = CONV_STEPS - 1
    const = lambda i: (0, 0)
    return pl.pallas_call(
        _conv_prompt_kernel,
        out_shape=jax.ShapeDtypeStruct((T, D), BF16),
        grid=(T // CONV_TL,),
        in_specs=[pl.BlockSpec((CONV_TL, D), lambda i: (jnp.minimum(i, last), 0)),
                  pl.BlockSpec((HALO, D), lambda i: (jnp.maximum(jnp.minimum(i, last) * hb - 1, 0), 0)),
                  pl.BlockSpec((CW, D), const),
                  pl.BlockSpec((1, D), const),
                  pl.BlockSpec((1, D), const),
                  pl.BlockSpec((1, D), const)],
        out_specs=pl.BlockSpec((CONV_TL, D), lambda i: (i, 0)),
        scratch_shapes=[pltpu.VMEM((D // 128, HALO + CONV_TL, 128), F32), pltpu.VMEM((CONV_TL, D), F32)],
        compiler_params=_cparams(("parallel",)),
        name="conv_prompt",
    )(u_all, u_all, dw_w, dw_b.reshape(1, D), ln_g.reshape(1, D), ln_b.reshape(1, D))


CONV_BB = 16


def _conv_sample_kernel(st_ref, u_ref, w_ref, dwb_ref, g_ref, b_ref, o_ref, ns_ref, win, z_scr):
    win[0:CW - 1] = st_ref[...]
    win[CW - 1:CW - 1 + SS] = u_ref[...]
    ns_ref[0:CW - 1 - SS] = st_ref[SS:CW - 1]
    ns_ref[CW - 1 - SS:CW - 1] = u_ref[...]
    for s in range(D // 128):
        cs = slice(s * 128, (s + 1) * 128)
        acc = None
        for j in range(CW):
            term = win[j:j + SS, :, cs] * w_ref[j:j + 1, cs][None]
            acc = term if acc is None else acc + term
        z_scr[:, :, cs] = acc + dwb_ref[:, cs][None]
    o_ref[...] = _ln_swish(z_scr[...], g_ref[...][None], b_ref[...][None]).astype(BF16)


def _conv_sample(state_t, u_t, dw_w, dw_b, ln_g, ln_b):
    const = lambda i: (0, 0)
    batch = lambda i: (0, i, 0)
    return pl.pallas_call(
        _conv_sample_kernel,
        out_shape=(jax.ShapeDtypeStruct((SS, BS, D), BF16), jax.ShapeDtypeStruct((CW - 1, BS, D), F32)),
        grid=(BS // CONV_BB,),
        in_specs=[pl.BlockSpec((CW - 1, CONV_BB, D), batch),
                  pl.BlockSpec((SS, CONV_BB, D), batch),
                  pl.BlockSpec((CW, D), const),
                  pl.BlockSpec((1, D), const),
                  pl.BlockSpec((1, D), const),
                  pl.BlockSpec((1, D), const)],
        out_specs=(pl.BlockSpec((SS, CONV_BB, D), batch),
                   pl.BlockSpec((CW - 1, CONV_BB, D), batch)),
        scratch_shapes=[pltpu.VMEM((CW - 1 + SS, CONV_BB, D), F32), pltpu.VMEM((SS, CONV_BB, D), F32)],
        compiler_params=_cparams(("parallel",)),
        name="conv_sample",
    )(state_t, u_t, dw_w, dw_b.reshape(1, D), ln_g.reshape(1, D), ln_b.reshape(1, D))


ATT_STEPS = TP // WIN


def _attn_prompt_kernel(sink_ref, q_ref, kvc_ref, kvp_ref, o_ref):
    i = pl.program_id(0)

    @pl.when(i < ATT_STEPS)
    def _():
        n = i % (SP // WIN)
        row = lax.broadcasted_iota(jnp.int32, (WIN, 2 * WIN), 0)
        col = lax.broadcasted_iota(jnp.int32, (WIN, 2 * WIN), 1)
        diff = row + WIN - col
        first_key = jnp.where(n > 0, 0, WIN)
        mask = (diff >= 0) & (diff < WIN) & (col >= first_key)
        kvc = kvc_ref[...].astype(BF16)
        kvp = kvp_ref[...].astype(BF16)
        outs = []
        for kh in range(NKV):
            k2 = jnp.concatenate([kvp[:, kh * HD:(kh + 1) * HD], kvc[:, kh * HD:(kh + 1) * HD]], axis=0)
            v2 = jnp.concatenate([kvp[:, KVW + kh * HD:KVW + (kh + 1) * HD],
                                  kvc[:, KVW + kh * HD:KVW + (kh + 1) * HD]], axis=0)
            for g in range(GRP):
                h = kh * GRP + g
                sink = sink_ref[h]
                qh = q_ref[:, h * HD:(h + 1) * HD]
                s = lax.dot_general(qh, k2, (((1,), (1,)), ((), ())), preferred_element_type=F32)
                s = jnp.where(mask, s, -jnp.inf)
                m = jnp.maximum(jnp.max(s, axis=-1, keepdims=True), sink)
                p = jnp.exp(s - m)
                den = jnp.sum(p, axis=-1, keepdims=True) + jnp.exp(sink - m)
                o = jnp.dot(p.astype(BF16), v2, preferred_element_type=F32)
                outs.append(o / den)
        o_ref[...] = jnp.concatenate(outs, axis=-1).astype(BF16)

    @pl.when(i >= ATT_STEPS)
    def _():
        o_ref[...] = jnp.zeros_like(o_ref)


def _attn_prompt(sinks, q_all, kv_all):
    last = ATT_STEPS - 1
    nblk = SP // WIN

    def prev_block(i):
        ic = jnp.minimum(i, last)
        return (jnp.where(ic % nblk == 0, ic, ic - 1), 0)

    return pl.pallas_call(
        _attn_prompt_kernel,
        out_shape=jax.ShapeDtypeStruct((T, D), BF16),
        grid=(T // WIN,),
        in_specs=[pl.BlockSpec(memory_space=pltpu.SMEM),
                  pl.BlockSpec((WIN, D), lambda i: (jnp.minimum(i, last), 0)),
                  pl.BlockSpec((WIN, 2 * KVW), lambda i: (jnp.minimum(i, last), 0)),
                  pl.BlockSpec((WIN, 2 * KVW), prev_block)],
        out_specs=pl.BlockSpec((WIN, D), lambda i: (i, 0)),
        compiler_params=_cparams(("parallel",)),
        name="attn_prompt",
    )(sinks, q_all, kv_all, kv_all)


ATT_BB = 8
QG = SS * GRP


def _attn_sample_kernel(sink_ref, q_ref, kvn_ref, ck_ref, cv_ref, o_ref):
    rr = lax.broadcasted_iota(jnp.int32, (QG, WIN), 0)
    qi_c = rr % SS
    key_c = lax.broadcasted_iota(jnp.int32, (QG, WIN), 1)
    mask_c = (key_c > qi_c)[None]
    rr_n = lax.broadcasted_iota(jnp.int32, (QG, SS), 0)
    key_n = lax.broadcasted_iota(jnp.int32, (QG, SS), 1)
    mask_n = (key_n <= rr_n % SS)[None]
    grow = lax.broadcasted_iota(jnp.int32, (QG, 1), 0) // SS
    for kh in range(NKV):
        sink = jnp.zeros((QG, 1), F32)
        for g in range(GRP):
            sink = jnp.where(grow == g, sink_ref[kh * GRP + g], sink)
        sink = sink[None]
        q = q_ref[:, kh]
        kc = ck_ref[:, kh].astype(BF16)
        vc = cv_ref[:, kh].astype(BF16)
        kn = kvn_ref[:, :, kh * HD:(kh + 1) * HD].astype(BF16)
        vn = kvn_ref[:, :, KVW + kh * HD:KVW + (kh + 1) * HD].astype(BF16)
        s1 = jnp.einsum("bqd,bds->bqs", q, kc, preferred_element_type=F32)
        s2 = jnp.einsum("bqd,bsd->bqs", q, kn, preferred_element_type=F32)
        s1 = jnp.where(mask_c, s1, -jnp.inf)
        s2 = jnp.where(mask_n, s2, -jnp.inf)
        m = jnp.maximum(jnp.maximum(jnp.max(s1, axis=-1, keepdims=True),
                                    jnp.max(s2, axis=-1, keepdims=True)), sink)
        p1 = jnp.exp(s1 - m)
        p2 = jnp.exp(s2 - m)
        den = (jnp.sum(p1, axis=-1, keepdims=True) + jnp.sum(p2, axis=-1, keepdims=True)
               + jnp.exp(sink - m))
        o = (jnp.einsum("bqs,bds->bqd", p1.astype(BF16), vc, preferred_element_type=F32)
             + jnp.einsum("bqs,bsd->bqd", p2.astype(BF16), vn, preferred_element_type=F32))
        o_ref[:, kh] = (o / den).astype(BF16)


def _attn_sample(sinks, q_g, kv_new, ck, cv):
    return pl.pallas_call(
        _attn_sample_kernel,
        out_shape=jax.ShapeDtypeStruct((BS, NKV, QG, HD), BF16),
        grid=(BS // ATT_BB,),
        in_specs=[pl.BlockSpec(memory_space=pltpu.SMEM),
                  pl.BlockSpec((ATT_BB, NKV, QG, HD), lambda i: (i, 0, 0, 0)),
                  pl.BlockSpec((ATT_BB, SS, 2 * KVW), lambda i: (i, 0, 0)),
                  pl.BlockSpec((ATT_BB, NKV, HD, WIN), lambda i: (i, 0, 0, 0)),
                  pl.BlockSpec((ATT_BB, NKV, HD, WIN), lambda i: (i, 0, 0, 0))],
        out_specs=pl.BlockSpec((ATT_BB, NKV, QG, HD), lambda i: (i, 0, 0, 0)),
        compiler_params=_cparams(("parallel",)),
        name="attn_sample",
    )(sinks, q_g, kv_new, ck, cv)


def _merge_kernel(zs_ref, at_ref, wc_ref, wa_ref, bc_ref, ba_ref, gc_ref, ga_ref, o_ref):
    conv = jnp.dot(zs_ref[...], wc_ref[...], preferred_element_type=F32) + bc_ref[...]
    attn = jnp.dot(at_ref[...], wa_ref[...], preferred_element_type=F32) + ba_ref[...]
    o_ref[...] = (gc_ref[...] * conv + ga_ref[...] * attn).astype(BF16)


def _merge(zs, attn, w_co, w_ao, b_co, b_ao, gates):
    tm, tn = 1024, 512
    ga_off = D // tn
    return pl.pallas_call(
        _merge_kernel,
        out_shape=jax.ShapeDtypeStruct((T, D), BF16),
        grid=(T // tm, D // tn),
        in_specs=[pl.BlockSpec((tm, D), lambda i, j: (i, 0)),
                  pl.BlockSpec((tm, D), lambda i, j: (i, 0)),
                  pl.BlockSpec((D, tn), lambda i, j: (0, j)),
                  pl.BlockSpec((D, tn), lambda i, j: (0, j)),
                  pl.BlockSpec((1, tn), lambda i, j: (0, j)),
                  pl.BlockSpec((1, tn), lambda i, j: (0, j)),
                  pl.BlockSpec((tm, tn), lambda i, j: (i, j)),
                  pl.BlockSpec((tm, tn), lambda i, j: (i, ga_off + j))],
        out_specs=pl.BlockSpec((tm, tn), lambda i, j: (i, j)),
        compiler_params=_cparams(("parallel", "arbitrary")),
        name="mixer_merge",
    )(zs, attn, w_co, w_ao, b_co.reshape(1, D), b_ao.reshape(1, D), gates, gates)


LANES = 128
ROW_TILES = D // LANES


def _router_logits(h, w, b):
    h_hi = h.astype(BF16)
    h_lo = (h - h_hi.astype(F32)).astype(BF16)
    w_hi = w.astype(BF16)
    w_lo = (w - w_hi.astype(F32)).astype(BF16)
    dn = (((1,), (1,)), ((), ()))
    return (lax.dot_general(w_hi, h_hi, dn, preferred_element_type=F32)
            + lax.dot_general(w_hi, h_lo, dn, preferred_element_type=F32)
            + lax.dot_general(w_lo, h_hi, dn, preferred_element_type=F32)) + b


def _out_proj_kernel(mg_ref, xp_ref, xs_ref, w_ref, g_ref, m2p_ref, m3p_ref, m4p_ref,
                     m2s_ref, m3s_ref, m4s_ref, wr_ref, br_ref, x1_ref, h2_ref, lg_ref, *, n_prompt):
    i = pl.program_id(0)
    y = jnp.dot(mg_ref[...], w_ref[...], preferred_element_type=F32)

    def finish(x_ref, m2_ref, m3_ref, m4_ref):
        x1 = x_ref[...] + m2_ref[...] * y.reshape(x_ref.shape)
        x1_ref[...] = x1.reshape(x1_ref.shape)
        h2 = _norm_mod(x1, g_ref[...], m3_ref[...], m4_ref[...]).reshape(x1_ref.shape)
        h2_ref[...] = h2
        lg_ref[...] = _router_logits(h2, wr_ref[...], br_ref[...])

    @pl.when(i < n_prompt)
    def _():
        finish(xp_ref, m2p_ref, m3p_ref, m4p_ref)

    @pl.when(i >= n_prompt)
    def _():
        finish(xs_ref, m2s_ref, m3s_ref, m4s_ref)


def _out_proj(merged, x_prompt, x_sample, w_out_b, g2, mp, ms, w_router, b_router):
    tl = _Tiles(256)
    xp, xs, mps, mss = tl.specs()
    row = pl.BlockSpec((tl.tm, D), lambda i: (i, 0))
    return pl.pallas_call(
        functools.partial(_out_proj_kernel, n_prompt=tl.n_prompt),
        out_shape=(jax.ShapeDtypeStruct((T, D), F32),
                   jax.ShapeDtypeStruct((T, D), F32),
                   jax.ShapeDtypeStruct((NE, T), F32)),
        grid=(tl.n,),
        in_specs=[row, xp, xs, pl.BlockSpec((D, D), lambda i: (0, 0)),
                  pl.BlockSpec((1, 1, D), lambda i: (0, 0, 0)), mps, mps, mps, mss, mss, mss,
                  pl.BlockSpec((NE, D), lambda i: (0, 0)),
                  pl.BlockSpec((NE, 1), lambda i: (0, 0))],
        out_specs=(row, row, pl.BlockSpec((NE, tl.tm), lambda i: (0, i))),
        compiler_params=_cparams(("parallel",)),
        name="out_proj_norm2",
    )(merged, x_prompt, x_sample, w_out_b, g2.reshape(1, 1, D), mp[2], mp[3], mp[4], ms[2], ms[3], ms[4],
      w_router.T, b_router.reshape(NE, 1))


ROUTE_TR = 512


def _router_kernel(lg_ref, tri_ref, idx_ref, gate_ref, rank_ref, cnt_ref, carry):
    @pl.when(pl.program_id(0) == 0)
    def _():
        carry[...] = jnp.zeros_like(carry)

    eid = lax.broadcasted_iota(jnp.int32, (NE, ROUTE_TR), 0).astype(F32)
    work = lg_ref[...]
    vals, ids = [], []
    onehot = jnp.zeros((NE, ROUTE_TR), F32)
    for _ in range(TOPK):
        m = jnp.max(work, axis=0, keepdims=True)
        sel = jnp.min(jnp.where(work == m, eid, float(NE)), axis=0, keepdims=True)
        hit = eid == sel
        work = jnp.where(hit, -jnp.inf, work)
        onehot = jnp.where(hit, 1.0, onehot)
        vals.append(m)
        ids.append(sel)
    es = [jnp.exp(v - vals[0]) for v in vals]
    den = es[0] + es[1] + es[2] + es[3]
    before = jnp.dot(onehot.astype(BF16), tri_ref[...], preferred_element_type=F32) + carry[:, 0:1]
    for k in range(TOPK):
        idx_ref[k:k + 1, :] = ids[k].astype(jnp.int32)
        gate_ref[k:k + 1, :] = es[k] / den
        rank_ref[k:k + 1, :] = jnp.sum(jnp.where(eid == ids[k], before, 0.0), axis=0,
                                       keepdims=True).astype(jnp.int32)
    carry[...] = carry[...] + jnp.sum(onehot, axis=1, keepdims=True)
    cnt_ref[...] = carry[...]


def _router(logits):
    tri = jnp.triu(jnp.ones((ROUTE_TR, ROUTE_TR), F32), 1).astype(BF16)
    row = lambda i: (0, i)
    return pl.pallas_call(
        _router_kernel,
        out_shape=(jax.ShapeDtypeStruct((TOPK, T), jnp.int32),
                   jax.ShapeDtypeStruct((TOPK, T), F32),
                   jax.ShapeDtypeStruct((TOPK, T), jnp.int32),
                   jax.ShapeDtypeStruct((NE, 128), F32)),
        grid=(T // ROUTE_TR,),
        in_specs=[pl.BlockSpec((NE, ROUTE_TR), row),
                  pl.BlockSpec((ROUTE_TR, ROUTE_TR), lambda i: (0, 0))],
        out_specs=(pl.BlockSpec((TOPK, ROUTE_TR), row),
                   pl.BlockSpec((TOPK, ROUTE_TR), row),
                   pl.BlockSpec((TOPK, ROUTE_TR), row),
                   pl.BlockSpec((NE, 128), lambda i: (0, 0))),
        scratch_shapes=[pltpu.VMEM((NE, 128), F32)],
        compiler_params=_cparams(("arbitrary",)),
        name="router_top4",
    )(logits, tri)


DISPATCH_AHEAD = 1
DISPATCH_SLOTS = DISPATCH_AHEAD + 1


def _dispatch_kernel(tok_ref, nt_ref, h_hbm, o_ref, buf, sem):
    i = pl.program_id(0)
    nt = nt_ref[0]

    def issue(tile, slot):
        for r in range(SUB):
            pltpu.make_async_copy(h_hbm.at[pl.ds(tok_ref[tile * SUB + r], 1)],
                                  buf.at[slot, pl.ds(r, 1)], sem.at[slot]).start()

    @pl.when(i == 0)
    def _():
        for a in range(DISPATCH_AHEAD):
            @pl.when(a < nt)
            def _(a=a):
                issue(a, a)

    @pl.when(i + DISPATCH_AHEAD < nt)
    def _():
        issue(i + DISPATCH_AHEAD, (i + DISPATCH_AHEAD) % DISPATCH_SLOTS)

    @pl.when(i < nt)
    def _():
        slot = i % DISPATCH_SLOTS
        pltpu.make_async_copy(h_hbm.at[pl.ds(0, SUB)], buf.at[slot], sem.at[slot]).wait()
        o_ref[...] = buf[slot].astype(BF16)

    @pl.when(i >= nt)
    def _():
        o_ref[...] = jnp.zeros_like(o_ref)


def _dispatch(tok_of_slot, nt_used, h2):
    return pl.pallas_call(
        _dispatch_kernel,
        out_shape=jax.ShapeDtypeStruct((P_MAX, ROW_TILES, LANES), BF16),
        grid_spec=pltpu.PrefetchScalarGridSpec(
            num_scalar_prefetch=2,
            grid=(NT_MAX,),
            in_specs=[pl.BlockSpec(memory_space=pl.ANY)],
            out_specs=pl.BlockSpec((SUB, ROW_TILES, LANES), lambda i, tok, nt: (i, 0, 0)),
            scratch_shapes=[pltpu.VMEM((DISPATCH_SLOTS, SUB, ROW_TILES, LANES), F32),
                            pltpu.SemaphoreType.DMA((DISPATCH_SLOTS,))]),
        compiler_params=_cparams(("arbitrary",)),
        name="moe_dispatch",
    )(tok_of_slot, nt_used, h2.reshape(T, ROW_TILES, LANES)).reshape(P_MAX, D)


def _moe_kernel(ge_ref, gs_ref, gn_ref, nt_ref, w1_ref, b1_ref, w2_ref, b2_ref, sel_ref, xs_hbm, y_hbm,
                x_buf, acc, h_scr, sem_in, sem_out):
    del ge_ref
    g = pl.program_id(0)
    c = pl.program_id(1)
    nsub = gn_ref[g]
    start = gs_ref[g]

    def rows_of(r):
        return pl.ds(pl.multiple_of(r * SUB, SUB), SUB)

    def x_copy(r):
        return pltpu.make_async_copy(xs_hbm.at[pl.ds((start + r) * SUB, SUB)], x_buf.at[rows_of(r)],
                                     sem_in.at[r])

    def y_copy(r):
        return pltpu.make_async_copy(acc.at[rows_of(r)], y_hbm.at[pl.ds((start + r) * SUB, SUB)], sem_out)

    def for_subtiles(fn, n):
        def body(r, carry):
            fn(r)
            return carry
        lax.fori_loop(0, n, body, 0)

    def up_proj(r):
        return (jnp.dot(x_buf[rows_of(r), :], w1_ref[0].astype(BF16), preferred_element_type=F32)
                + b1_ref[0])

    def down_proj(r, h):
        hn = jnp.concatenate(
            [pltpu.roll(h[:, k * 128:(k + 1) * 128], 127, 1) for k in range(2 * TF // 128)], axis=1)
        glu = jnp.minimum(h, LIMIT)
        lin = jnp.clip(hn, -LIMIT, LIMIT)
        act = (glu * _sigmoid(ALPHA * glu) * (lin + 1.0)).astype(BF16)
        sel = sel_ref[...]
        act = jnp.concatenate(
            [jnp.dot(act[:, k * MXU_N:(k + 1) * MXU_N], sel, preferred_element_type=F32)
             for k in range(2 * TF // MXU_N)], axis=1).astype(BF16)
        rows = rows_of(r)
        acc[rows, :] = acc[rows, :] + jnp.dot(act, w2_ref[0].astype(BF16), preferred_element_type=F32)

    @pl.when(nsub > 0)
    def _():
        @pl.when(c == 0)
        def _():
            for_subtiles(lambda r: x_copy(r).start(), nsub)

            def init(r):
                acc[rows_of(r), :] = jnp.broadcast_to(b2_ref[0], (SUB, D))
            for_subtiles(init, nsub)
            x_copy(0).wait()

        h_scr[0] = up_proj(0)

        def step(r):
            @pl.when(c == 0)
            def _():
                x_copy(r + 1).wait()

            h = h_scr[r % 2]
            h_scr[(r + 1) % 2] = up_proj(r + 1)
            down_proj(r, h)

            @pl.when(c == NC - 1)
            def _():
                y_copy(r).start()
        for_subtiles(step, nsub - 1)
        down_proj(nsub - 1, h_scr[(nsub - 1) % 2])

        @pl.when(c == NC - 1)
        def _():
            y_copy(nsub - 1).start()
            for_subtiles(lambda r: y_copy(r).wait(), nsub)

    @pl.when((g == G_MAX - 1) & (c == NC - 1))
    def _():
        acc[0:SUB, :] = jnp.zeros((SUB, D), F32)

        def fill(tile, carry):
            cp = pltpu.make_async_copy(acc.at[pl.ds(0, SUB)], y_hbm.at[pl.ds(tile * SUB, SUB)], sem_out)
            cp.start()
            cp.wait()
            return carry
        lax.fori_loop(nt_ref[0], NT_MAX, fill, 0)


def _moe(g_expert, g_start, g_nsub, nt_used, w1, b1, w2, b2, xs):
    sel = (jnp.arange(MXU_N)[:, None] == 2 * jnp.arange(MXU_N // 2)[None, :]).astype(BF16)

    def cidx(g, c, gn):
        return jnp.where(gn[g] > 0, c, NC - 1)

    return pl.pallas_call(
        _moe_kernel,
        out_shape=jax.ShapeDtypeStruct((P_MAX, D), F32),
        grid_spec=pltpu.PrefetchScalarGridSpec(
            num_scalar_prefetch=4,
            grid=(G_MAX, NC),
            in_specs=[pl.BlockSpec((1, D, 2 * TF), lambda g, c, ge, gs, gn, nt: (ge[g], 0, cidx(g, c, gn))),
                      pl.BlockSpec((1, 1, 2 * TF), lambda g, c, ge, gs, gn, nt: (ge[g], 0, cidx(g, c, gn))),
                      pl.BlockSpec((1, TF, D), lambda g, c, ge, gs, gn, nt: (ge[g], cidx(g, c, gn), 0)),
                      pl.BlockSpec((1, 1, D), lambda g, c, ge, gs, gn, nt: (ge[g], 0, 0)),
                      pl.BlockSpec((MXU_N, MXU_N // 2), lambda g, c, ge, gs, gn, nt: (0, 0)),
                      pl.BlockSpec(memory_space=pl.ANY)],
            out_specs=pl.BlockSpec(memory_space=pl.ANY),
            scratch_shapes=[pltpu.VMEM((GSUB * SUB, D), BF16),
                            pltpu.VMEM((GSUB * SUB, D), F32),
                            pltpu.VMEM((2, SUB, 2 * TF), F32),
                            pltpu.SemaphoreType.DMA((GSUB,)),
                            pltpu.SemaphoreType.DMA]),
        compiler_params=_cparams(("arbitrary", "arbitrary")),
        name="moe_experts",
    )(g_expert, g_start, g_nsub, nt_used, w1, b1.reshape(NE, 1, 2 * DFF), w2, b2.reshape(NE, 1, D), sel, xs)


COMB_TM = 128


def _combine_kernel(slot_ref, x1_ref, gate_ref, m5_ref, g_ref, y_hbm, o_ref, buf, sem, *, tok0):
    i = pl.program_id(0)
    n = pl.num_programs(0)

    def issue(tile, slot):
        for k in range(TOPK):
            for r in range(COMB_TM):
                s = slot_ref[k * T + tok0 + tile * COMB_TM + r]
                pltpu.make_async_copy(y_hbm.at[pl.ds(s, 1)], buf.at[slot, k, pl.ds(r, 1)],
                                      sem.at[slot]).start()

    @pl.when(i == 0)
    def _():
        issue(0, 0)

    @pl.when(i + 1 < n)
    def _():
        issue(i + 1, (i + 1) % 2)

    slot = i % 2
    for k in range(TOPK):
        pltpu.make_async_copy(y_hbm.at[pl.ds(0, COMB_TM)], buf.at[slot, k], sem.at[slot]).wait()
    gates = gate_ref[...]
    f = None
    for k in range(TOPK):
        term = gates[:, k:k + 1] * buf[slot, k]
        f = term if f is None else f + term
    x2 = x1_ref[...].reshape(o_ref.shape) + m5_ref[...] * f.reshape(o_ref.shape)
    ms = jnp.mean(x2 * x2, axis=-1, keepdims=True)
    o_ref[...] = x2 * lax.rsqrt(ms + EPS) * g_ref[...]


def _combine(prompt, slots, x1, gates_t, m5, gf, y):
    tm = COMB_TM
    if prompt:
        nb, r, steps, rb0, shape3 = 1, tm, TP // tm, 0, (BP, SP, D)
        per_b = SP // tm
        x_index = lambda i, s: (i // per_b, i % per_b, 0)
        m_index = lambda i, s: (i // per_b, 0, 0)
    else:
        nb, r, steps, rb0, shape3 = tm // SS, SS, TS // tm, TP // tm, (BS, SS, D)
        x_index = lambda i, s: (i, 0, 0)
        m_index = x_index
    return pl.pallas_call(
        functools.partial(_combine_kernel, tok0=rb0 * tm),
        out_shape=jax.ShapeDtypeStruct(shape3, F32),
        grid_spec=pltpu.PrefetchScalarGridSpec(
            num_scalar_prefetch=1,
            grid=(steps,),
            in_specs=[pl.BlockSpec((tm, D), lambda i, s: (rb0 + i, 0)),
                      pl.BlockSpec((tm, TOPK), lambda i, s: (rb0 + i, 0)),
                      pl.BlockSpec((nb, 1, D), m_index),
                      pl.BlockSpec((1, 1, D), lambda i, s: (0, 0, 0)),
                      pl.BlockSpec(memory_space=pl.ANY)],
            out_specs=pl.BlockSpec((nb, r, D), x_index),
            scratch_shapes=[pltpu.VMEM((2, TOPK, tm, D), F32), pltpu.SemaphoreType.DMA((2,))]),
        compiler_params=_cparams(("arbitrary",)),
        name="moe_combine_norm",
    )(slots, x1, gates_t, m5, gf.reshape(1, 1, D), y)


def _routing_tables(idx_t, rank_t, counts):
    tiles = (counts + SUB - 1) // SUB
    tile_end = jnp.cumsum(tiles)
    tile_start = tile_end - tiles
    nt_used = tile_end[-1:]
    experts = jnp.arange(NE, dtype=jnp.int32)
    base = jnp.sum(jnp.where(idx_t[..., None] == experts, tile_start * SUB, 0), axis=-1)
    slot = base + rank_t
    tok = jnp.tile(jnp.arange(T, dtype=jnp.int32), TOPK)
    tok_of_slot = jnp.zeros((P_MAX,), jnp.int32).at[slot.reshape(-1)].set(tok)
    ngrp = (tiles + GSUB - 1) // GSUB
    grp_end = jnp.cumsum(ngrp)
    grp_start = grp_end - ngrp
    n_groups = grp_end[-1]
    gid = jnp.arange(G_MAX, dtype=jnp.int32)
    gclamp = jnp.minimum(gid, jnp.maximum(n_groups - 1, 0))
    g_expert = jnp.sum(gclamp[:, None] >= grp_end[None, :], axis=-1).astype(jnp.int32)
    pick = lambda v: jnp.sum(jnp.where(g_expert[:, None] == experts[None, :], v[None, :], 0), axis=-1)
    local = gclamp - pick(grp_start)
    g_start = pick(tile_start) + local * GSUB
    g_nsub = jnp.where(gid < n_groups, jnp.minimum(GSUB, pick(tiles) - local * GSUB), 0)
    return (slot.reshape(-1).astype(jnp.int32), tok_of_slot, nt_used.astype(jnp.int32),
            g_expert, g_start.astype(jnp.int32), g_nsub.astype(jnp.int32))


def kernel(x_prompt, x_sample, c_prompt, c_sample, cache_k, cache_v, state_conv, w_ada, b_ada, norm1_g,
           w_in, b_in, conv_dw_w, conv_dw_b, conv_ln_g, conv_ln_b, w_conv_out, b_conv_out, sinks,
           w_attn_out, b_attn_out, w_out, norm2_g, w_router, b_router, w_mlp1, b_mlp1, w_mlp2, b_mlp2,
           norm_f_g):
    mods = _modulations(jnp.concatenate([c_prompt, c_sample], axis=0), w_ada, b_ada)
    mp = [mods[:BP, k * D:(k + 1) * D].reshape(BP, 1, D) for k in range(6)]
    ms = [mods[BP:, k * D:(k + 1) * D].reshape(BS, 1, D) for k in range(6)]

    h1 = _norm1(x_prompt, x_sample, norm1_g, mp, ms)
    w_in_b = w_in.astype(BF16)
    b_in2 = b_in.reshape(1, IN_W)
    u_all = _in_proj(h1, w_in_b, b_in2, OFF_GLU_A, OFF_GLU_B, D, "glu", F32)
    q_all = _in_proj(h1, w_in_b, b_in2, OFF_Q, None, D, "plain", BF16, HD ** -0.5)
    kv_all = _in_proj(h1, w_in_b, b_in2, OFF_KV, None, 2 * KVW, "plain", F32)
    gates = _in_proj(h1, w_in_b, b_in2, OFF_GC, None, 2 * D, "sigmoid", F32)

    zs = _conv_prompt(u_all, conv_dw_w, conv_dw_b, conv_ln_g, conv_ln_b)
    u_s_t = u_all[TP:].reshape(BS, SS, D).transpose(1, 0, 2)
    zs_s_t, conv_s_t = _conv_sample(state_conv.transpose(1, 0, 2), u_s_t, conv_dw_w, conv_dw_b,
                                    conv_ln_g, conv_ln_b)
    zs = lax.dynamic_update_slice(zs, zs_s_t.transpose(1, 0, 2).reshape(TS, D), (TP, 0))
    conv_s = conv_s_t.transpose(1, 0, 2)
    conv_p = jnp.stack([u_all[(b + 1) * SP - (CW - 1):(b + 1) * SP] for b in range(BP)])

    attn = _attn_prompt(sinks, q_all, kv_all)
    q_g = (q_all[TP:].reshape(BS, SS, NKV, GRP, HD).transpose(0, 2, 3, 1, 4)
           .reshape(BS, NKV, QG, HD))
    kv_s = kv_all[TP:].reshape(BS, SS, 2 * KVW)
    attn_s = _attn_sample(sinks, q_g, kv_s, cache_k.transpose(0, 2, 3, 1), cache_v.transpose(0, 2, 3, 1))
    attn_s = (attn_s.reshape(BS, NKV, GRP, SS, HD).transpose(0, 3, 1, 2, 4).reshape(TS, D))
    attn = lax.dynamic_update_slice(attn, attn_s, (TP, 0))
    kv_tail = jnp.stack([kv_all[(b + 1) * SP - WIN:(b + 1) * SP] for b in range(BP)])
    k_p = kv_tail[:, :, :KVW].reshape(BP, WIN, NKV, HD)
    v_p = kv_tail[:, :, KVW:].reshape(BP, WIN, NKV, HD)
    k_s = jnp.concatenate([cache_k[:, SS:], kv_s[:, :, :KVW].reshape(BS, SS, NKV, HD)], axis=1)
    v_s = jnp.concatenate([cache_v[:, SS:], kv_s[:, :, KVW:].reshape(BS, SS, NKV, HD)], axis=1)

    merged = _merge(zs, attn, w_conv_out.astype(BF16), w_attn_out.astype(BF16), b_conv_out, b_attn_out,
                    gates)
    x1, h2, logits = _out_proj(merged, x_prompt, x_sample, w_out.astype(BF16), norm2_g, mp, ms,
                               w_router, b_router)

    idx_t, gate_t, rank_t, cnt = _router(logits)
    counts = cnt[:, 0].astype(jnp.int32)
    slots, tok_of_slot, nt_used, g_expert, g_start, g_nsub = _routing_tables(idx_t, rank_t, counts)

    xs = _dispatch(tok_of_slot, nt_used, h2)
    y = _moe(g_expert, g_start, g_nsub, nt_used, w_mlp1, b_mlp1, w_mlp2, b_mlp2, xs)

    gates_tok = gate_t.T
    y_prompt = _combine(True, slots, x1, gates_tok, mp[5], norm_f_g, y)
    y_sample = _combine(False, slots, x1, gates_tok, ms[5], norm_f_g, y)
    return (y_prompt, y_sample, k_p, v_p, conv_p, k_s, v_s, conv_s)
```

```python
import functools

import jax
import jax.numpy as jnp
from jax import lax
from jax.experimental import pallas as pl
from jax.experimental.pallas import tpu as pltpu

F32 = jnp.float32
BF16 = jnp.bfloat16

D = 2048
BP, SP = 4, 2048
BS, SS = 128, 8
TP = BP * SP
TS = BS * SS
T = TP + TS
HD = 64
NH = 32
NKV = 4
GRP = NH // NKV
WIN = 128
KVW = NKV * HD
CW = 31
NE = 32
TOPK = 4
DFF = 2048
ALPHA = 1.702
LIMIT = 7.0
EPS = 1e-5
OFF_GLU_A, OFF_GLU_B, OFF_Q, OFF_KV, OFF_GC = 0, D, 2 * D, 3 * D, 3 * D + 2 * KVW
IN_W = 5 * D + 2 * KVW

VMEM_LIMIT = 56 * 1024 * 1024

SUB = 256
GSUB = 8
NT_MAX = (T * TOPK) // SUB + NE
P_MAX = NT_MAX * SUB
G_MAX = -(-((T * TOPK) // SUB) // GSUB) + NE
TF = 512
NC = DFF // TF
MXU_N = 256


def _cparams(sem):
    return pltpu.CompilerParams(dimension_semantics=sem, vmem_limit_bytes=VMEM_LIMIT)


def _sigmoid(x):
    return 1.0 / (1.0 + jnp.exp(-x))


def _norm_mod(x, g, shift, scale):
    ms = jnp.mean(x * x, axis=-1, keepdims=True)
    return (x * lax.rsqrt(ms + EPS) * g) * (1.0 + scale) + shift


class _Tiles:
    def __init__(self, tm):
        self.tm = tm
        self.per_b = SP // tm
        self.n_prompt = TP // tm
        self.n = T // tm
        self.sb = tm // SS

    def prompt_x(self, i):
        ip = jnp.minimum(i, self.n_prompt - 1)
        return (ip // self.per_b, ip % self.per_b, 0)

    def prompt_mod(self, i):
        return (jnp.minimum(i, self.n_prompt - 1) // self.per_b, 0, 0)

    def sample_x(self, i):
        return (jnp.maximum(i - self.n_prompt, 0), 0, 0)

    def specs(self, tail=()):
        wrap = lambda f: (lambda i, *_: f(i))
        xp = pl.BlockSpec((1, self.tm, D), wrap(self.prompt_x))
        xs = pl.BlockSpec((self.sb, SS, D), wrap(self.sample_x))
        mp = pl.BlockSpec((1, 1, D), wrap(self.prompt_mod))
        ms = pl.BlockSpec((self.sb, 1, D), wrap(self.sample_x))
        return xp, xs, mp, ms


def _mod_kernel(c_ref, w_ref, b_ref, o_ref):
    c = c_ref[...]
    sc = (c * _sigmoid(c)).astype(BF16)
    o_ref[...] = jnp.dot(sc, w_ref[...].astype(BF16), preferred_element_type=F32) + b_ref[...]


def _modulations(c_all, w_ada, b_ada):
    nb = c_all.shape[0]
    tn = 1024
    return pl.pallas_call(
        _mod_kernel,
        out_shape=jax.ShapeDtypeStruct((nb, 6 * D), F32),
        grid=(6 * D // tn,),
        in_specs=[pl.BlockSpec((nb, D), lambda j: (0, 0)),
                  pl.BlockSpec((D, tn), lambda j: (0, j)),
                  pl.BlockSpec((1, tn), lambda j: (0, j))],
        out_specs=pl.BlockSpec((nb, tn), lambda j: (0, j)),
        compiler_params=_cparams(("parallel",)),
        name="modulations",
    )(c_all, w_ada, b_ada.reshape(1, -1))


def _norm1_kernel(xp_ref, xs_ref, g_ref, shp_ref, scp_ref, shs_ref, scs_ref, o_ref, *, n_prompt):
    i = pl.program_id(0)

    @pl.when(i < n_prompt)
    def _():
        h = _norm_mod(xp_ref[...], g_ref[...], shp_ref[...], scp_ref[...])
        o_ref[...] = h.reshape(o_ref.shape).astype(BF16)

    @pl.when(i >= n_prompt)
    def _():
        h = _norm_mod(xs_ref[...], g_ref[...], shs_ref[...], scs_ref[...])
        o_ref[...] = h.reshape(o_ref.shape).astype(BF16)


def _norm1(x_prompt, x_sample, g, mp, ms):
    tl = _Tiles(256)
    xp, xs, mps, mss = tl.specs()
    return pl.pallas_call(
        functools.partial(_norm1_kernel, n_prompt=tl.n_prompt),
        out_shape=jax.ShapeDtypeStruct((T, D), BF16),
        grid=(tl.n,),
        in_specs=[xp, xs, pl.BlockSpec((1, 1, D), lambda i: (0, 0, 0)), mps, mps, mss, mss],
        out_specs=pl.BlockSpec((tl.tm, D), lambda i: (i, 0)),
        compiler_params=_cparams(("parallel",)),
        name="norm1_adaln",
    )(x_prompt, x_sample, g.reshape(1, 1, D), mp[0], mp[1], ms[0], ms[1])


def _in_proj_kernel(*refs, mode, out_scale):
    if mode == "glu":
        h_ref, wa_ref, wb_ref, ba_ref, bb_ref, o_ref = refs
    else:
        h_ref, wa_ref, ba_ref, o_ref = refs
    h = h_ref[...]
    a = jnp.dot(h, wa_ref[...], preferred_element_type=F32) + ba_ref[...]
    if mode == "glu":
        b = jnp.dot(h, wb_ref[...], preferred_element_type=F32) + bb_ref[...]
        a = a * _sigmoid(b)
    elif mode == "sigmoid":
        a = _sigmoid(a)
    elif out_scale != 1.0:
        a = a * out_scale
    o_ref[...] = a.astype(o_ref.dtype)


def _in_proj(h, w_b, b_in2, col_a, col_b, n_cols, mode, out_dtype, out_scale=1.0):
    tm, tn = 1536, 512
    ca, cb = col_a // tn, (col_b // tn if col_b is not None else 0)
    in_specs = [pl.BlockSpec((tm, D), lambda i, j: (i, 0)),
                pl.BlockSpec((D, tn), lambda i, j: (0, ca + j))]
    args = [h, w_b]
    if mode == "glu":
        in_specs.append(pl.BlockSpec((D, tn), lambda i, j: (0, cb + j)))
        args.append(w_b)
    in_specs.append(pl.BlockSpec((1, tn), lambda i, j: (0, ca + j)))
    args.append(b_in2)
    if mode == "glu":
        in_specs.append(pl.BlockSpec((1, tn), lambda i, j: (0, cb + j)))
        args.append(b_in2)
    return pl.pallas_call(
        functools.partial(_in_proj_kernel, mode=mode, out_scale=out_scale),
        out_shape=jax.ShapeDtypeStruct((T, n_cols), out_dtype),
        grid=(T // tm, n_cols // tn),
        in_specs=in_specs,
        out_specs=pl.BlockSpec((tm, tn), lambda i, j: (i, j)),
        compiler_params=_cparams(("parallel", "arbitrary")),
        name="in_proj_" + mode,
    )(*args)


CONV_TL = 128
HALO = 32
CONV_STEPS = TP // CONV_TL


def _ln_swish(z, g, b):
    mu = jnp.mean(z, axis=-1, keepdims=True)
    d = z - mu
    var = jnp.mean(d * d, axis=-1, keepdims=True)
    zn = d * lax.rsqrt(var + EPS) * g + b
    return zn * _sigmoid(zn)


def _conv_prompt_kernel(u_ref, halo_ref, w_ref, dwb_ref, g_ref, b_ref, o_ref, win, z_scr):
    i = pl.program_id(0)

    @pl.when(i < CONV_STEPS)
    def _():
        first = (i % (SP // CONV_TL)) == 0
        base = HALO - (CW - 1)
        for s in range(D // 128):
            cs = slice(s * 128, (s + 1) * 128)
            win[s, 0:HALO, :] = jnp.where(first, 0.0, halo_ref[:, cs])
            win[s, HALO:HALO + CONV_TL, :] = u_ref[:, cs]
        for s in range(D // 128):
            cs = slice(s * 128, (s + 1) * 128)
            acc = None
            for rho in range(8):
                offs = [base + j for j in range(CW) if (base + j) % 8 == rho]
                seg = win[s, offs[0]:offs[-1] + CONV_TL, :]
                for o in offs:
                    j = o - base
                    term = seg[o - offs[0]:o - offs[0] + CONV_TL] * w_ref[j:j + 1, cs]
                    acc = term if acc is None else acc + term
            z_scr[:, cs] = acc + dwb_ref[:, cs]
        o_ref[...] = _ln_swish(z_scr[...], g_ref[...], b_ref[...]).astype(BF16)

    @pl.when(i >= CONV_STEPS)
    def _():
        o_ref[...] = jnp.zeros_like(o_ref)


def _conv_prompt(u_all, dw_w, dw_b, ln_g, ln_b):
    hb = CONV_TL // HALO
    last = CONV_STEPS - 1
    const = lambda i: (0, 0)
    return pl.pallas_call(
        _conv_prompt_kernel,
        out_shape=jax.ShapeDtypeStruct((T, D), BF16),
        grid=(T // CONV_TL,),
        in_specs=[pl.BlockSpec((CONV_TL, D), lambda i: (jnp.minimum(i, last), 0)),
                  pl.BlockSpec((HALO, D), lambda i: (jnp.maximum(jnp.minimum(i, last) * hb - 1, 0), 0)),
                  pl.BlockSpec((CW, D), const),
                  pl.BlockSpec((1, D), const),
                  pl.BlockSpec((1, D), const),
                  pl.BlockSpec((1, D), const)],
        out_specs=pl.BlockSpec((CONV_TL, D), lambda i: (i, 0)),
        scratch_shapes=[pltpu.VMEM((D // 128, HALO + CONV_TL, 128), F32), pltpu.VMEM((CONV_TL, D), F32)],
        compiler_params=_cparams(("parallel",)),
        name="conv_prompt",
    )(u_all, u_all, dw_w, dw_b.reshape(1, D), ln_g.reshape(1, D), ln_b.reshape(1, D))


CONV_BB = 16


def _conv_sample_kernel(st_ref, u_ref, w_ref, dwb_ref, g_ref, b_ref, o_ref, ns_ref, win, z_scr):
    win[0:CW - 1] = st_ref[...]
    win[CW - 1:CW - 1 + SS] = u_ref[...]
    ns_ref[0:CW - 1 - SS] = st_ref[SS:CW - 1]
    ns_ref[CW - 1 - SS:CW - 1] = u_ref[...]
    for s in range(D // 128):
        cs = slice(s * 128, (s + 1) * 128)
        acc = None
        for j in range(CW):
            term = win[j:j + SS, :, cs] * w_ref[j:j + 1, cs][None]
            acc = term if acc is None else acc + term
        z_scr[:, :, cs] = acc + dwb_ref[:, cs][None]
    o_ref[...] = _ln_swish(z_scr[...], g_ref[...][None], b_ref[...][None]).astype(BF16)


def _conv_sample(state_t, u_t, dw_w, dw_b, ln_g, ln_b):
    const = lambda i: (0, 0)
    batch = lambda i: (0, i, 0)
    return pl.pallas_call(
        _conv_sample_kernel,
        out_shape=(jax.ShapeDtypeStruct((SS, BS, D), BF16), jax.ShapeDtypeStruct((CW - 1, BS, D), F32)),
        grid=(BS // CONV_BB,),
        in_specs=[pl.BlockSpec((CW - 1, CONV_BB, D), batch),
                  pl.BlockSpec((SS, CONV_BB, D), batch),
                  pl.BlockSpec((CW, D), const),
                  pl.BlockSpec((1, D), const),
                  pl.BlockSpec((1, D), const),
                  pl.BlockSpec((1, D), const)],
        out_specs=(pl.BlockSpec((SS, CONV_BB, D), batch),
                   pl.BlockSpec((CW - 1, CONV_BB, D), batch)),
        scratch_shapes=[pltpu.VMEM((CW - 1 + SS, CONV_BB, D), F32), pltpu.VMEM((SS, CONV_BB, D), F32)],
        compiler_params=_cparams(("parallel",)),
        name="conv_sample",
    )(state_t, u_t, dw_w, dw_b.reshape(1, D), ln_g.reshape(1, D), ln_b.reshape(1, D))


ATT_STEPS = TP // WIN


def _attn_prompt_kernel(sink_ref, q_ref, kvc_ref, kvp_ref, o_ref):
    i = pl.program_id(0)

    @pl.when(i < ATT_STEPS)
    def _():
        n = i % (SP // WIN)
        row = lax.broadcasted_iota(jnp.int32, (WIN, 2 * WIN), 0)
        col = lax.broadcasted_iota(jnp.int32, (WIN, 2 * WIN), 1)
        diff = row + WIN - col
        first_key = jnp.where(n > 0, 0, WIN)
        mask = (diff >= 0) & (diff < WIN) & (col >= first_key)
        kvc = kvc_ref[...].astype(BF16)
        kvp = kvp_ref[...].astype(BF16)
        outs = []
        for kh in range(NKV):
            k2 = jnp.concatenate([kvp[:, kh * HD:(kh + 1) * HD], kvc[:, kh * HD:(kh + 1) * HD]], axis=0)
            v2 = jnp.concatenate([kvp[:, KVW + kh * HD:KVW + (kh + 1) * HD],
                                  kvc[:, KVW + kh * HD:KVW + (kh + 1) * HD]], axis=0)
            for g in range(GRP):
                h = kh * GRP + g
                sink = sink_ref[h]
                qh = q_ref[:, h * HD:(h + 1) * HD]
                s = lax.dot_general(qh, k2, (((1,), (1,)), ((), ())), preferred_element_type=F32)
                s = jnp.where(mask, s, -jnp.inf)
                m = jnp.maximum(jnp.max(s, axis=-1, keepdims=True), sink)
                p = jnp.exp(s - m)
                den = jnp.sum(p, axis=-1, keepdims=True) + jnp.exp(sink - m)
                o = jnp.dot(p.astype(BF16), v2, preferred_element_type=F32)
                outs.append(o / den)
        o_ref[...] = jnp.concatenate(outs, axis=-1).astype(BF16)

    @pl.when(i >= ATT_STEPS)
    def _():
        o_ref[...] = jnp.zeros_like(o_ref)


def _attn_prompt(sinks, q_all, kv_all):
    last = ATT_STEPS - 1
    nblk = SP // WIN

    def prev_block(i):
        ic = jnp.minimum(i, last)
        return (jnp.where(ic % nblk == 0, ic, ic - 1), 0)

    return pl.pallas_call(
        _attn_prompt_kernel,
        out_shape=jax.ShapeDtypeStruct((T, D), BF16),
        grid=(T // WIN,),
        in_specs=[pl.BlockSpec(memory_space=pltpu.SMEM),
                  pl.BlockSpec((WIN, D), lambda i: (jnp.minimum(i, last), 0)),
                  pl.BlockSpec((WIN, 2 * KVW), lambda i: (jnp.minimum(i, last), 0)),
                  pl.BlockSpec((WIN, 2 * KVW), prev_block)],
        out_specs=pl.BlockSpec((WIN, D), lambda i: (i, 0)),
        compiler_params=_cparams(("parallel",)),
        name="attn_prompt",
    )(sinks, q_all, kv_all, kv_all)


ATT_BB = 8
QG = SS * GRP


def _attn_sample_kernel(sink_ref, q_ref, kvn_ref, ck_ref, cv_ref, o_ref):
    rr = lax.broadcasted_iota(jnp.int32, (QG, WIN), 0)
    qi_c = rr % SS
    key_c = lax.broadcasted_iota(jnp.int32, (QG, WIN), 1)
    mask_c = (key_c > qi_c)[None]
    rr_n = lax.broadcasted_iota(jnp.int32, (QG, SS), 0)
    key_n = lax.broadcasted_iota(jnp.int32, (QG, SS), 1)
    mask_n = (key_n <= rr_n % SS)[None]
    grow = lax.broadcasted_iota(jnp.int32, (QG, 1), 0) // SS
    for kh in range(NKV):
        sink = jnp.zeros((QG, 1), F32)
        for g in range(GRP):
            sink = jnp.where(grow == g, sink_ref[kh * GRP + g], sink)
        sink = sink[None]
        q = q_ref[:, kh]
        kc = ck_ref[:, kh].astype(BF16)
        vc = cv_ref[:, kh].astype(BF16)
        kn = kvn_ref[:, :, kh * HD:(kh + 1) * HD].astype(BF16)
        vn = kvn_ref[:, :, KVW + kh * HD:KVW + (kh + 1) * HD].astype(BF16)
        s1 = jnp.einsum("bqd,bds->bqs", q, kc, preferred_element_type=F32)
        s2 = jnp.einsum("bqd,bsd->bqs", q, kn, preferred_element_type=F32)
        s1 = jnp.where(mask_c, s1, -jnp.inf)
        s2 = jnp.where(mask_n, s2, -jnp.inf)
        m = jnp.maximum(jnp.maximum(jnp.max(s1, axis=-1, keepdims=True),
                                    jnp.max(s2, axis=-1, keepdims=True)), sink)
        p1 = jnp.exp(s1 - m)
        p2 = jnp.exp(s2 - m)
        den = (jnp.sum(p1, axis=-1, keepdims=True) + jnp.sum(p2, axis=-1, keepdims=True)
               + jnp.exp(sink - m))
        o = (jnp.einsum("bqs,bds->bqd", p1.astype(BF16), vc, preferred_element_type=F32)
             + jnp.einsum("bqs,bsd->bqd", p2.astype(BF16), vn, preferred_element_type=F32))
        o_ref[:, kh] = (o / den).astype(BF16)


def _attn_sample(sinks, q_g, kv_new, ck, cv):
    return pl.pallas_call(
        _attn_sample_kernel,
        out_shape=jax.ShapeDtypeStruct((BS, NKV, QG, HD), BF16),
        grid=(BS // ATT_BB,),
        in_specs=[pl.BlockSpec(memory_space=pltpu.SMEM),
                  pl.BlockSpec((ATT_BB, NKV, QG, HD), lambda i: (i, 0, 0, 0)),
                  pl.BlockSpec((ATT_BB, SS, 2 * KVW), lambda i: (i, 0, 0)),
                  pl.BlockSpec((ATT_BB, NKV, HD, WIN), lambda i: (i, 0, 0, 0)),
                  pl.BlockSpec((ATT_BB, NKV, HD, WIN), lambda i: (i, 0, 0, 0))],
        out_specs=pl.BlockSpec((ATT_BB, NKV, QG, HD), lambda i: (i, 0, 0, 0)),
        compiler_params=_cparams(("parallel",)),
        name="attn_sample",
    )(sinks, q_g, kv_new, ck, cv)


def _merge_kernel(zs_ref, at_ref, wc_ref, wa_ref, bc_ref, ba_ref, gc_ref, ga_ref, o_ref):
    conv = jnp.dot(zs_ref[...], wc_ref[...], preferred_element_type=F32) + bc_ref[...]
    attn = jnp.dot(at_ref[...], wa_ref[...], preferred_element_type=F32) + ba_ref[...]
    o_ref[...] = (gc_ref[...] * conv + ga_ref[...] * attn).astype(BF16)


def _merge(zs, attn, w_co, w_ao, b_co, b_ao, gates):
    tm, tn = 1024, 512
    ga_off = D // tn
    return pl.pallas_call(
        _merge_kernel,
        out_shape=jax.ShapeDtypeStruct((T, D), BF16),
        grid=(T // tm, D // tn),
        in_specs=[pl.BlockSpec((tm, D), lambda i, j: (i, 0)),
                  pl.BlockSpec((tm, D), lambda i, j: (i, 0)),
                  pl.BlockSpec((D, tn), lambda i, j: (0, j)),
                  pl.BlockSpec((D, tn), lambda i, j: (0, j)),
                  pl.BlockSpec((1, tn), lambda i, j: (0, j)),
                  pl.BlockSpec((1, tn), lambda i, j: (0, j)),
                  pl.BlockSpec((tm, tn), lambda i, j: (i, j)),
                  pl.BlockSpec((tm, tn), lambda i, j: (i, ga_off + j))],
        out_specs=pl.BlockSpec((tm, tn), lambda i, j: (i, j)),
        compiler_params=_cparams(("parallel", "arbitrary")),
        name="mixer_merge",
    )(zs, attn, w_co, w_ao, b_co.reshape(1, D), b_ao.reshape(1, D), gates, gates)


def _router_logits(h, w, b):
    h_hi = h.astype(BF16)
    h_lo = (h - h_hi.astype(F32)).astype(BF16)
    w_hi = w.astype(BF16)
    w_lo = (w - w_hi.astype(F32)).astype(BF16)
    dn = (((1,), (1,)), ((), ()))
    return (lax.dot_general(w_hi, h_hi, dn, preferred_element_type=F32)
            + lax.dot_general(w_hi, h_lo, dn, preferred_element_type=F32)
            + lax.dot_general(w_lo, h_hi, dn, preferred_element_type=F32)) + b


def _out_proj_kernel(mg_ref, xp_ref, xs_ref, w_ref, g_ref, m2p_ref, m3p_ref, m4p_ref,
                     m2s_ref, m3s_ref, m4s_ref, wr_ref, br_ref, x1_ref, h2_ref, lg_ref, *, n_prompt):
    i = pl.program_id(0)
    y = jnp.dot(mg_ref[...], w_ref[...], preferred_element_type=F32)

    def finish(x_ref, m2_ref, m3_ref, m4_ref):
        x1 = x_ref[...] + m2_ref[...] * y.reshape(x_ref.shape)
        x1_ref[...] = x1.reshape(x1_ref.shape)
        h2 = _norm_mod(x1, g_ref[...], m3_ref[...], m4_ref[...]).reshape(x1_ref.shape)
        h2_ref[...] = h2
        lg_ref[...] = _router_logits(h2, wr_ref[...], br_ref[...])

    @pl.when(i < n_prompt)
    def _():
        finish(xp_ref, m2p_ref, m3p_ref, m4p_ref)

    @pl.when(i >= n_prompt)
    def _():
        finish(xs_ref, m2s_ref, m3s_ref, m4s_ref)


def _out_proj(merged, x_prompt, x_sample, w_out_b, g2, mp, ms, w_router, b_router):
    tl = _Tiles(256)
    xp, xs, mps, mss = tl.specs()
    row = pl.BlockSpec((tl.tm, D), lambda i: (i, 0))
    return pl.pallas_call(
        functools.partial(_out_proj_kernel, n_prompt=tl.n_prompt),
        out_shape=(jax.ShapeDtypeStruct((T, D), F32),
                   jax.ShapeDtypeStruct((T, D), F32),
                   jax.ShapeDtypeStruct((NE, T), F32)),
        grid=(tl.n,),
        in_specs=[row, xp, xs, pl.BlockSpec((D, D), lambda i: (0, 0)),
                  pl.BlockSpec((1, 1, D), lambda i: (0, 0, 0)), mps, mps, mps, mss, mss, mss,
                  pl.BlockSpec((NE, D), lambda i: (0, 0)),
                  pl.BlockSpec((NE, 1), lambda i: (0, 0))],
        out_specs=(row, row, pl.BlockSpec((NE, tl.tm), lambda i: (0, i))),
        compiler_params=_cparams(("parallel",)),
        name="out_proj_norm2",
    )(merged, x_prompt, x_sample, w_out_b, g2.reshape(1, 1, D), mp[2], mp[3], mp[4], ms[2], ms[3], ms[4],
      w_router.T, b_router.reshape(NE, 1))


ROUTE_TR = 512


def _router_kernel(lg_ref, tri_ref, idx_ref, gate_ref, rank_ref, cnt_ref, carry):
    @pl.when(pl.program_id(0) == 0)
    def _():
        carry[...] = jnp.zeros_like(carry)

    eid = lax.broadcasted_iota(jnp.int32, (NE, ROUTE_TR), 0).astype(F32)
    work = lg_ref[...]
    vals, ids = [], []
    onehot = jnp.zeros((NE, ROUTE_TR), F32)
    for _ in range(TOPK):
        m = jnp.max(work, axis=0, keepdims=True)
        sel = jnp.min(jnp.where(work == m, eid, float(NE)), axis=0, keepdims=True)
        hit = eid == sel
        work = jnp.where(hit, -jnp.inf, work)
        onehot = jnp.where(hit, 1.0, onehot)
        vals.append(m)
        ids.append(sel)
    es = [jnp.exp(v - vals[0]) for v in vals]
    den = es[0] + es[1] + es[2] + es[3]
    before = jnp.dot(onehot.astype(BF16), tri_ref[...], preferred_element_type=F32) + carry[:, 0:1]
    for k in range(TOPK):
        idx_ref[k:k + 1, :] = ids[k].astype(jnp.int32)
        gate_ref[k:k + 1, :] = es[k] / den
        rank_ref[k:k + 1, :] = jnp.sum(jnp.where(eid == ids[k], before, 0.0), axis=0,
                                       keepdims=True).astype(jnp.int32)
    carry[...] = carry[...] + jnp.sum(onehot, axis=1, keepdims=True)
    cnt_ref[...] = carry[...]


def _router(logits):
    tri = jnp.triu(jnp.ones((ROUTE_TR, ROUTE_TR), F32), 1).astype(BF16)
    row = lambda i: (0, i)
    return pl.pallas_call(
        _router_kernel,
        out_shape=(jax.ShapeDtypeStruct((TOPK, T), jnp.int32),
                   jax.ShapeDtypeStruct((TOPK, T), F32),
                   jax.ShapeDtypeStruct((TOPK, T), jnp.int32),
                   jax.ShapeDtypeStruct((NE, 128), F32)),
        grid=(T // ROUTE_TR,),
        in_specs=[pl.BlockSpec((NE, ROUTE_TR), row),
                  pl.BlockSpec((ROUTE_TR, ROUTE_TR), lambda i: (0, 0))],
        out_specs=(pl.BlockSpec((TOPK, ROUTE_TR), row),
                   pl.BlockSpec((TOPK, ROUTE_TR), row),
                   pl.BlockSpec((TOPK, ROUTE_TR), row),
                   pl.BlockSpec((NE, 128), lambda i: (0, 0))),
        scratch_shapes=[pltpu.VMEM((NE, 128), F32)],
        compiler_params=_cparams(("arbitrary",)),
        name="router_top4",
    )(logits, tri)


DISPATCH_AHEAD = 1
DISPATCH_SLOTS = DISPATCH_AHEAD + 1
SCALAR_UNROLL = 8


def _dispatch_kernel(slot_ref, nt_ref, pad_lo_ref, pad_hi_ref, h_hbm, o_ref, tok_ref, buf, sem):
    i = pl.program_id(0)
    nt = nt_ref[0]

    @pl.when(i == 0)
    def _():
        for e in range(NE):
            def zero(j, c):
                tok_ref[j] = 0
                return c
            lax.fori_loop(pad_lo_ref[e], pad_hi_ref[e], zero, 0)
        for k in range(TOPK):
            def put(j, c, k=k):
                for u in range(SCALAR_UNROLL):
                    t = j * SCALAR_UNROLL + u
                    tok_ref[slot_ref[k * T + t]] = t
                return c
            lax.fori_loop(0, T // SCALAR_UNROLL, put, 0)

    def issue(tile, slot):
        for r in range(SUB):
            pltpu.make_async_copy(h_hbm.at[pl.ds(tok_ref[tile * SUB + r], 1)],
                                  buf.at[slot, pl.ds(r, 1)], sem.at[slot]).start(priority=r % 2)

    @pl.when(i == 0)
    def _():
        for a in range(DISPATCH_AHEAD):
            @pl.when(a < nt)
            def _(a=a):
                issue(a, a)

    @pl.when(i + DISPATCH_AHEAD < nt)
    def _():
        issue(i + DISPATCH_AHEAD, (i + DISPATCH_AHEAD) % DISPATCH_SLOTS)

    @pl.when(i < nt)
    def _():
        slot = i % DISPATCH_SLOTS
        pltpu.make_async_copy(h_hbm.at[pl.ds(0, SUB)], buf.at[slot], sem.at[slot]).wait()
        o_ref[...] = buf[slot].astype(BF16)

    @pl.when(i >= nt)
    def _():
        o_ref[...] = jnp.zeros_like(o_ref)


def _dispatch(slots, nt_used, pad_lo, pad_hi, h2):
    return pl.pallas_call(
        _dispatch_kernel,
        out_shape=jax.ShapeDtypeStruct((P_MAX, D), BF16),
        grid_spec=pltpu.PrefetchScalarGridSpec(
            num_scalar_prefetch=4,
            grid=(NT_MAX,),
            in_specs=[pl.BlockSpec(memory_space=pl.ANY)],
            out_specs=pl.BlockSpec((SUB, D), lambda i, *_: (i, 0)),
            scratch_shapes=[pltpu.SMEM((P_MAX,), jnp.int32),
                            pltpu.VMEM((DISPATCH_SLOTS, SUB, D), F32),
                            pltpu.SemaphoreType.DMA((DISPATCH_SLOTS,))]),
        compiler_params=_cparams(("arbitrary",)),
        name="moe_dispatch",
    )(slots, nt_used, pad_lo, pad_hi, h2)


def _moe_kernel(ge_ref, gs_ref, gn_ref, nt_ref, w1_ref, b1_ref, w2_ref, b2_ref, sel_ref, xs_hbm, y_hbm,
                x_buf, acc, h_scr, sem_in, sem_out):
    del ge_ref
    g = pl.program_id(0)
    c = pl.program_id(1)
    nsub = gn_ref[g]
    start = gs_ref[g]

    def rows_of(r):
        return pl.ds(pl.multiple_of(r * SUB, SUB), SUB)

    def x_copy(r):
        return pltpu.make_async_copy(xs_hbm.at[pl.ds((start + r) * SUB, SUB)], x_buf.at[rows_of(r)],
                                     sem_in.at[r])

    def y_copy(r):
        return pltpu.make_async_copy(acc.at[rows_of(r)], y_hbm.at[pl.ds((start + r) * SUB, SUB)], sem_out)

    def for_subtiles(fn, n):
        def body(r, carry):
            fn(r)
            return carry
        lax.fori_loop(0, n, body, 0)

    def up_proj(r):
        return (jnp.dot(x_buf[rows_of(r), :], w1_ref[0].astype(BF16), preferred_element_type=F32)
                + b1_ref[0])

    def down_proj(r, h):
        hn = jnp.concatenate(
            [pltpu.roll(h[:, k * 128:(k + 1) * 128], 127, 1) for k in range(2 * TF // 128)], axis=1)
        glu = jnp.minimum(h, LIMIT)
        lin = jnp.clip(hn, -LIMIT, LIMIT)
        act = (glu * _sigmoid(ALPHA * glu) * (lin + 1.0)).astype(BF16)
        sel = sel_ref[...]
        act = jnp.concatenate(
            [jnp.dot(act[:, k * MXU_N:(k + 1) * MXU_N], sel, preferred_element_type=F32)
             for k in range(2 * TF // MXU_N)], axis=1).astype(BF16)
        rows = rows_of(r)
        acc[rows, :] = acc[rows, :] + jnp.dot(act, w2_ref[0].astype(BF16), preferred_element_type=F32)

    @pl.when(nsub > 0)
    def _():
        @pl.when(c == 0)
        def _():
            for_subtiles(lambda r: x_copy(r).start(), nsub)

            def init(r):
                acc[rows_of(r), :] = jnp.broadcast_to(b2_ref[0], (SUB, D))
            for_subtiles(init, nsub)
            x_copy(0).wait()

        h_scr[0] = up_proj(0)

        def step(r):
            @pl.when(c == 0)
            def _():
                x_copy(r + 1).wait()

            h = h_scr[r % 2]
            h_scr[(r + 1) % 2] = up_proj(r + 1)
            down_proj(r, h)

            @pl.when(c == NC - 1)
            def _():
                y_copy(r).start()
        for_subtiles(step, nsub - 1)
        down_proj(nsub - 1, h_scr[(nsub - 1) % 2])

        @pl.when(c == NC - 1)
        def _():
            y_copy(nsub - 1).start()
            for_subtiles(lambda r: y_copy(r).wait(), nsub)

    @pl.when((g == G_MAX - 1) & (c == NC - 1))
    def _():
        acc[0:SUB, :] = jnp.zeros((SUB, D), F32)

        def fill(tile, carry):
            cp = pltpu.make_async_copy(acc.at[pl.ds(0, SUB)], y_hbm.at[pl.ds(tile * SUB, SUB)], sem_out)
            cp.start()
            cp.wait()
            return carry
        lax.fori_loop(nt_ref[0], NT_MAX, fill, 0)


def _moe(g_expert, g_start, g_nsub, nt_used, w1, b1, w2, b2, xs):
    sel = (jnp.arange(MXU_N)[:, None] == 2 * jnp.arange(MXU_N // 2)[None, :]).astype(BF16)

    def cidx(g, c, gn):
        return jnp.where(gn[g] > 0, c, NC - 1)

    return pl.pallas_call(
        _moe_kernel,
        out_shape=jax.ShapeDtypeStruct((P_MAX, D), F32),
        grid_spec=pltpu.PrefetchScalarGridSpec(
            num_scalar_prefetch=4,
            grid=(G_MAX, NC),
            in_specs=[pl.BlockSpec((1, D, 2 * TF), lambda g, c, ge, gs, gn, nt: (ge[g], 0, cidx(g, c, gn))),
                      pl.BlockSpec((1, 1, 2 * TF), lambda g, c, ge, gs, gn, nt: (ge[g], 0, cidx(g, c, gn))),
                      pl.BlockSpec((1, TF, D), lambda g, c, ge, gs, gn, nt: (ge[g], cidx(g, c, gn), 0)),
                      pl.BlockSpec((1, 1, D), lambda g, c, ge, gs, gn, nt: (ge[g], 0, 0)),
                      pl.BlockSpec((MXU_N, MXU_N // 2), lambda g, c, ge, gs, gn, nt: (0, 0)),
                      pl.BlockSpec(memory_space=pl.ANY)],
            out_specs=pl.BlockSpec(memory_space=pl.ANY),
            scratch_shapes=[pltpu.VMEM((GSUB * SUB, D), BF16),
                            pltpu.VMEM((GSUB * SUB, D), F32),
                            pltpu.VMEM((2, SUB, 2 * TF), F32),
                            pltpu.SemaphoreType.DMA((GSUB,)),
                            pltpu.SemaphoreType.DMA]),
        compiler_params=_cparams(("arbitrary", "arbitrary")),
        name="moe_experts",
    )(g_expert, g_start, g_nsub, nt_used, w1, b1.reshape(NE, 1, 2 * DFF), w2, b2.reshape(NE, 1, D), sel, xs)


COMB_TM = 128


def _combine_kernel(slot_ref, x1_ref, gate_ref, m5_ref, g_ref, y_hbm, o_ref, buf, sem, *, tok0):
    i = pl.program_id(0)
    n = pl.num_programs(0)

    def issue(tile, slot):
        for k in range(TOPK):
            for r in range(COMB_TM):
                s = slot_ref[k * T + tok0 + tile * COMB_TM + r]
                pltpu.make_async_copy(y_hbm.at[pl.ds(s, 1)], buf.at[slot, k, pl.ds(r, 1)],
                                      sem.at[slot]).start(priority=r % 2)

    @pl.when(i == 0)
    def _():
        issue(0, 0)

    @pl.when(i + 1 < n)
    def _():
        issue(i + 1, (i + 1) % 2)

    slot = i % 2
    for k in range(TOPK):
        pltpu.make_async_copy(y_hbm.at[pl.ds(0, COMB_TM)], buf.at[slot, k], sem.at[slot]).wait()
    gates = gate_ref[...]
    f = None
    for k in range(TOPK):
        term = gates[:, k:k + 1] * buf[slot, k]
        f = term if f is None else f + term
    x2 = x1_ref[...].reshape(o_ref.shape) + m5_ref[...] * f.reshape(o_ref.shape)
    ms = jnp.mean(x2 * x2, axis=-1, keepdims=True)
    o_ref[...] = x2 * lax.rsqrt(ms + EPS) * g_ref[...]


def _combine(prompt, slots, x1, gates_t, m5, gf, y):
    tm = COMB_TM
    if prompt:
        nb, r, steps, rb0, shape3 = 1, tm, TP // tm, 0, (BP, SP, D)
        per_b = SP // tm
        x_index = lambda i, s: (i // per_b, i % per_b, 0)
        m_index = lambda i, s: (i // per_b, 0, 0)
    else:
        nb, r, steps, rb0, shape3 = tm // SS, SS, TS // tm, TP // tm, (BS, SS, D)
        x_index = lambda i, s: (i, 0, 0)
        m_index = x_index
    return pl.pallas_call(
        functools.partial(_combine_kernel, tok0=rb0 * tm),
        out_shape=jax.ShapeDtypeStruct(shape3, F32),
        grid_spec=pltpu.PrefetchScalarGridSpec(
            num_scalar_prefetch=1,
            grid=(steps,),
            in_specs=[pl.BlockSpec((tm, D), lambda i, s: (rb0 + i, 0)),
                      pl.BlockSpec((tm, TOPK), lambda i, s: (rb0 + i, 0)),
                      pl.BlockSpec((nb, 1, D), m_index),
                      pl.BlockSpec((1, 1, D), lambda i, s: (0, 0, 0)),
                      pl.BlockSpec(memory_space=pl.ANY)],
            out_specs=pl.BlockSpec((nb, r, D), x_index),
            scratch_shapes=[pltpu.VMEM((2, TOPK, tm, D), F32), pltpu.SemaphoreType.DMA((2,))]),
        compiler_params=_cparams(("arbitrary",)),
        name="moe_combine_norm",
    )(slots, x1, gates_t, m5, gf.reshape(1, 1, D), y)


def _routing_tables(idx_t, rank_t, counts):
    tiles = (counts + SUB - 1) // SUB
    tile_end = jnp.cumsum(tiles)
    tile_start = tile_end - tiles
    nt_used = tile_end[-1:]
    experts = jnp.arange(NE, dtype=jnp.int32)
    base = jnp.sum(jnp.where(idx_t[..., None] == experts, tile_start * SUB, 0), axis=-1)
    slot = base + rank_t
    ngrp = (tiles + GSUB - 1) // GSUB
    grp_end = jnp.cumsum(ngrp)
    grp_start = grp_end - ngrp
    n_groups = grp_end[-1]
    gid = jnp.arange(G_MAX, dtype=jnp.int32)
    gclamp = jnp.minimum(gid, jnp.maximum(n_groups - 1, 0))
    g_expert = jnp.sum(gclamp[:, None] >= grp_end[None, :], axis=-1).astype(jnp.int32)
    pick = lambda v: jnp.sum(jnp.where(g_expert[:, None] == experts[None, :], v[None, :], 0), axis=-1)
    local = gclamp - pick(grp_start)
    g_start = pick(tile_start) + local * GSUB
    g_nsub = jnp.where(gid < n_groups, jnp.minimum(GSUB, pick(tiles) - local * GSUB), 0)
    pad_lo = (tile_start * SUB + counts).astype(jnp.int32)
    pad_hi = (tile_end * SUB).astype(jnp.int32)
    return (slot.reshape(-1).astype(jnp.int32), nt_used.astype(jnp.int32), pad_lo, pad_hi,
            g_expert, g_start.astype(jnp.int32), g_nsub.astype(jnp.int32))


def kernel(x_prompt, x_sample, c_prompt, c_sample, cache_k, cache_v, state_conv, w_ada, b_ada, norm1_g,
           w_in, b_in, conv_dw_w, conv_dw_b, conv_ln_g, conv_ln_b, w_conv_out, b_conv_out, sinks,
           w_attn_out, b_attn_out, w_out, norm2_g, w_router, b_router, w_mlp1, b_mlp1, w_mlp2, b_mlp2,
           norm_f_g):
    mods = _modulations(jnp.concatenate([c_prompt, c_sample], axis=0), w_ada, b_ada)
    mp = [mods[:BP, k * D:(k + 1) * D].reshape(BP, 1, D) for k in range(6)]
    ms = [mods[BP:, k * D:(k + 1) * D].reshape(BS, 1, D) for k in range(6)]

    h1 = _norm1(x_prompt, x_sample, norm1_g, mp, ms)
    w_in_b = w_in.astype(BF16)
    b_in2 = b_in.reshape(1, IN_W)
    u_all = _in_proj(h1, w_in_b, b_in2, OFF_GLU_A, OFF_GLU_B, D, "glu", F32)
    q_all = _in_proj(h1, w_in_b, b_in2, OFF_Q, None, D, "plain", BF16, HD ** -0.5)
    kv_all = _in_proj(h1, w_in_b, b_in2, OFF_KV, None, 2 * KVW, "plain", F32)
    gates = _in_proj(h1, w_in_b, b_in2, OFF_GC, None, 2 * D, "sigmoid", F32)

    zs = _conv_prompt(u_all, conv_dw_w, conv_dw_b, conv_ln_g, conv_ln_b)
    u_s_t = u_all[TP:].reshape(BS, SS, D).transpose(1, 0, 2)
    zs_s_t, conv_s_t = _conv_sample(state_conv.transpose(1, 0, 2), u_s_t, conv_dw_w, conv_dw_b,
                                    conv_ln_g, conv_ln_b)
    zs = lax.dynamic_update_slice(zs, zs_s_t.transpose(1, 0, 2).reshape(TS, D), (TP, 0))
    conv_s = conv_s_t.transpose(1, 0, 2)
    conv_p = jnp.stack([u_all[(b + 1) * SP - (CW - 1):(b + 1) * SP] for b in range(BP)])

    attn = _attn_prompt(sinks, q_all, kv_all)
    q_g = (q_all[TP:].reshape(BS, SS, NKV, GRP, HD).transpose(0, 2, 3, 1, 4)
           .reshape(BS, NKV, QG, HD))
    kv_s = kv_all[TP:].reshape(BS, SS, 2 * KVW)
    attn_s = _attn_sample(sinks, q_g, kv_s, cache_k.transpose(0, 2, 3, 1), cache_v.transpose(0, 2, 3, 1))
    attn_s = (attn_s.reshape(BS, NKV, GRP, SS, HD).transpose(0, 3, 1, 2, 4).reshape(TS, D))
    attn = lax.dynamic_update_slice(attn, attn_s, (TP, 0))
    kv_tail = jnp.stack([kv_all[(b + 1) * SP - WIN:(b + 1) * SP] for b in range(BP)])
    k_p = kv_tail[:, :, :KVW].reshape(BP, WIN, NKV, HD)
    v_p = kv_tail[:, :, KVW:].reshape(BP, WIN, NKV, HD)
    k_s = jnp.concatenate([cache_k[:, SS:], kv_s[:, :, :KVW].reshape(BS, SS, NKV, HD)], axis=1)
    v_s = jnp.concatenate([cache_v[:, SS:], kv_s[:, :, KVW:].reshape(BS, SS, NKV, HD)], axis=1)

    merged = _merge(zs, attn, w_conv_out.astype(BF16), w_attn_out.astype(BF16), b_conv_out, b_attn_out,
                    gates)
    x1, h2, logits = _out_proj(merged, x_prompt, x_sample, w_out.astype(BF16), norm2_g, mp, ms,
                               w_router, b_router)

    idx_t, gate_t, rank_t, cnt = _router(logits)
    counts = cnt[:, 0].astype(jnp.int32)
    slots, nt_used, pad_lo, pad_hi, g_expert, g_start, g_nsub = _routing_tables(idx_t, rank_t, counts)

    xs = _dispatch(slots, nt_used, pad_lo, pad_hi, h2)
    y = _moe(g_expert, g_start, g_nsub, nt_used, w_mlp1, b_mlp1, w_mlp2, b_mlp2, xs)

    gates_tok = gate_t.T
    y_prompt = _combine(True, slots, x1, gates_tok, mp[5], norm_f_g, y)
    y_sample = _combine(False, slots, x1, gates_tok, ms[5], norm_f_g, y)
    return (y_prompt, y_sample, k_p, v_p, conv_p, k_s, v_s, conv_s)
```

```python
import functools

import jax
import jax.numpy as jnp
from jax import lax
from jax.experimental import pallas as pl
from jax.experimental.pallas import tpu as pltpu

F32 = jnp.float32
BF16 = jnp.bfloat16

D = 2048
BP, SP = 4, 2048
BS, SS = 128, 8
TP = BP * SP
TS = BS * SS
T = TP + TS
HD = 64
NH = 32
NKV = 4
GRP = NH // NKV
WIN = 128
KVW = NKV * HD
CW = 31
NE = 32
TOPK = 4
DFF = 2048
ALPHA = 1.702
LIMIT = 7.0
EPS = 1e-5
OFF_GLU_A, OFF_GLU_B, OFF_Q, OFF_KV, OFF_GC = 0, D, 2 * D, 3 * D, 3 * D + 2 * KVW
IN_W = 5 * D + 2 * KVW

VMEM_LIMIT = 56 * 1024 * 1024

SUB = 256
GSUB = 8
NT_MAX = (T * TOPK) // SUB + NE
P_MAX = NT_MAX * SUB
G_MAX = -(-((T * TOPK) // SUB) // GSUB) + NE
TF = 512
NC = DFF // TF
MXU_N = 256


def _cparams(sem):
    return pltpu.CompilerParams(dimension_semantics=sem, vmem_limit_bytes=VMEM_LIMIT)


def _sigmoid(x):
    return 1.0 / (1.0 + jnp.exp(-x))


def _norm_mod(x, g, shift, scale):
    ms = jnp.mean(x * x, axis=-1, keepdims=True)
    return (x * lax.rsqrt(ms + EPS) * g) * (1.0 + scale) + shift


class _Tiles:
    def __init__(self, tm):
        self.tm = tm
        self.per_b = SP // tm
        self.n_prompt = TP // tm
        self.n = T // tm
        self.sb = tm // SS

    def prompt_x(self, i):
        ip = jnp.minimum(i, self.n_prompt - 1)
        return (ip // self.per_b, ip % self.per_b, 0)

    def prompt_mod(self, i):
        return (jnp.minimum(i, self.n_prompt - 1) // self.per_b, 0, 0)

    def sample_x(self, i):
        return (jnp.maximum(i - self.n_prompt, 0), 0, 0)

    def specs(self, tail=()):
        wrap = lambda f: (lambda i, *_: f(i))
        xp = pl.BlockSpec((1, self.tm, D), wrap(self.prompt_x))
        xs = pl.BlockSpec((self.sb, SS, D), wrap(self.sample_x))
        mp = pl.BlockSpec((1, 1, D), wrap(self.prompt_mod))
        ms = pl.BlockSpec((self.sb, 1, D), wrap(self.sample_x))
        return xp, xs, mp, ms


def _mod_kernel(c_ref, w_ref, b_ref, o_ref):
    c = c_ref[...]
    sc = (c * _sigmoid(c)).astype(BF16)
    o_ref[...] = jnp.dot(sc, w_ref[...].astype(BF16), preferred_element_type=F32) + b_ref[...]


def _modulations(c_all, w_ada, b_ada):
    nb = c_all.shape[0]
    tn = 1024
    return pl.pallas_call(
        _mod_kernel,
        out_shape=jax.ShapeDtypeStruct((nb, 6 * D), F32),
        grid=(6 * D // tn,),
        in_specs=[pl.BlockSpec((nb, D), lambda j: (0, 0)),
                  pl.BlockSpec((D, tn), lambda j: (0, j)),
                  pl.BlockSpec((1, tn), lambda j: (0, j))],
        out_specs=pl.BlockSpec((nb, tn), lambda j: (0, j)),
        compiler_params=_cparams(("parallel",)),
        name="modulations",
    )(c_all, w_ada, b_ada.reshape(1, -1))


def _norm1_kernel(xp_ref, xs_ref, g_ref, shp_ref, scp_ref, shs_ref, scs_ref, o_ref, *, n_prompt):
    i = pl.program_id(0)

    @pl.when(i < n_prompt)
    def _():
        h = _norm_mod(xp_ref[...], g_ref[...], shp_ref[...], scp_ref[...])
        o_ref[...] = h.reshape(o_ref.shape).astype(BF16)

    @pl.when(i >= n_prompt)
    def _():
        h = _norm_mod(xs_ref[...], g_ref[...], shs_ref[...], scs_ref[...])
        o_ref[...] = h.reshape(o_ref.shape).astype(BF16)


def _norm1(x_prompt, x_sample, g, mp, ms):
    tl = _Tiles(256)
    xp, xs, mps, mss = tl.specs()
    return pl.pallas_call(
        functools.partial(_norm1_kernel, n_prompt=tl.n_prompt),
        out_shape=jax.ShapeDtypeStruct((T, D), BF16),
        grid=(tl.n,),
        in_specs=[xp, xs, pl.BlockSpec((1, 1, D), lambda i: (0, 0, 0)), mps, mps, mss, mss],
        out_specs=pl.BlockSpec((tl.tm, D), lambda i: (i, 0)),
        compiler_params=_cparams(("parallel",)),
        name="norm1_adaln",
    )(x_prompt, x_sample, g.reshape(1, 1, D), mp[0], mp[1], ms[0], ms[1])


def _in_proj_kernel(*refs, mode, out_scale):
    if mode == "glu":
        h_ref, wa_ref, wb_ref, ba_ref, bb_ref, o_ref = refs
    else:
        h_ref, wa_ref, ba_ref, o_ref = refs
    h = h_ref[...]
    a = jnp.dot(h, wa_ref[...].astype(BF16), preferred_element_type=F32) + ba_ref[...]
    if mode == "glu":
        b = jnp.dot(h, wb_ref[...].astype(BF16), preferred_element_type=F32) + bb_ref[...]
        a = a * _sigmoid(b)
    elif mode == "sigmoid":
        a = _sigmoid(a)
    elif out_scale != 1.0:
        a = a * out_scale
    o_ref[...] = a.astype(o_ref.dtype)


def _in_proj(h, w_b, b_in2, col_a, col_b, n_cols, mode, out_dtype, out_scale=1.0):
    tm, tn = 1536, 512
    ca, cb = col_a // tn, (col_b // tn if col_b is not None else 0)
    in_specs = [pl.BlockSpec((tm, D), lambda i, j: (i, 0)),
                pl.BlockSpec((D, tn), lambda i, j: (0, ca + j))]
    args = [h, w_b]
    if mode == "glu":
        in_specs.append(pl.BlockSpec((D, tn), lambda i, j: (0, cb + j)))
        args.append(w_b)
    in_specs.append(pl.BlockSpec((1, tn), lambda i, j: (0, ca + j)))
    args.append(b_in2)
    if mode == "glu":
        in_specs.append(pl.BlockSpec((1, tn), lambda i, j: (0, cb + j)))
        args.append(b_in2)
    return pl.pallas_call(
        functools.partial(_in_proj_kernel, mode=mode, out_scale=out_scale),
        out_shape=jax.ShapeDtypeStruct((T, n_cols), out_dtype),
        grid=(T // tm, n_cols // tn),
        in_specs=in_specs,
        out_specs=pl.BlockSpec((tm, tn), lambda i, j: (i, j)),
        compiler_params=_cparams(("parallel", "arbitrary")),
        name="in_proj_" + mode,
    )(*args)


CONV_TL = 128
HALO = 32
CONV_STEPS = TP // CONV_TL


def _ln_swish(z, g, b):
    mu = jnp.mean(z, axis=-1, keepdims=True)
    d = z - mu
    var = jnp.mean(d * d, axis=-1, keepdims=True)
    zn = d * lax.rsqrt(var + EPS) * g + b
    return zn * _sigmoid(zn)


def _conv_prompt_kernel(u_ref, halo_ref, w_ref, dwb_ref, g_ref, b_ref, o_ref, win, z_scr):
    i = pl.program_id(0)

    @pl.when(i < CONV_STEPS)
    def _():
        first = (i % (SP // CONV_TL)) == 0
        base = HALO - (CW - 1)
        for s in range(D // 128):
            cs = slice(s * 128, (s + 1) * 128)
            win[s, 0:HALO, :] = jnp.where(first, 0.0, halo_ref[:, cs])
            win[s, HALO:HALO + CONV_TL, :] = u_ref[:, cs]
        for s in range(D // 128):
            cs = slice(s * 128, (s + 1) * 128)
            acc = None
            for rho in range(8):
                offs = [base + j for j in range(CW) if (base + j) % 8 == rho]
                seg = win[s, offs[0]:offs[-1] + CONV_TL, :]
                for o in offs:
                    j = o - base
                    term = seg[o - offs[0]:o - offs[0] + CONV_TL] * w_ref[j:j + 1, cs]
                    acc = term if acc is None else acc + term
            z_scr[:, cs] = acc + dwb_ref[:, cs]
        o_ref[...] = _ln_swish(z_scr[...], g_ref[...], b_ref[...]).astype(BF16)

    @pl.when(i >= CONV_STEPS)
    def _():
        o_ref[...] = jnp.zeros_like(o_ref)


def _conv_prompt(u_all, dw_w, dw_b, ln_g, ln_b):
    hb = CONV_TL // HALO
    last = CONV_STEPS - 1
    const = lambda i: (0, 0)
    return pl.pallas_call(
        _conv_prompt_kernel,
        out_shape=jax.ShapeDtypeStruct((T, D), BF16),
        grid=(T // CONV_TL,),
        in_specs=[pl.BlockSpec((CONV_TL, D), lambda i: (jnp.minimum(i, last), 0)),
                  pl.BlockSpec((HALO, D), lambda i: (jnp.maximum(jnp.minimum(i, last) * hb - 1, 0), 0)),
                  pl.BlockSpec((CW, D), const),
                  pl.BlockSpec((1, D), const),
                  pl.BlockSpec((1, D), const),
                  pl.BlockSpec((1, D), const)],
        out_specs=pl.BlockSpec((CONV_TL, D), lambda i: (i, 0)),
        scratch_shapes=[pltpu.VMEM((D // 128, HALO + CONV_TL, 128), F32), pltpu.VMEM((CONV_TL, D), F32)],
        compiler_params=_cparams(("parallel",)),
        name="conv_prompt",
    )(u_all, u_all, dw_w, dw_b.reshape(1, D), ln_g.reshape(1, D), ln_b.reshape(1, D))


CONV_BB = 16


def _conv_sample_kernel(st_ref, u_ref, w_ref, dwb_ref, g_ref, b_ref, o_ref, ns_ref, win, z_scr):
    win[0:CW - 1] = st_ref[...]
    win[CW - 1:CW - 1 + SS] = u_ref[...]
    ns_ref[0:CW - 1 - SS] = st_ref[SS:CW - 1]
    ns_ref[CW - 1 - SS:CW - 1] = u_ref[...]
    for s in range(D // 128):
        cs = slice(s * 128, (s + 1) * 128)
        acc = None
        for j in range(CW):
            term = win[j:j + SS, :, cs] * w_ref[j:j + 1, cs][None]
            acc = term if acc is None else acc + term
        z_scr[:, :, cs] = acc + dwb_ref[:, cs][None]
    o_ref[...] = _ln_swish(z_scr[...], g_ref[...][None], b_ref[...][None]).astype(BF16)


def _conv_sample(state_t, u_t, dw_w, dw_b, ln_g, ln_b):
    const = lambda i: (0, 0)
    batch = lambda i: (0, i, 0)
    return pl.pallas_call(
        _conv_sample_kernel,
        out_shape=(jax.ShapeDtypeStruct((SS, BS, D), BF16), jax.ShapeDtypeStruct((CW - 1, BS, D), F32)),
        grid=(BS // CONV_BB,),
        in_specs=[pl.BlockSpec((CW - 1, CONV_BB, D), batch),
                  pl.BlockSpec((SS, CONV_BB, D), batch),
                  pl.BlockSpec((CW, D), const),
                  pl.BlockSpec((1, D), const),
                  pl.BlockSpec((1, D), const),
                  pl.BlockSpec((1, D), const)],
        out_specs=(pl.BlockSpec((SS, CONV_BB, D), batch),
                   pl.BlockSpec((CW - 1, CONV_BB, D), batch)),
        scratch_shapes=[pltpu.VMEM((CW - 1 + SS, CONV_BB, D), F32), pltpu.VMEM((SS, CONV_BB, D), F32)],
        compiler_params=_cparams(("parallel",)),
        name="conv_sample",
    )(state_t, u_t, dw_w, dw_b.reshape(1, D), ln_g.reshape(1, D), ln_b.reshape(1, D))


ATT_STEPS = TP // WIN


def _attn_prompt_kernel(sink_ref, q_ref, kvc_ref, kvp_ref, o_ref):
    i = pl.program_id(0)

    @pl.when(i < ATT_STEPS)
    def _():
        n = i % (SP // WIN)
        row = lax.broadcasted_iota(jnp.int32, (WIN, 2 * WIN), 0)
        col = lax.broadcasted_iota(jnp.int32, (WIN, 2 * WIN), 1)
        diff = row + WIN - col
        first_key = jnp.where(n > 0, 0, WIN)
        mask = (diff >= 0) & (diff < WIN) & (col >= first_key)
        kvc = kvc_ref[...].astype(BF16)
        kvp = kvp_ref[...].astype(BF16)
        outs = []
        for kh in range(NKV):
            k2 = jnp.concatenate([kvp[:, kh * HD:(kh + 1) * HD], kvc[:, kh * HD:(kh + 1) * HD]], axis=0)
            v2 = jnp.concatenate([kvp[:, KVW + kh * HD:KVW + (kh + 1) * HD],
                                  kvc[:, KVW + kh * HD:KVW + (kh + 1) * HD]], axis=0)
            for g in range(GRP):
                h = kh * GRP + g
                sink = sink_ref[h]
                qh = q_ref[:, h * HD:(h + 1) * HD]
                s = lax.dot_general(qh, k2, (((1,), (1,)), ((), ())), preferred_element_type=F32)
                s = jnp.where(mask, s, -jnp.inf)
                m = jnp.maximum(jnp.max(s, axis=-1, keepdims=True), sink)
                p = jnp.exp(s - m)
                den = jnp.sum(p, axis=-1, keepdims=True) + jnp.exp(sink - m)
                o = jnp.dot(p.astype(BF16), v2, preferred_element_type=F32)
                outs.append(o / den)
        o_ref[...] = jnp.concatenate(outs, axis=-1).astype(BF16)

    @pl.when(i >= ATT_STEPS)
    def _():
        o_ref[...] = jnp.zeros_like(o_ref)


def _attn_prompt(sinks, q_all, kv_all):
    last = ATT_STEPS - 1
    nblk = SP // WIN

    def prev_block(i):
        ic = jnp.minimum(i, last)
        return (jnp.where(ic % nblk == 0, ic, ic - 1), 0)

    return pl.pallas_call(
        _attn_prompt_kernel,
        out_shape=jax.ShapeDtypeStruct((T, D), BF16),
        grid=(T // WIN,),
        in_specs=[pl.BlockSpec(memory_space=pltpu.SMEM),
                  pl.BlockSpec((WIN, D), lambda i: (jnp.minimum(i, last), 0)),
                  pl.BlockSpec((WIN, 2 * KVW), lambda i: (jnp.minimum(i, last), 0)),
                  pl.BlockSpec((WIN, 2 * KVW), prev_block)],
        out_specs=pl.BlockSpec((WIN, D), lambda i: (i, 0)),
        compiler_params=_cparams(("parallel",)),
        name="attn_prompt",
    )(sinks, q_all, kv_all, kv_all)


ATT_BB = 8
QG = SS * GRP


def _attn_sample_kernel(sink_ref, q_ref, kvn_ref, ck_ref, cv_ref, o_ref):
    rr = lax.broadcasted_iota(jnp.int32, (QG, WIN), 0)
    qi_c = rr % SS
    key_c = lax.broadcasted_iota(jnp.int32, (QG, WIN), 1)
    mask_c = (key_c > qi_c)[None]
    rr_n = lax.broadcasted_iota(jnp.int32, (QG, SS), 0)
    key_n = lax.broadcasted_iota(jnp.int32, (QG, SS), 1)
    mask_n = (key_n <= rr_n % SS)[None]
    grow = lax.broadcasted_iota(jnp.int32, (QG, 1), 0) // SS
    for kh in range(NKV):
        sink = jnp.zeros((QG, 1), F32)
        for g in range(GRP):
            sink = jnp.where(grow == g, sink_ref[kh * GRP + g], sink)
        sink = sink[None]
        q = q_ref[:, kh]
        kc = ck_ref[:, kh].astype(BF16)
        vc = cv_ref[:, kh].astype(BF16)
        kn = kvn_ref[:, :, kh * HD:(kh + 1) * HD].astype(BF16)
        vn = kvn_ref[:, :, KVW + kh * HD:KVW + (kh + 1) * HD].astype(BF16)
        s1 = jnp.einsum("bqd,bds->bqs", q, kc, preferred_element_type=F32)
        s2 = jnp.einsum("bqd,bsd->bqs", q, kn, preferred_element_type=F32)
        s1 = jnp.where(mask_c, s1, -jnp.inf)
        s2 = jnp.where(mask_n, s2, -jnp.inf)
        m = jnp.maximum(jnp.maximum(jnp.max(s1, axis=-1, keepdims=True),
                                    jnp.max(s2, axis=-1, keepdims=True)), sink)
        p1 = jnp.exp(s1 - m)
        p2 = jnp.exp(s2 - m)
        den = (jnp.sum(p1, axis=-1, keepdims=True) + jnp.sum(p2, axis=-1, keepdims=True)
               + jnp.exp(sink - m))
        o = (jnp.einsum("bqs,bds->bqd", p1.astype(BF16), vc, preferred_element_type=F32)
             + jnp.einsum("bqs,bsd->bqd", p2.astype(BF16), vn, preferred_element_type=F32))
        o_ref[:, kh] = (o / den).astype(BF16)


def _attn_sample(sinks, q_g, kv_new, ck, cv):
    return pl.pallas_call(
        _attn_sample_kernel,
        out_shape=jax.ShapeDtypeStruct((BS, NKV, QG, HD), BF16),
        grid=(BS // ATT_BB,),
        in_specs=[pl.BlockSpec(memory_space=pltpu.SMEM),
                  pl.BlockSpec((ATT_BB, NKV, QG, HD), lambda i: (i, 0, 0, 0)),
                  pl.BlockSpec((ATT_BB, SS, 2 * KVW), lambda i: (i, 0, 0)),
                  pl.BlockSpec((ATT_BB, NKV, HD, WIN), lambda i: (i, 0, 0, 0)),
                  pl.BlockSpec((ATT_BB, NKV, HD, WIN), lambda i: (i, 0, 0, 0))],
        out_specs=pl.BlockSpec((ATT_BB, NKV, QG, HD), lambda i: (i, 0, 0, 0)),
        compiler_params=_cparams(("parallel",)),
        name="attn_sample",
    )(sinks, q_g, kv_new, ck, cv)


def _merge_kernel(zs_ref, at_ref, wc_ref, wa_ref, bc_ref, ba_ref, gc_ref, ga_ref, o_ref):
    conv = jnp.dot(zs_ref[...], wc_ref[...].astype(BF16), preferred_element_type=F32) + bc_ref[...]
    attn = jnp.dot(at_ref[...], wa_ref[...].astype(BF16), preferred_element_type=F32) + ba_ref[...]
    o_ref[...] = (gc_ref[...] * conv + ga_ref[...] * attn).astype(BF16)


def _merge(zs, attn, w_co, w_ao, b_co, b_ao, gates):
    tm, tn = 1024, 512
    ga_off = D // tn
    return pl.pallas_call(
        _merge_kernel,
        out_shape=jax.ShapeDtypeStruct((T, D), BF16),
        grid=(T // tm, D // tn),
        in_specs=[pl.BlockSpec((tm, D), lambda i, j: (i, 0)),
                  pl.BlockSpec((tm, D), lambda i, j: (i, 0)),
                  pl.BlockSpec((D, tn), lambda i, j: (0, j)),
                  pl.BlockSpec((D, tn), lambda i, j: (0, j)),
                  pl.BlockSpec((1, tn), lambda i, j: (0, j)),
                  pl.BlockSpec((1, tn), lambda i, j: (0, j)),
                  pl.BlockSpec((tm, tn), lambda i, j: (i, j)),
                  pl.BlockSpec((tm, tn), lambda i, j: (i, ga_off + j))],
        out_specs=pl.BlockSpec((tm, tn), lambda i, j: (i, j)),
        compiler_params=_cparams(("parallel", "arbitrary")),
        name="mixer_merge",
    )(zs, attn, w_co, w_ao, b_co.reshape(1, D), b_ao.reshape(1, D), gates, gates)


def _router_logits(h, w, b):
    h_hi = h.astype(BF16)
    h_lo = (h - h_hi.astype(F32)).astype(BF16)
    w_hi = w.astype(BF16)
    w_lo = (w - w_hi.astype(F32)).astype(BF16)
    dn = (((1,), (1,)), ((), ()))
    return (lax.dot_general(w_hi, h_hi, dn, preferred_element_type=F32)
            + lax.dot_general(w_hi, h_lo, dn, preferred_element_type=F32)
            + lax.dot_general(w_lo, h_hi, dn, preferred_element_type=F32)) + b


def _out_proj_kernel(mg_ref, xp_ref, xs_ref, w_ref, g_ref, m2p_ref, m3p_ref, m4p_ref,
                     m2s_ref, m3s_ref, m4s_ref, wr_ref, br_ref, x1_ref, h2_ref, lg_ref, *, n_prompt):
    i = pl.program_id(0)
    y = jnp.dot(mg_ref[...], w_ref[...], preferred_element_type=F32)

    def finish(x_ref, m2_ref, m3_ref, m4_ref):
        x1 = x_ref[...] + m2_ref[...] * y.reshape(x_ref.shape)
        x1_ref[...] = x1.reshape(x1_ref.shape)
        h2 = _norm_mod(x1, g_ref[...], m3_ref[...], m4_ref[...]).reshape(x1_ref.shape)
        h2_ref[...] = h2
        lg_ref[...] = _router_logits(h2, wr_ref[...], br_ref[...])

    @pl.when(i < n_prompt)
    def _():
        finish(xp_ref, m2p_ref, m3p_ref, m4p_ref)

    @pl.when(i >= n_prompt)
    def _():
        finish(xs_ref, m2s_ref, m3s_ref, m4s_ref)


def _out_proj(merged, x_prompt, x_sample, w_out_b, g2, mp, ms, w_router, b_router):
    tl = _Tiles(256)
    xp, xs, mps, mss = tl.specs()
    row = pl.BlockSpec((tl.tm, D), lambda i: (i, 0))
    return pl.pallas_call(
        functools.partial(_out_proj_kernel, n_prompt=tl.n_prompt),
        out_shape=(jax.ShapeDtypeStruct((T, D), F32),
                   jax.ShapeDtypeStruct((T, D), F32),
                   jax.ShapeDtypeStruct((NE, T), F32)),
        grid=(tl.n,),
        in_specs=[row, xp, xs, pl.BlockSpec((D, D), lambda i: (0, 0)),
                  pl.BlockSpec((1, 1, D), lambda i: (0, 0, 0)), mps, mps, mps, mss, mss, mss,
                  pl.BlockSpec((NE, D), lambda i: (0, 0)),
                  pl.BlockSpec((NE, 1), lambda i: (0, 0))],
        out_specs=(row, row, pl.BlockSpec((NE, tl.tm), lambda i: (0, i))),
        compiler_params=_cparams(("parallel",)),
        name="out_proj_norm2",
    )(merged, x_prompt, x_sample, w_out_b, g2.reshape(1, 1, D), mp[2], mp[3], mp[4], ms[2], ms[3], ms[4],
      w_router.T, b_router.reshape(NE, 1))


ROUTE_TR = 512


def _router_kernel(lg_ref, tri_ref, idx_ref, gate_ref, rank_ref, cnt_ref, carry):
    @pl.when(pl.program_id(0) == 0)
    def _():
        carry[...] = jnp.zeros_like(carry)

    eid = lax.broadcasted_iota(jnp.int32, (NE, ROUTE_TR), 0).astype(F32)
    work = lg_ref[...]
    vals, ids = [], []
    onehot = jnp.zeros((NE, ROUTE_TR), F32)
    for _ in range(TOPK):
        m = jnp.max(work, axis=0, keepdims=True)
        sel = jnp.min(jnp.where(work == m, eid, float(NE)), axis=0, keepdims=True)
        hit = eid == sel
        work = jnp.where(hit, -jnp.inf, work)
        onehot = jnp.where(hit, 1.0, onehot)
        vals.append(m)
        ids.append(sel)
    es = [jnp.exp(v - vals[0]) for v in vals]
    den = es[0] + es[1] + es[2] + es[3]
    before = jnp.dot(onehot.astype(BF16), tri_ref[...], preferred_element_type=F32) + carry[:, 0:1]
    for k in range(TOPK):
        idx_ref[k:k + 1, :] = ids[k].astype(jnp.int32)
        gate_ref[k:k + 1, :] = es[k] / den
        rank_ref[k:k + 1, :] = jnp.sum(jnp.where(eid == ids[k], before, 0.0), axis=0,
                                       keepdims=True).astype(jnp.int32)
    carry[...] = carry[...] + jnp.sum(onehot, axis=1, keepdims=True)
    cnt_ref[...] = carry[...]


def _router(logits):
    tri = jnp.triu(jnp.ones((ROUTE_TR, ROUTE_TR), F32), 1).astype(BF16)
    row = lambda i: (0, i)
    return pl.pallas_call(
        _router_kernel,
        out_shape=(jax.ShapeDtypeStruct((TOPK, T), jnp.int32),
                   jax.ShapeDtypeStruct((TOPK, T), F32),
                   jax.ShapeDtypeStruct((TOPK, T), jnp.int32),
                   jax.ShapeDtypeStruct((NE, 128), F32)),
        grid=(T // ROUTE_TR,),
        in_specs=[pl.BlockSpec((NE, ROUTE_TR), row),
                  pl.BlockSpec((ROUTE_TR, ROUTE_TR), lambda i: (0, 0))],
        out_specs=(pl.BlockSpec((TOPK, ROUTE_TR), row),
                   pl.BlockSpec((TOPK, ROUTE_TR), row),
                   pl.BlockSpec((TOPK, ROUTE_TR), row),
                   pl.BlockSpec((NE, 128), lambda i: (0, 0))),
        scratch_shapes=[pltpu.VMEM((NE, 128), F32)],
        compiler_params=_cparams(("arbitrary",)),
        name="router_top4",
    )(logits, tri)


DISPATCH_ROWS = 2 * SUB
DISPATCH_STEPS = NT_MAX // 2


def _dispatch_kernel(tok_ref, nt_ref, h_hbm, o_ref, buf, sem):
    i = pl.program_id(0)
    nt = (nt_ref[0] + 1) // 2

    def issue(step, slot):
        for r in range(DISPATCH_ROWS):
            pltpu.make_async_copy(h_hbm.at[pl.ds(tok_ref[step * DISPATCH_ROWS + r], 1)],
                                  buf.at[slot, pl.ds(r, 1)], sem.at[slot]).start()

    @pl.when(i == 0)
    def _():
        issue(0, 0)

    @pl.when(i + 1 < nt)
    def _():
        issue(i + 1, (i + 1) % 2)

    @pl.when(i < nt)
    def _():
        slot = i % 2
        pltpu.make_async_copy(h_hbm.at[pl.ds(0, DISPATCH_ROWS)], buf.at[slot], sem.at[slot]).wait()
        o_ref[...] = buf[slot].astype(BF16)

    @pl.when(i >= nt)
    def _():
        o_ref[...] = jnp.zeros_like(o_ref)


def _dispatch(tok_of_slot, nt_used, h2):
    return pl.pallas_call(
        _dispatch_kernel,
        out_shape=jax.ShapeDtypeStruct((P_MAX, D), BF16),
        grid_spec=pltpu.PrefetchScalarGridSpec(
            num_scalar_prefetch=2,
            grid=(DISPATCH_STEPS,),
            in_specs=[pl.BlockSpec(memory_space=pl.ANY)],
            out_specs=pl.BlockSpec((DISPATCH_ROWS, D), lambda i, *_: (i, 0)),
            scratch_shapes=[pltpu.VMEM((2, DISPATCH_ROWS, D), F32),
                            pltpu.SemaphoreType.DMA((2,))]),
        compiler_params=_cparams(("arbitrary",)),
        name="moe_dispatch",
    )(tok_of_slot, nt_used, h2)


def _moe_kernel(ge_ref, gs_ref, gn_ref, nt_ref, w1_ref, b1_ref, w2_ref, b2_ref, sel_ref, xs_hbm, y_hbm,
                x_buf, acc, h_scr, sem_in, sem_out):
    del ge_ref
    g = pl.program_id(0)
    c = pl.program_id(1)
    nsub = gn_ref[g]
    start = gs_ref[g]

    def rows_of(r):
        return pl.ds(pl.multiple_of(r * SUB, SUB), SUB)

    def x_copy(r):
        return pltpu.make_async_copy(xs_hbm.at[pl.ds((start + r) * SUB, SUB)], x_buf.at[rows_of(r)],
                                     sem_in.at[r])

    def y_copy(r):
        return pltpu.make_async_copy(acc.at[rows_of(r)], y_hbm.at[pl.ds((start + r) * SUB, SUB)], sem_out)

    def for_subtiles(fn, n):
        def body(r, carry):
            fn(r)
            return carry
        lax.fori_loop(0, n, body, 0)

    def up_proj(r):
        return (jnp.dot(x_buf[rows_of(r), :], w1_ref[0].astype(BF16), preferred_element_type=F32)
                + b1_ref[0])

    def down_proj(r, h):
        hn = jnp.concatenate(
            [pltpu.roll(h[:, k * 128:(k + 1) * 128], 127, 1) for k in range(2 * TF // 128)], axis=1)
        glu = jnp.minimum(h, LIMIT)
        lin = jnp.clip(hn, -LIMIT, LIMIT)
        act = (glu * _sigmoid(ALPHA * glu) * (lin + 1.0)).astype(BF16)
        sel = sel_ref[...]
        act = jnp.concatenate(
            [jnp.dot(act[:, k * MXU_N:(k + 1) * MXU_N], sel, preferred_element_type=F32)
             for k in range(2 * TF // MXU_N)], axis=1).astype(BF16)
        rows = rows_of(r)
        acc[rows, :] = acc[rows, :] + jnp.dot(act, w2_ref[0].astype(BF16), preferred_element_type=F32)

    @pl.when(nsub > 0)
    def _():
        @pl.when(c == 0)
        def _():
            for_subtiles(lambda r: x_copy(r).start(), nsub)

            def init(r):
                acc[rows_of(r), :] = jnp.broadcast_to(b2_ref[0], (SUB, D))
            for_subtiles(init, nsub)
            x_copy(0).wait()

        h_scr[0] = up_proj(0)

        def step(r):
            @pl.when(c == 0)
            def _():
                x_copy(r + 1).wait()

            h = h_scr[r % 2]
            h_scr[(r + 1) % 2] = up_proj(r + 1)
            down_proj(r, h)

            @pl.when(c == NC - 1)
            def _():
                y_copy(r).start()
        for_subtiles(step, nsub - 1)
        down_proj(nsub - 1, h_scr[(nsub - 1) % 2])

        @pl.when(c == NC - 1)
        def _():
            y_copy(nsub - 1).start()
            for_subtiles(lambda r: y_copy(r).wait(), nsub)

    @pl.when((g == G_MAX - 1) & (c == NC - 1))
    def _():
        acc[0:SUB, :] = jnp.zeros((SUB, D), F32)

        def fill(tile, carry):
            cp = pltpu.make_async_copy(acc.at[pl.ds(0, SUB)], y_hbm.at[pl.ds(tile * SUB, SUB)], sem_out)
            cp.start()
            cp.wait()
            return carry
        lax.fori_loop(nt_ref[0], NT_MAX, fill, 0)


def _moe(g_expert, g_start, g_nsub, nt_used, w1, b1, w2, b2, xs):
    sel = (jnp.arange(MXU_N)[:, None] == 2 * jnp.arange(MXU_N // 2)[None, :]).astype(BF16)

    def cidx(g, c, gn):
        return jnp.where(gn[g] > 0, c, NC - 1)

    return pl.pallas_call(
        _moe_kernel,
        out_shape=jax.ShapeDtypeStruct((P_MAX, D), F32),
        grid_spec=pltpu.PrefetchScalarGridSpec(
            num_scalar_prefetch=4,
            grid=(G_MAX, NC),
            in_specs=[pl.BlockSpec((1, D, 2 * TF), lambda g, c, ge, gs, gn, nt: (ge[g], 0, cidx(g, c, gn))),
                      pl.BlockSpec((1, 1, 2 * TF), lambda g, c, ge, gs, gn, nt: (ge[g], 0, cidx(g, c, gn))),
                      pl.BlockSpec((1, TF, D), lambda g, c, ge, gs, gn, nt: (ge[g], cidx(g, c, gn), 0)),
                      pl.BlockSpec((1, 1, D), lambda g, c, ge, gs, gn, nt: (ge[g], 0, 0)),
                      pl.BlockSpec((MXU_N, MXU_N // 2), lambda g, c, ge, gs, gn, nt: (0, 0)),
                      pl.BlockSpec(memory_space=pl.ANY)],
            out_specs=pl.BlockSpec(memory_space=pl.ANY),
            scratch_shapes=[pltpu.VMEM((GSUB * SUB, D), BF16),
                            pltpu.VMEM((GSUB * SUB, D), F32),
                            pltpu.VMEM((2, SUB, 2 * TF), F32),
                            pltpu.SemaphoreType.DMA((GSUB,)),
                            pltpu.SemaphoreType.DMA]),
        compiler_params=_cparams(("arbitrary", "arbitrary")),
        name="moe_experts",
    )(g_expert, g_start, g_nsub, nt_used, w1, b1.reshape(NE, 1, 2 * DFF), w2, b2.reshape(NE, 1, D), sel, xs)


COMB_TM = 128


def _combine_kernel(slot_ref, x1_ref, gate_ref, m5_ref, g_ref, y_hbm, o_ref, buf, sem, *, tok0):
    i = pl.program_id(0)
    n = pl.num_programs(0)

    def issue(tile, slot):
        for k in range(TOPK):
            for r in range(COMB_TM):
                s = slot_ref[k * T + tok0 + tile * COMB_TM + r]
                pltpu.make_async_copy(y_hbm.at[pl.ds(s, 1)], buf.at[slot, k, pl.ds(r, 1)],
                                      sem.at[slot]).start()

    @pl.when(i == 0)
    def _():
        issue(0, 0)

    @pl.when(i + 1 < n)
    def _():
        issue(i + 1, (i + 1) % 2)

    slot = i % 2
    for k in range(TOPK):
        pltpu.make_async_copy(y_hbm.at[pl.ds(0, COMB_TM)], buf.at[slot, k], sem.at[slot]).wait()
    gates = gate_ref[...]
    f = None
    for k in range(TOPK):
        term = gates[:, k:k + 1] * buf[slot, k]
        f = term if f is None else f + term
    x2 = x1_ref[...].reshape(o_ref.shape) + m5_ref[...] * f.reshape(o_ref.shape)
    ms = jnp.mean(x2 * x2, axis=-1, keepdims=True)
    o_ref[...] = x2 * lax.rsqrt(ms + EPS) * g_ref[...]


def _combine(prompt, slots, x1, gates_t, m5, gf, y):
    tm = COMB_TM
    if prompt:
        nb, r, steps, rb0, shape3 = 1, tm, TP // tm, 0, (BP, SP, D)
        per_b = SP // tm
        x_index = lambda i, s: (i // per_b, i % per_b, 0)
        m_index = lambda i, s: (i // per_b, 0, 0)
    else:
        nb, r, steps, rb0, shape3 = tm // SS, SS, TS // tm, TP // tm, (BS, SS, D)
        x_index = lambda i, s: (i, 0, 0)
        m_index = x_index
    return pl.pallas_call(
        functools.partial(_combine_kernel, tok0=rb0 * tm),
        out_shape=jax.ShapeDtypeStruct(shape3, F32),
        grid_spec=pltpu.PrefetchScalarGridSpec(
            num_scalar_prefetch=1,
            grid=(steps,),
            in_specs=[pl.BlockSpec((tm, D), lambda i, s: (rb0 + i, 0)),
                      pl.BlockSpec((tm, TOPK), lambda i, s: (rb0 + i, 0)),
                      pl.BlockSpec((nb, 1, D), m_index),
                      pl.BlockSpec((1, 1, D), lambda i, s: (0, 0, 0)),
                      pl.BlockSpec(memory_space=pl.ANY)],
            out_specs=pl.BlockSpec((nb, r, D), x_index),
            scratch_shapes=[pltpu.VMEM((2, TOPK, tm, D), F32), pltpu.SemaphoreType.DMA((2,))]),
        compiler_params=_cparams(("arbitrary",)),
        name="moe_combine_norm",
    )(slots, x1, gates_t, m5, gf.reshape(1, 1, D), y)


def _routing_tables(idx_t, rank_t, counts):
    tiles = (counts + SUB - 1) // SUB
    tile_end = jnp.cumsum(tiles)
    tile_start = tile_end - tiles
    nt_used = tile_end[-1:]
    experts = jnp.arange(NE, dtype=jnp.int32)
    base = jnp.sum(jnp.where(idx_t[..., None] == experts, tile_start * SUB, 0), axis=-1)
    slot = base + rank_t
    tok = jnp.tile(jnp.arange(T, dtype=jnp.int32), TOPK)
    tok_of_slot = jnp.zeros((P_MAX,), jnp.int32).at[slot.reshape(-1)].set(tok)
    ngrp = (tiles + GSUB - 1) // GSUB
    grp_end = jnp.cumsum(ngrp)
    grp_start = grp_end - ngrp
    n_groups = grp_end[-1]
    gid = jnp.arange(G_MAX, dtype=jnp.int32)
    gclamp = jnp.minimum(gid, jnp.maximum(n_groups - 1, 0))
    g_expert = jnp.sum(gclamp[:, None] >= grp_end[None, :], axis=-1).astype(jnp.int32)
    pick = lambda v: jnp.sum(jnp.where(g_expert[:, None] == experts[None, :], v[None, :], 0), axis=-1)
    local = gclamp - pick(grp_start)
    g_start = pick(tile_start) + local * GSUB
    g_nsub = jnp.where(gid < n_groups, jnp.minimum(GSUB, pick(tiles) - local * GSUB), 0)
    return (slot.reshape(-1).astype(jnp.int32), tok_of_slot, nt_used.astype(jnp.int32),
            g_expert, g_start.astype(jnp.int32), g_nsub.astype(jnp.int32))


def kernel(x_prompt, x_sample, c_prompt, c_sample, cache_k, cache_v, state_conv, w_ada, b_ada, norm1_g,
           w_in, b_in, conv_dw_w, conv_dw_b, conv_ln_g, conv_ln_b, w_conv_out, b_conv_out, sinks,
           w_attn_out, b_attn_out, w_out, norm2_g, w_router, b_router, w_mlp1, b_mlp1, w_mlp2, b_mlp2,
           norm_f_g):
    mods = _modulations(jnp.concatenate([c_prompt, c_sample], axis=0), w_ada, b_ada)
    mp = [mods[:BP, k * D:(k + 1) * D].reshape(BP, 1, D) for k in range(6)]
    ms = [mods[BP:, k * D:(k + 1) * D].reshape(BS, 1, D) for k in range(6)]

    h1 = _norm1(x_prompt, x_sample, norm1_g, mp, ms)
    w_in_b = w_in
    b_in2 = b_in.reshape(1, IN_W)
    u_all = _in_proj(h1, w_in_b, b_in2, OFF_GLU_A, OFF_GLU_B, D, "glu", F32)
    q_all = _in_proj(h1, w_in_b, b_in2, OFF_Q, None, D, "plain", BF16, HD ** -0.5)
    kv_all = _in_proj(h1, w_in_b, b_in2, OFF_KV, None, 2 * KVW, "plain", F32)
    gates = _in_proj(h1, w_in_b, b_in2, OFF_GC, None, 2 * D, "sigmoid", F32)

    zs = _conv_prompt(u_all, conv_dw_w, conv_dw_b, conv_ln_g, conv_ln_b)
    u_s_t = u_all[TP:].reshape(BS, SS, D).transpose(1, 0, 2)
    zs_s_t, conv_s_t = _conv_sample(state_conv.transpose(1, 0, 2), u_s_t, conv_dw_w, conv_dw_b,
                                    conv_ln_g, conv_ln_b)
    zs = lax.dynamic_update_slice(zs, zs_s_t.transpose(1, 0, 2).reshape(TS, D), (TP, 0))
    conv_s = conv_s_t.transpose(1, 0, 2)
    conv_p = jnp.stack([u_all[(b + 1) * SP - (CW - 1):(b + 1) * SP] for b in range(BP)])

    attn = _attn_prompt(sinks, q_all, kv_all)
    q_g = (q_all[TP:].reshape(BS, SS, NKV, GRP, HD).transpose(0, 2, 3, 1, 4)
           .reshape(BS, NKV, QG, HD))
    kv_s = kv_all[TP:].reshape(BS, SS, 2 * KVW)
    attn_s = _attn_sample(sinks, q_g, kv_s, cache_k.transpose(0, 2, 3, 1), cache_v.transpose(0, 2, 3, 1))
    attn_s = (attn_s.reshape(BS, NKV, GRP, SS, HD).transpose(0, 3, 1, 2, 4).reshape(TS, D))
    attn = lax.dynamic_update_slice(attn, attn_s, (TP, 0))
    kv_tail = jnp.stack([kv_all[(b + 1) * SP - WIN:(b + 1) * SP] for b in range(BP)])
    k_p = kv_tail[:, :, :KVW].reshape(BP, WIN, NKV, HD)
    v_p = kv_tail[:, :, KVW:].reshape(BP, WIN, NKV, HD)
    k_s = jnp.concatenate([cache_k[:, SS:], kv_s[:, :, :KVW].reshape(BS, SS, NKV, HD)], axis=1)
    v_s = jnp.concatenate([cache_v[:, SS:], kv_s[:, :, KVW:].reshape(BS, SS, NKV, HD)], axis=1)

    merged = _merge(zs, attn, w_conv_out, w_attn_out, b_conv_out, b_attn_out,
                    gates)
    x1, h2, logits = _out_proj(merged, x_prompt, x_sample, w_out.astype(BF16), norm2_g, mp, ms,
                               w_router, b_router)

    idx_t, gate_t, rank_t, cnt = _router(logits)
    counts = cnt[:, 0].astype(jnp.int32)
    slots, tok_of_slot, nt_used, g_expert, g_start, g_nsub = _routing_tables(idx_t, rank_t, counts)

    xs = _dispatch(tok_of_slot, nt_used, h2)
    y = _moe(g_expert, g_start, g_nsub, nt_used, w_mlp1, b_mlp1, w_mlp2, b_mlp2, xs)

    gates_tok = gate_t.T
    y_prompt = _combine(True, slots, x1, gates_tok, mp[5], norm_f_g, y)
    y_sample = _combine(False, slots, x1, gates_tok, ms[5], norm_f_g, y)
    return (y_prompt, y_sample, k_p, v_p, conv_p, k_s, v_s, conv_s)
```

```python
import functools

import jax
import jax.numpy as jnp
from jax import lax
from jax.experimental import pallas as pl
from jax.experimental.pallas import tpu as pltpu

F32 = jnp.float32
BF16 = jnp.bfloat16

D = 2048
BP, SP = 4, 2048
BS, SS = 128, 8
TP = BP * SP
TS = BS * SS
T = TP + TS
HD = 64
NH = 32
NKV = 4
GRP = NH // NKV
WIN = 128
KVW = NKV * HD
CW = 31
NE = 32
TOPK = 4
DFF = 2048
ALPHA = 1.702
LIMIT = 7.0
EPS = 1e-5
OFF_GLU_A, OFF_GLU_B, OFF_Q, OFF_KV, OFF_GC = 0, D, 2 * D, 3 * D, 3 * D + 2 * KVW
IN_W = 5 * D + 2 * KVW

VMEM_LIMIT = 56 * 1024 * 1024

SUB = 256
GSUB = 8
NT_MAX = (T * TOPK) // SUB + NE
P_MAX = NT_MAX * SUB
G_MAX = -(-((T * TOPK) // SUB) // GSUB) + NE
TF = 512
NC = DFF // TF
MXU_N = 256


def _cparams(sem):
    return pltpu.CompilerParams(dimension_semantics=sem, vmem_limit_bytes=VMEM_LIMIT)


def _sigmoid(x):
    return 1.0 / (1.0 + jnp.exp(-x))


def _norm_mod(x, g, shift, scale):
    ms = jnp.mean(x * x, axis=-1, keepdims=True)
    return (x * lax.rsqrt(ms + EPS) * g) * (1.0 + scale) + shift


class _Tiles:
    def __init__(self, tm):
        self.tm = tm
        self.per_b = SP // tm
        self.n_prompt = TP // tm
        self.n = T // tm
        self.sb = tm // SS

    def prompt_x(self, i):
        ip = jnp.minimum(i, self.n_prompt - 1)
        return (ip // self.per_b, ip % self.per_b, 0)

    def prompt_mod(self, i):
        return (jnp.minimum(i, self.n_prompt - 1) // self.per_b, 0, 0)

    def sample_x(self, i):
        return (jnp.maximum(i - self.n_prompt, 0), 0, 0)

    def specs(self, tail=()):
        wrap = lambda f: (lambda i, *_: f(i))
        xp = pl.BlockSpec((1, self.tm, D), wrap(self.prompt_x))
        xs = pl.BlockSpec((self.sb, SS, D), wrap(self.sample_x))
        mp = pl.BlockSpec((1, 1, D), wrap(self.prompt_mod))
        ms = pl.BlockSpec((self.sb, 1, D), wrap(self.sample_x))
        return xp, xs, mp, ms


def _mod_kernel(c_ref, w_ref, b_ref, o_ref):
    c = c_ref[...]
    sc = (c * _sigmoid(c)).astype(BF16)
    o_ref[...] = jnp.dot(sc, w_ref[...].astype(BF16), preferred_element_type=F32) + b_ref[...]


def _modulations(c_all, w_ada, b_ada):
    nb = c_all.shape[0]
    tn = 1024
    return pl.pallas_call(
        _mod_kernel,
        out_shape=jax.ShapeDtypeStruct((nb, 6 * D), F32),
        grid=(6 * D // tn,),
        in_specs=[pl.BlockSpec((nb, D), lambda j: (0, 0)),
                  pl.BlockSpec((D, tn), lambda j: (0, j)),
                  pl.BlockSpec((1, tn), lambda j: (0, j))],
        out_specs=pl.BlockSpec((nb, tn), lambda j: (0, j)),
        compiler_params=_cparams(("parallel",)),
        name="modulations",
    )(c_all, w_ada, b_ada.reshape(1, -1))


def _norm1_kernel(xp_ref, xs_ref, g_ref, shp_ref, scp_ref, shs_ref, scs_ref, o_ref, *, n_prompt):
    i = pl.program_id(0)

    @pl.when(i < n_prompt)
    def _():
        h = _norm_mod(xp_ref[...], g_ref[...], shp_ref[...], scp_ref[...])
        o_ref[...] = h.reshape(o_ref.shape).astype(BF16)

    @pl.when(i >= n_prompt)
    def _():
        h = _norm_mod(xs_ref[...], g_ref[...], shs_ref[...], scs_ref[...])
        o_ref[...] = h.reshape(o_ref.shape).astype(BF16)


def _norm1(x_prompt, x_sample, g, mp, ms):
    tl = _Tiles(256)
    xp, xs, mps, mss = tl.specs()
    return pl.pallas_call(
        functools.partial(_norm1_kernel, n_prompt=tl.n_prompt),
        out_shape=jax.ShapeDtypeStruct((T, D), BF16),
        grid=(tl.n,),
        in_specs=[xp, xs, pl.BlockSpec((1, 1, D), lambda i: (0, 0, 0)), mps, mps, mss, mss],
        out_specs=pl.BlockSpec((tl.tm, D), lambda i: (i, 0)),
        compiler_params=_cparams(("parallel",)),
        name="norm1_adaln",
    )(x_prompt, x_sample, g.reshape(1, 1, D), mp[0], mp[1], ms[0], ms[1])


def _in_proj_kernel(*refs, mode, out_scale):
    if mode == "glu":
        h_ref, wa_ref, wb_ref, ba_ref, bb_ref, o_ref = refs
    else:
        h_ref, wa_ref, ba_ref, o_ref = refs
    h = h_ref[...]
    a = jnp.dot(h, wa_ref[...].astype(BF16), preferred_element_type=F32) + ba_ref[...]
    if mode == "glu":
        b = jnp.dot(h, wb_ref[...].astype(BF16), preferred_element_type=F32) + bb_ref[...]
        a = a * _sigmoid(b)
    elif mode == "sigmoid":
        a = _sigmoid(a)
    elif out_scale != 1.0:
        a = a * out_scale
    o_ref[...] = a.astype(o_ref.dtype)


def _in_proj(h, w_b, b_in2, col_a, col_b, n_cols, mode, out_dtype, out_scale=1.0):
    tm, tn = 1536, 512
    ca, cb = col_a // tn, (col_b // tn if col_b is not None else 0)
    in_specs = [pl.BlockSpec((tm, D), lambda i, j: (i, 0)),
                pl.BlockSpec((D, tn), lambda i, j: (0, ca + j))]
    args = [h, w_b]
    if mode == "glu":
        in_specs.append(pl.BlockSpec((D, tn), lambda i, j: (0, cb + j)))
        args.append(w_b)
    in_specs.append(pl.BlockSpec((1, tn), lambda i, j: (0, ca + j)))
    args.append(b_in2)
    if mode == "glu":
        in_specs.append(pl.BlockSpec((1, tn), lambda i, j: (0, cb + j)))
        args.append(b_in2)
    return pl.pallas_call(
        functools.partial(_in_proj_kernel, mode=mode, out_scale=out_scale),
        out_shape=jax.ShapeDtypeStruct((T, n_cols), out_dtype),
        grid=(T // tm, n_cols // tn),
        in_specs=in_specs,
        out_specs=pl.BlockSpec((tm, tn), lambda i, j: (i, j)),
        compiler_params=_cparams(("parallel", "arbitrary")),
        name="in_proj_" + mode,
    )(*args)


CONV_TL = 128
HALO = 32
CONV_STEPS = TP // CONV_TL


def _ln_swish(z, g, b):
    mu = jnp.mean(z, axis=-1, keepdims=True)
    d = z - mu
    var = jnp.mean(d * d, axis=-1, keepdims=True)
    zn = d * lax.rsqrt(var + EPS) * g + b
    return zn * _sigmoid(zn)


def _conv_prompt_kernel(u_ref, halo_ref, w_ref, dwb_ref, g_ref, b_ref, o_ref, win, z_scr):
    i = pl.program_id(0)

    @pl.when(i < CONV_STEPS)
    def _():
        first = (i % (SP // CONV_TL)) == 0
        base = HALO - (CW - 1)
        for s in range(D // 128):
            cs = slice(s * 128, (s + 1) * 128)
            win[s, 0:HALO, :] = jnp.where(first, 0.0, halo_ref[:, cs])
            win[s, HALO:HALO + CONV_TL, :] = u_ref[:, cs]
        for s in range(D // 128):
            cs = slice(s * 128, (s + 1) * 128)
            acc = None
            for rho in range(8):
                offs = [base + j for j in range(CW) if (base + j) % 8 == rho]
                seg = win[s, offs[0]:offs[-1] + CONV_TL, :]
                for o in offs:
                    j = o - base
                    term = seg[o - offs[0]:o - offs[0] + CONV_TL] * w_ref[j:j + 1, cs]
                    acc = term if acc is None else acc + term
            z_scr[:, cs] = acc + dwb_ref[:, cs]
        o_ref[...] = _ln_swish(z_scr[...], g_ref[...], b_ref[...]).astype(BF16)

    @pl.when(i >= CONV_STEPS)
    def _():
        o_ref[...] = jnp.zeros_like(o_ref)


def _conv_prompt(u_all, dw_w, dw_b, ln_g, ln_b):
    hb = CONV_TL // HALO
    last = CONV_STEPS - 1
    const = lambda i: (0, 0)
    return pl.pallas_call(
        _conv_prompt_kernel,
        out_shape=jax.ShapeDtypeStruct((T, D), BF16),
        grid=(T // CONV_TL,),
        in_specs=[pl.BlockSpec((CONV_TL, D), lambda i: (jnp.minimum(i, last), 0)),
                  pl.BlockSpec((HALO, D), lambda i: (jnp.maximum(jnp.minimum(i, last) * hb - 1, 0), 0)),
                  pl.BlockSpec((CW, D), const),
                  pl.BlockSpec((1, D), const),
                  pl.BlockSpec((1, D), const),
                  pl.BlockSpec((1, D), const)],
        out_specs=pl.BlockSpec((CONV_TL, D), lambda i: (i, 0)),
        scratch_shapes=[pltpu.VMEM((D // 128, HALO + CONV_TL, 128), F32), pltpu.VMEM((CONV_TL, D), F32)],
        compiler_params=_cparams(("parallel",)),
        name="conv_prompt",
    )(u_all, u_all, dw_w, dw_b.reshape(1, D), ln_g.reshape(1, D), ln_b.reshape(1, D))


CONV_BB = 16


def _conv_sample_kernel(st_ref, u_ref, w_ref, dwb_ref, g_ref, b_ref, o_ref, ns_ref, win, z_scr):
    win[0:CW - 1] = st_ref[...]
    win[CW - 1:CW - 1 + SS] = u_ref[...]
    ns_ref[0:CW - 1 - SS] = st_ref[SS:CW - 1]
    ns_ref[CW - 1 - SS:CW - 1] = u_ref[...]
    for s in range(D // 128):
        cs = slice(s * 128, (s + 1) * 128)
        acc = None
        for j in range(CW):
            term = win[j:j + SS, :, cs] * w_ref[j:j + 1, cs][None]
            acc = term if acc is None else acc + term
        z_scr[:, :, cs] = acc + dwb_ref[:, cs][None]
    o_ref[...] = _ln_swish(z_scr[...], g_ref[...][None], b_ref[...][None]).astype(BF16)


def _conv_sample(state_t, u_t, dw_w, dw_b, ln_g, ln_b):
    const = lambda i: (0, 0)
    batch = lambda i: (0, i, 0)
    return pl.pallas_call(
        _conv_sample_kernel,
        out_shape=(jax.ShapeDtypeStruct((SS, BS, D), BF16), jax.ShapeDtypeStruct((CW - 1, BS, D), F32)),
        grid=(BS // CONV_BB,),
        in_specs=[pl.BlockSpec((CW - 1, CONV_BB, D), batch),
                  pl.BlockSpec((SS, CONV_BB, D), batch),
                  pl.BlockSpec((CW, D), const),
                  pl.BlockSpec((1, D), const),
                  pl.BlockSpec((1, D), const),
                  pl.BlockSpec((1, D), const)],
        out_specs=(pl.BlockSpec((SS, CONV_BB, D), batch),
                   pl.BlockSpec((CW - 1, CONV_BB, D), batch)),
        scratch_shapes=[pltpu.VMEM((CW - 1 + SS, CONV_BB, D), F32), pltpu.VMEM((SS, CONV_BB, D), F32)],
        compiler_params=_cparams(("parallel",)),
        name="conv_sample",
    )(state_t, u_t, dw_w, dw_b.reshape(1, D), ln_g.reshape(1, D), ln_b.reshape(1, D))


ATT_STEPS = TP // WIN


def _attn_prompt_kernel(sink_ref, q_ref, kvc_ref, kvp_ref, o_ref):
    i = pl.program_id(0)

    @pl.when(i < ATT_STEPS)
    def _():
        n = i % (SP // WIN)
        row = lax.broadcasted_iota(jnp.int32, (WIN, 2 * WIN), 0)
        col = lax.broadcasted_iota(jnp.int32, (WIN, 2 * WIN), 1)
        diff = row + WIN - col
        first_key = jnp.where(n > 0, 0, WIN)
        mask = (diff >= 0) & (diff < WIN) & (col >= first_key)
        kvc = kvc_ref[...].astype(BF16)
        kvp = kvp_ref[...].astype(BF16)
        outs = []
        for kh in range(NKV):
            k2 = jnp.concatenate([kvp[:, kh * HD:(kh + 1) * HD], kvc[:, kh * HD:(kh + 1) * HD]], axis=0)
            v2 = jnp.concatenate([kvp[:, KVW + kh * HD:KVW + (kh + 1) * HD],
                                  kvc[:, KVW + kh * HD:KVW + (kh + 1) * HD]], axis=0)
            for g in range(GRP):
                h = kh * GRP + g
                sink = sink_ref[h]
                qh = q_ref[:, h * HD:(h + 1) * HD]
                s = lax.dot_general(qh, k2, (((1,), (1,)), ((), ())), preferred_element_type=F32)
                s = jnp.where(mask, s, -jnp.inf)
                m = jnp.maximum(jnp.max(s, axis=-1, keepdims=True), sink)
                p = jnp.exp(s - m)
                den = jnp.sum(p, axis=-1, keepdims=True) + jnp.exp(sink - m)
                o = jnp.dot(p.astype(BF16), v2, preferred_element_type=F32)
                outs.append(o / den)
        o_ref[...] = jnp.concatenate(outs, axis=-1).astype(BF16)

    @pl.when(i >= ATT_STEPS)
    def _():
        o_ref[...] = jnp.zeros_like(o_ref)


def _attn_prompt(sinks, q_all, kv_all):
    last = ATT_STEPS - 1
    nblk = SP // WIN

    def prev_block(i):
        ic = jnp.minimum(i, last)
        return (jnp.where(ic % nblk == 0, ic, ic - 1), 0)

    return pl.pallas_call(
        _attn_prompt_kernel,
        out_shape=jax.ShapeDtypeStruct((T, D), BF16),
        grid=(T // WIN,),
        in_specs=[pl.BlockSpec(memory_space=pltpu.SMEM),
                  pl.BlockSpec((WIN, D), lambda i: (jnp.minimum(i, last), 0)),
                  pl.BlockSpec((WIN, 2 * KVW), lambda i: (jnp.minimum(i, last), 0)),
                  pl.BlockSpec((WIN, 2 * KVW), prev_block)],
        out_specs=pl.BlockSpec((WIN, D), lambda i: (i, 0)),
        compiler_params=_cparams(("parallel",)),
        name="attn_prompt",
    )(sinks, q_all, kv_all, kv_all)


ATT_BB = 8
QG = SS * GRP


def _attn_sample_kernel(sink_ref, q_ref, kvn_ref, ck_ref, cv_ref, o_ref):
    rr = lax.broadcasted_iota(jnp.int32, (QG, WIN), 0)
    qi_c = rr % SS
    key_c = lax.broadcasted_iota(jnp.int32, (QG, WIN), 1)
    mask_c = (key_c > qi_c)[None]
    rr_n = lax.broadcasted_iota(jnp.int32, (QG, SS), 0)
    key_n = lax.broadcasted_iota(jnp.int32, (QG, SS), 1)
    mask_n = (key_n <= rr_n % SS)[None]
    grow = lax.broadcasted_iota(jnp.int32, (QG, 1), 0) // SS
    for kh in range(NKV):
        sink = jnp.zeros((QG, 1), F32)
        for g in range(GRP):
            sink = jnp.where(grow == g, sink_ref[kh * GRP + g], sink)
        sink = sink[None]
        q = q_ref[:, kh]
        kc = ck_ref[:, kh].astype(BF16)
        vc = cv_ref[:, kh].astype(BF16)
        kn = kvn_ref[:, :, kh * HD:(kh + 1) * HD].astype(BF16)
        vn = kvn_ref[:, :, KVW + kh * HD:KVW + (kh + 1) * HD].astype(BF16)
        s1 = jnp.einsum("bqd,bds->bqs", q, kc, preferred_element_type=F32)
        s2 = jnp.einsum("bqd,bsd->bqs", q, kn, preferred_element_type=F32)
        s1 = jnp.where(mask_c, s1, -jnp.inf)
        s2 = jnp.where(mask_n, s2, -jnp.inf)
        m = jnp.maximum(jnp.maximum(jnp.max(s1, axis=-1, keepdims=True),
                                    jnp.max(s2, axis=-1, keepdims=True)), sink)
        p1 = jnp.exp(s1 - m)
        p2 = jnp.exp(s2 - m)
        den = (jnp.sum(p1, axis=-1, keepdims=True) + jnp.sum(p2, axis=-1, keepdims=True)
               + jnp.exp(sink - m))
        o = (jnp.einsum("bqs,bds->bqd", p1.astype(BF16), vc, preferred_element_type=F32)
             + jnp.einsum("bqs,bsd->bqd", p2.astype(BF16), vn, preferred_element_type=F32))
        o_ref[:, kh] = (o / den).astype(BF16)


def _attn_sample(sinks, q_g, kv_new, ck, cv):
    return pl.pallas_call(
        _attn_sample_kernel,
        out_shape=jax.ShapeDtypeStruct((BS, NKV, QG, HD), BF16),
        grid=(BS // ATT_BB,),
        in_specs=[pl.BlockSpec(memory_space=pltpu.SMEM),
                  pl.BlockSpec((ATT_BB, NKV, QG, HD), lambda i: (i, 0, 0, 0)),
                  pl.BlockSpec((ATT_BB, SS, 2 * KVW), lambda i: (i, 0, 0)),
                  pl.BlockSpec((ATT_BB, NKV, HD, WIN), lambda i: (i, 0, 0, 0)),
                  pl.BlockSpec((ATT_BB, NKV, HD, WIN), lambda i: (i, 0, 0, 0))],
        out_specs=pl.BlockSpec((ATT_BB, NKV, QG, HD), lambda i: (i, 0, 0, 0)),
        compiler_params=_cparams(("parallel",)),
        name="attn_sample",
    )(sinks, q_g, kv_new, ck, cv)


def _merge_kernel(zs_ref, at_ref, wc_ref, wa_ref, bc_ref, ba_ref, gc_ref, ga_ref, o_ref):
    conv = jnp.dot(zs_ref[...], wc_ref[...].astype(BF16), preferred_element_type=F32) + bc_ref[...]
    attn = jnp.dot(at_ref[...], wa_ref[...].astype(BF16), preferred_element_type=F32) + ba_ref[...]
    o_ref[...] = (gc_ref[...] * conv + ga_ref[...] * attn).astype(BF16)


def _merge(zs, attn, w_co, w_ao, b_co, b_ao, gates):
    tm, tn = 1024, 512
    ga_off = D // tn
    return pl.pallas_call(
        _merge_kernel,
        out_shape=jax.ShapeDtypeStruct((T, D), BF16),
        grid=(T // tm, D // tn),
        in_specs=[pl.BlockSpec((tm, D), lambda i, j: (i, 0)),
                  pl.BlockSpec((tm, D), lambda i, j: (i, 0)),
                  pl.BlockSpec((D, tn), lambda i, j: (0, j)),
                  pl.BlockSpec((D, tn), lambda i, j: (0, j)),
                  pl.BlockSpec((1, tn), lambda i, j: (0, j)),
                  pl.BlockSpec((1, tn), lambda i, j: (0, j)),
                  pl.BlockSpec((tm, tn), lambda i, j: (i, j)),
                  pl.BlockSpec((tm, tn), lambda i, j: (i, ga_off + j))],
        out_specs=pl.BlockSpec((tm, tn), lambda i, j: (i, j)),
        compiler_params=_cparams(("parallel", "arbitrary")),
        name="mixer_merge",
    )(zs, attn, w_co, w_ao, b_co.reshape(1, D), b_ao.reshape(1, D), gates, gates)


def _router_logits(h, w, b):
    h_hi = h.astype(BF16)
    h_lo = (h - h_hi.astype(F32)).astype(BF16)
    w_hi = w.astype(BF16)
    w_lo = (w - w_hi.astype(F32)).astype(BF16)
    dn = (((1,), (1,)), ((), ()))
    return (lax.dot_general(w_hi, h_hi, dn, preferred_element_type=F32)
            + lax.dot_general(w_hi, h_lo, dn, preferred_element_type=F32)
            + lax.dot_general(w_lo, h_hi, dn, preferred_element_type=F32)) + b


def _out_proj_kernel(mg_ref, xp_ref, xs_ref, w_ref, g_ref, m2p_ref, m3p_ref, m4p_ref,
                     m2s_ref, m3s_ref, m4s_ref, wr_ref, br_ref, x1_ref, h2_ref, lg_ref, *, n_prompt):
    i = pl.program_id(0)
    y = jnp.dot(mg_ref[...], w_ref[...], preferred_element_type=F32)

    def finish(x_ref, m2_ref, m3_ref, m4_ref):
        x1 = x_ref[...] + m2_ref[...] * y.reshape(x_ref.shape)
        x1_ref[...] = x1.reshape(x1_ref.shape)
        h2 = _norm_mod(x1, g_ref[...], m3_ref[...], m4_ref[...]).reshape(x1_ref.shape)
        h2_ref[...] = h2
        lg_ref[...] = _router_logits(h2, wr_ref[...], br_ref[...])

    @pl.when(i < n_prompt)
    def _():
        finish(xp_ref, m2p_ref, m3p_ref, m4p_ref)

    @pl.when(i >= n_prompt)
    def _():
        finish(xs_ref, m2s_ref, m3s_ref, m4s_ref)


def _out_proj(merged, x_prompt, x_sample, w_out_b, g2, mp, ms, w_router, b_router):
    tl = _Tiles(256)
    xp, xs, mps, mss = tl.specs()
    row = pl.BlockSpec((tl.tm, D), lambda i: (i, 0))
    return pl.pallas_call(
        functools.partial(_out_proj_kernel, n_prompt=tl.n_prompt),
        out_shape=(jax.ShapeDtypeStruct((T, D), F32),
                   jax.ShapeDtypeStruct((T, D), F32),
                   jax.ShapeDtypeStruct((NE, T), F32)),
        grid=(tl.n,),
        in_specs=[row, xp, xs, pl.BlockSpec((D, D), lambda i: (0, 0)),
                  pl.BlockSpec((1, 1, D), lambda i: (0, 0, 0)), mps, mps, mps, mss, mss, mss,
                  pl.BlockSpec((NE, D), lambda i: (0, 0)),
                  pl.BlockSpec((NE, 1), lambda i: (0, 0))],
        out_specs=(row, row, pl.BlockSpec((NE, tl.tm), lambda i: (0, i))),
        compiler_params=_cparams(("parallel",)),
        name="out_proj_norm2",
    )(merged, x_prompt, x_sample, w_out_b, g2.reshape(1, 1, D), mp[2], mp[3], mp[4], ms[2], ms[3], ms[4],
      w_router.T, b_router.reshape(NE, 1))


ROUTE_TR = 512


def _router_kernel(lg_ref, tri_ref, idx_ref, gate_ref, rank_ref, cnt_ref, carry):
    @pl.when(pl.program_id(0) == 0)
    def _():
        carry[...] = jnp.zeros_like(carry)

    eid = lax.broadcasted_iota(jnp.int32, (NE, ROUTE_TR), 0).astype(F32)
    work = lg_ref[...]
    vals, ids = [], []
    onehot = jnp.zeros((NE, ROUTE_TR), F32)
    for _ in range(TOPK):
        m = jnp.max(work, axis=0, keepdims=True)
        sel = jnp.min(jnp.where(work == m, eid, float(NE)), axis=0, keepdims=True)
        hit = eid == sel
        work = jnp.where(hit, -jnp.inf, work)
        onehot = jnp.where(hit, 1.0, onehot)
        vals.append(m)
        ids.append(sel)
    es = [jnp.exp(v - vals[0]) for v in vals]
    den = es[0] + es[1] + es[2] + es[3]
    before = jnp.dot(onehot.astype(BF16), tri_ref[...], preferred_element_type=F32) + carry[:, 0:1]
    for k in range(TOPK):
        idx_ref[k:k + 1, :] = ids[k].astype(jnp.int32)
        gate_ref[k:k + 1, :] = es[k] / den
        rank_ref[k:k + 1, :] = jnp.sum(jnp.where(eid == ids[k], before, 0.0), axis=0,
                                       keepdims=True).astype(jnp.int32)
    carry[...] = carry[...] + jnp.sum(onehot, axis=1, keepdims=True)
    cnt_ref[...] = carry[...]


def _router(logits):
    tri = jnp.triu(jnp.ones((ROUTE_TR, ROUTE_TR), F32), 1).astype(BF16)
    row = lambda i: (0, i)
    return pl.pallas_call(
        _router_kernel,
        out_shape=(jax.ShapeDtypeStruct((TOPK, T), jnp.int32),
                   jax.ShapeDtypeStruct((TOPK, T), F32),
                   jax.ShapeDtypeStruct((TOPK, T), jnp.int32),
                   jax.ShapeDtypeStruct((NE, 128), F32)),
        grid=(T // ROUTE_TR,),
        in_specs=[pl.BlockSpec((NE, ROUTE_TR), row),
                  pl.BlockSpec((ROUTE_TR, ROUTE_TR), lambda i: (0, 0))],
        out_specs=(pl.BlockSpec((TOPK, ROUTE_TR), row),
                   pl.BlockSpec((TOPK, ROUTE_TR), row),
                   pl.BlockSpec((TOPK, ROUTE_TR), row),
                   pl.BlockSpec((NE, 128), lambda i: (0, 0))),
        scratch_shapes=[pltpu.VMEM((NE, 128), F32)],
        compiler_params=_cparams(("arbitrary",)),
        name="router_top4",
    )(logits, tri)


DISPATCH_ROWS = SUB
DISPATCH_STEPS = P_MAX // DISPATCH_ROWS


def _dispatch_kernel(tok_ref, nt_ref, h_hbm, o_ref, buf, sem):
    i = pl.program_id(0)
    nt = nt_ref[0] * SUB // DISPATCH_ROWS

    def issue(step, slot):
        for r in range(DISPATCH_ROWS):
            pltpu.make_async_copy(h_hbm.at[pl.ds(tok_ref[step * DISPATCH_ROWS + r], 1)],
                                  buf.at[slot, pl.ds(r, 1)], sem.at[slot]).start()

    @pl.when(i == 0)
    def _():
        issue(0, 0)

    @pl.when(i + 1 < nt)
    def _():
        issue(i + 1, (i + 1) % 2)

    @pl.when(i < nt)
    def _():
        slot = i % 2
        pltpu.make_async_copy(h_hbm.at[pl.ds(0, DISPATCH_ROWS)], buf.at[slot], sem.at[slot]).wait()
        o_ref[...] = buf[slot].astype(BF16)

    @pl.when(i >= nt)
    def _():
        o_ref[...] = jnp.zeros_like(o_ref)


def _dispatch(tok_of_slot, nt_used, h2):
    return pl.pallas_call(
        _dispatch_kernel,
        out_shape=jax.ShapeDtypeStruct((P_MAX, D), BF16),
        grid_spec=pltpu.PrefetchScalarGridSpec(
            num_scalar_prefetch=2,
            grid=(DISPATCH_STEPS,),
            in_specs=[pl.BlockSpec(memory_space=pl.ANY)],
            out_specs=pl.BlockSpec((DISPATCH_ROWS, D), lambda i, *_: (i, 0)),
            scratch_shapes=[pltpu.VMEM((2, DISPATCH_ROWS, D), F32),
                            pltpu.SemaphoreType.DMA((2,))]),
        compiler_params=_cparams(("arbitrary",)),
        name="moe_dispatch",
    )(tok_of_slot, nt_used, h2)


def _moe_kernel(ge_ref, gs_ref, gn_ref, nt_ref, w1_ref, b1_ref, w2_ref, b2_ref, sel_ref, xs_hbm, y_hbm,
                x_buf, acc, h_scr, sem_in, sem_out):
    del ge_ref
    g = pl.program_id(0)
    c = pl.program_id(1)
    nsub = gn_ref[g]
    start = gs_ref[g]

    def rows_of(r):
        return pl.ds(pl.multiple_of(r * SUB, SUB), SUB)

    def x_copy(r):
        return pltpu.make_async_copy(xs_hbm.at[pl.ds((start + r) * SUB, SUB)], x_buf.at[rows_of(r)],
                                     sem_in.at[r])

    def y_copy(r):
        return pltpu.make_async_copy(acc.at[rows_of(r)], y_hbm.at[pl.ds((start + r) * SUB, SUB)], sem_out)

    def for_subtiles(fn, n):
        def body(r, carry):
            fn(r)
            return carry
        lax.fori_loop(0, n, body, 0)

    def up_proj(r):
        return (jnp.dot(x_buf[rows_of(r), :], w1_ref[0].astype(BF16), preferred_element_type=F32)
                + b1_ref[0])

    def down_proj(r, h):
        hn = jnp.concatenate(
            [pltpu.roll(h[:, k * 128:(k + 1) * 128], 127, 1) for k in range(2 * TF // 128)], axis=1)
        glu = jnp.minimum(h, LIMIT)
        lin = jnp.clip(hn, -LIMIT, LIMIT)
        act = (glu * _sigmoid(ALPHA * glu) * (lin + 1.0)).astype(BF16)
        sel = sel_ref[...]
        act = jnp.concatenate(
            [jnp.dot(act[:, k * MXU_N:(k + 1) * MXU_N], sel, preferred_element_type=F32)
             for k in range(2 * TF // MXU_N)], axis=1).astype(BF16)
        rows = rows_of(r)
        acc[rows, :] = acc[rows, :] + jnp.dot(act, w2_ref[0].astype(BF16), preferred_element_type=F32)

    @pl.when(nsub > 0)
    def _():
        @pl.when(c == 0)
        def _():
            for_subtiles(lambda r: x_copy(r).start(), nsub)

            def init(r):
                acc[rows_of(r), :] = jnp.broadcast_to(b2_ref[0], (SUB, D))
            for_subtiles(init, nsub)
            x_copy(0).wait()

        h_scr[0] = up_proj(0)

        def step(r):
            @pl.when(c == 0)
            def _():
                x_copy(r + 1).wait()

            h = h_scr[r % 2]
            h_scr[(r + 1) % 2] = up_proj(r + 1)
            down_proj(r, h)

            @pl.when(c == NC - 1)
            def _():
                y_copy(r).start()
        for_subtiles(step, nsub - 1)
        down_proj(nsub - 1, h_scr[(nsub - 1) % 2])

        @pl.when(c == NC - 1)
        def _():
            y_copy(nsub - 1).start()
            for_subtiles(lambda r: y_copy(r).wait(), nsub)

    @pl.when((g == G_MAX - 1) & (c == NC - 1))
    def _():
        acc[0:SUB, :] = jnp.zeros((SUB, D), F32)

        def fill(tile, carry):
            cp = pltpu.make_async_copy(acc.at[pl.ds(0, SUB)], y_hbm.at[pl.ds(tile * SUB, SUB)], sem_out)
            cp.start()
            cp.wait()
            return carry
        lax.fori_loop(nt_ref[0], NT_MAX, fill, 0)


def _moe(g_expert, g_start, g_nsub, nt_used, w1, b1, w2, b2, xs):
    sel = (jnp.arange(MXU_N)[:, None] == 2 * jnp.arange(MXU_N // 2)[None, :]).astype(BF16)

    def cidx(g, c, gn):
        return jnp.where(gn[g] > 0, c, NC - 1)

    return pl.pallas_call(
        _moe_kernel,
        out_shape=jax.ShapeDtypeStruct((P_MAX, D), F32),
        grid_spec=pltpu.PrefetchScalarGridSpec(
            num_scalar_prefetch=4,
            grid=(G_MAX, NC),
            in_specs=[pl.BlockSpec((1, D, 2 * TF), lambda g, c, ge, gs, gn, nt: (ge[g], 0, cidx(g, c, gn))),
                      pl.BlockSpec((1, 1, 2 * TF), lambda g, c, ge, gs, gn, nt: (ge[g], 0, cidx(g, c, gn))),
                      pl.BlockSpec((1, TF, D), lambda g, c, ge, gs, gn, nt: (ge[g], cidx(g, c, gn), 0)),
                      pl.BlockSpec((1, 1, D), lambda g, c, ge, gs, gn, nt: (ge[g], 0, 0)),
                      pl.BlockSpec((MXU_N, MXU_N // 2), lambda g, c, ge, gs, gn, nt: (0, 0)),
                      pl.BlockSpec(memory_space=pl.ANY)],
            out_specs=pl.BlockSpec(memory_space=pl.ANY),
            scratch_shapes=[pltpu.VMEM((GSUB * SUB, D), BF16),
                            pltpu.VMEM((GSUB * SUB, D), F32),
                            pltpu.VMEM((2, SUB, 2 * TF), F32),
                            pltpu.SemaphoreType.DMA((GSUB,)),
                            pltpu.SemaphoreType.DMA]),
        compiler_params=_cparams(("arbitrary", "arbitrary")),
        name="moe_experts",
    )(g_expert, g_start, g_nsub, nt_used, w1, b1.reshape(NE, 1, 2 * DFF), w2, b2.reshape(NE, 1, D), sel, xs)


COMB_TM = 128


def _combine_kernel(slot_ref, x1_ref, gate_ref, m5_ref, g_ref, y_hbm, o_ref, buf, sem, *, tok0):
    i = pl.program_id(0)
    n = pl.num_programs(0)

    def issue(tile, slot):
        for k in range(TOPK):
            for r in range(COMB_TM):
                s = slot_ref[k * T + tok0 + tile * COMB_TM + r]
                pltpu.make_async_copy(y_hbm.at[pl.ds(s, 1)], buf.at[slot, k, pl.ds(r, 1)],
                                      sem.at[slot]).start()

    @pl.when(i == 0)
    def _():
        issue(0, 0)

    @pl.when(i + 1 < n)
    def _():
        issue(i + 1, (i + 1) % 2)

    slot = i % 2
    for k in range(TOPK):
        pltpu.make_async_copy(y_hbm.at[pl.ds(0, COMB_TM)], buf.at[slot, k], sem.at[slot]).wait()
    gates = gate_ref[...]
    f = None
    for k in range(TOPK):
        term = gates[:, k:k + 1] * buf[slot, k]
        f = term if f is None else f + term
    x2 = x1_ref[...].reshape(o_ref.shape) + m5_ref[...] * f.reshape(o_ref.shape)
    ms = jnp.mean(x2 * x2, axis=-1, keepdims=True)
    o_ref[...] = x2 * lax.rsqrt(ms + EPS) * g_ref[...]


def _combine(prompt, slots, x1, gates_t, m5, gf, y):
    tm = COMB_TM
    if prompt:
        nb, r, steps, rb0, shape3 = 1, tm, TP // tm, 0, (BP, SP, D)
        per_b = SP // tm
        x_index = lambda i, s: (i // per_b, i % per_b, 0)
        m_index = lambda i, s: (i // per_b, 0, 0)
    else:
        nb, r, steps, rb0, shape3 = tm // SS, SS, TS // tm, TP // tm, (BS, SS, D)
        x_index = lambda i, s: (i, 0, 0)
        m_index = x_index
    return pl.pallas_call(
        functools.partial(_combine_kernel, tok0=rb0 * tm),
        out_shape=jax.ShapeDtypeStruct(shape3, F32),
        grid_spec=pltpu.PrefetchScalarGridSpec(
            num_scalar_prefetch=1,
            grid=(steps,),
            in_specs=[pl.BlockSpec((tm, D), lambda i, s: (rb0 + i, 0)),
                      pl.BlockSpec((tm, TOPK), lambda i, s: (rb0 + i, 0)),
                      pl.BlockSpec((nb, 1, D), m_index),
                      pl.BlockSpec((1, 1, D), lambda i, s: (0, 0, 0)),
                      pl.BlockSpec(memory_space=pl.ANY)],
            out_specs=pl.BlockSpec((nb, r, D), x_index),
            scratch_shapes=[pltpu.VMEM((2, TOPK, tm, D), F32), pltpu.SemaphoreType.DMA((2,))]),
        compiler_params=_cparams(("arbitrary",)),
        name="moe_combine_norm",
    )(slots, x1, gates_t, m5, gf.reshape(1, 1, D), y)


def _routing_tables(idx_t, rank_t, counts):
    tiles = (counts + SUB - 1) // SUB
    tile_end = jnp.cumsum(tiles)
    tile_start = tile_end - tiles
    nt_used = tile_end[-1:]
    experts = jnp.arange(NE, dtype=jnp.int32)
    base = jnp.sum(jnp.where(idx_t[..., None] == experts, tile_start * SUB, 0), axis=-1)
    slot = base + rank_t
    tok = jnp.tile(jnp.arange(T, dtype=jnp.int32), TOPK)
    tok_of_slot = jnp.zeros((P_MAX,), jnp.int32).at[slot.reshape(-1)].set(tok)
    ngrp = (tiles + GSUB - 1) // GSUB
    grp_end = jnp.cumsum(ngrp)
    grp_start = grp_end - ngrp
    n_groups = grp_end[-1]
    gid = jnp.arange(G_MAX, dtype=jnp.int32)
    gclamp = jnp.minimum(gid, jnp.maximum(n_groups - 1, 0))
    g_expert = jnp.sum(gclamp[:, None] >= grp_end[None, :], axis=-1).astype(jnp.int32)
    pick = lambda v: jnp.sum(jnp.where(g_expert[:, None] == experts[None, :], v[None, :], 0), axis=-1)
    local = gclamp - pick(grp_start)
    g_start = pick(tile_start) + local * GSUB
    g_nsub = jnp.where(gid < n_groups, jnp.minimum(GSUB, pick(tiles) - local * GSUB), 0)
    return (slot.reshape(-1).astype(jnp.int32), tok_of_slot, nt_used.astype(jnp.int32),
            g_expert, g_start.astype(jnp.int32), g_nsub.astype(jnp.int32))


def kernel(x_prompt, x_sample, c_prompt, c_sample, cache_k, cache_v, state_conv, w_ada, b_ada, norm1_g,
           w_in, b_in, conv_dw_w, conv_dw_b, conv_ln_g, conv_ln_b, w_conv_out, b_conv_out, sinks,
           w_attn_out, b_attn_out, w_out, norm2_g, w_router, b_router, w_mlp1, b_mlp1, w_mlp2, b_mlp2,
           norm_f_g):
    mods = _modulations(jnp.concatenate([c_prompt, c_sample], axis=0), w_ada, b_ada)
    mp = [mods[:BP, k * D:(k + 1) * D].reshape(BP, 1, D) for k in range(6)]
    ms = [mods[BP:, k * D:(k + 1) * D].reshape(BS, 1, D) for k in range(6)]

    h1 = _norm1(x_prompt, x_sample, norm1_g, mp, ms)
    w_in_b = w_in
    b_in2 = b_in.reshape(1, IN_W)
    u_all = _in_proj(h1, w_in_b, b_in2, OFF_GLU_A, OFF_GLU_B, D, "glu", F32)
    q_all = _in_proj(h1, w_in_b, b_in2, OFF_Q, None, D, "plain", BF16, HD ** -0.5)
    kv_all = _in_proj(h1, w_in_b, b_in2, OFF_KV, None, 2 * KVW, "plain", F32)
    gates = _in_proj(h1, w_in_b, b_in2, OFF_GC, None, 2 * D, "sigmoid", F32)

    zs = _conv_prompt(u_all, conv_dw_w, conv_dw_b, conv_ln_g, conv_ln_b)
    u_s_t = u_all[TP:].reshape(BS, SS, D).transpose(1, 0, 2)
    zs_s_t, conv_s_t = _conv_sample(state_conv.transpose(1, 0, 2), u_s_t, conv_dw_w, conv_dw_b,
                                    conv_ln_g, conv_ln_b)
    zs = lax.dynamic_update_slice(zs, zs_s_t.transpose(1, 0, 2).reshape(TS, D), (TP, 0))
    conv_s = conv_s_t.transpose(1, 0, 2)
    conv_p = jnp.stack([u_all[(b + 1) * SP - (CW - 1):(b + 1) * SP] for b in range(BP)])

    attn = _attn_prompt(sinks, q_all, kv_all)
    q_g = (q_all[TP:].reshape(BS, SS, NKV, GRP, HD).transpose(0, 2, 3, 1, 4)
           .reshape(BS, NKV, QG, HD))
    kv_s = kv_all[TP:].reshape(BS, SS, 2 * KVW)
    attn_s = _attn_sample(sinks, q_g, kv_s, cache_k.transpose(0, 2, 3, 1), cache_v.transpose(0, 2, 3, 1))
    attn_s = (attn_s.reshape(BS, NKV, GRP, SS, HD).transpose(0, 3, 1, 2, 4).reshape(TS, D))
    attn = lax.dynamic_update_slice(attn, attn_s, (TP, 0))
    kv_tail = jnp.stack([kv_all[(b + 1) * SP - WIN:(b + 1) * SP] for b in range(BP)])
    k_p = kv_tail[:, :, :KVW].reshape(BP, WIN, NKV, HD)
    v_p = kv_tail[:, :, KVW:].reshape(BP, WIN, NKV, HD)
    k_s = jnp.concatenate([cache_k[:, SS:], kv_s[:, :, :KVW].reshape(BS, SS, NKV, HD)], axis=1)
    v_s = jnp.concatenate([cache_v[:, SS:], kv_s[:, :, KVW:].reshape(BS, SS, NKV, HD)], axis=1)

    merged = _merge(zs, attn, w_conv_out, w_attn_out, b_conv_out, b_attn_out,
                    gates)
    x1, h2, logits = _out_proj(merged, x_prompt, x_sample, w_out.astype(BF16), norm2_g, mp, ms,
                               w_router, b_router)

    idx_t, gate_t, rank_t, cnt = _router(logits)
    counts = cnt[:, 0].astype(jnp.int32)
    slots, tok_of_slot, nt_used, g_expert, g_start, g_nsub = _routing_tables(idx_t, rank_t, counts)

    xs = _dispatch(tok_of_slot, nt_used, h2)
    y = _moe(g_expert, g_start, g_nsub, nt_used, w_mlp1, b_mlp1, w_mlp2, b_mlp2, xs)

    gates_tok = gate_t.T
    y_prompt = _combine(True, slots, x1, gates_tok, mp[5], norm_f_g, y)
    y_sample = _combine(False, slots, x1, gates_tok, ms[5], norm_f_g, y)
    return (y_prompt, y_sample, k_p, v_p, conv_p, k_s, v_s, conv_s)
```

```python
import functools

import jax
import jax.numpy as jnp
from jax import lax
from jax.experimental import pallas as pl
from jax.experimental.pallas import tpu as pltpu

F32 = jnp.float32
BF16 = jnp.bfloat16

D = 2048
BP, SP = 4, 2048
BS, SS = 128, 8
TP = BP * SP
TS = BS * SS
T = TP + TS
HD = 64
NH = 32
NKV = 4
GRP = NH // NKV
WIN = 128
KVW = NKV * HD
CW = 31
NE = 32
TOPK = 4
DFF = 2048
ALPHA = 1.702
LIMIT = 7.0
EPS = 1e-5
OFF_GLU_A, OFF_GLU_B, OFF_Q, OFF_KV, OFF_GC = 0, D, 2 * D, 3 * D, 3 * D + 2 * KVW
IN_W = 5 * D + 2 * KVW

VMEM_LIMIT = 56 * 1024 * 1024

SUB = 256
GSUB = 8
NT_MAX = (T * TOPK) // SUB + NE
P_MAX = NT_MAX * SUB
G_MAX = -(-((T * TOPK) // SUB) // GSUB) + NE
TF = 512
NC = DFF // TF
MXU_N = 256


def _cparams(sem):
    return pltpu.CompilerParams(dimension_semantics=sem, vmem_limit_bytes=VMEM_LIMIT)


def _sigmoid(x):
    return 1.0 / (1.0 + jnp.exp(-x))


def _norm_mod(x, g, shift, scale):
    ms = jnp.mean(x * x, axis=-1, keepdims=True)
    return (x * lax.rsqrt(ms + EPS) * g) * (1.0 + scale) + shift


class _Tiles:
    def __init__(self, tm):
        self.tm = tm
        self.per_b = SP // tm
        self.n_prompt = TP // tm
        self.n = T // tm
        self.sb = tm // SS

    def prompt_x(self, i):
        ip = jnp.minimum(i, self.n_prompt - 1)
        return (ip // self.per_b, ip % self.per_b, 0)

    def prompt_mod(self, i):
        return (jnp.minimum(i, self.n_prompt - 1) // self.per_b, 0, 0)

    def sample_x(self, i):
        return (jnp.maximum(i - self.n_prompt, 0), 0, 0)

    def specs(self, tail=()):
        wrap = lambda f: (lambda i, *_: f(i))
        xp = pl.BlockSpec((1, self.tm, D), wrap(self.prompt_x))
        xs = pl.BlockSpec((self.sb, SS, D), wrap(self.sample_x))
        mp = pl.BlockSpec((1, 1, D), wrap(self.prompt_mod))
        ms = pl.BlockSpec((self.sb, 1, D), wrap(self.sample_x))
        return xp, xs, mp, ms


def _mod_kernel(c_ref, w_ref, b_ref, o_ref):
    c = c_ref[...]
    sc = (c * _sigmoid(c)).astype(BF16)
    o_ref[...] = jnp.dot(sc, w_ref[...].astype(BF16), preferred_element_type=F32) + b_ref[...]


def _modulations(c_all, w_ada, b_ada):
    nb = c_all.shape[0]
    tn = 1024
    return pl.pallas_call(
        _mod_kernel,
        out_shape=jax.ShapeDtypeStruct((nb, 6 * D), F32),
        grid=(6 * D // tn,),
        in_specs=[pl.BlockSpec((nb, D), lambda j: (0, 0)),
                  pl.BlockSpec((D, tn), lambda j: (0, j)),
                  pl.BlockSpec((1, tn), lambda j: (0, j))],
        out_specs=pl.BlockSpec((nb, tn), lambda j: (0, j)),
        compiler_params=_cparams(("parallel",)),
        name="modulations",
    )(c_all, w_ada, b_ada.reshape(1, -1))


def _norm1_kernel(xp_ref, xs_ref, g_ref, shp_ref, scp_ref, shs_ref, scs_ref, o_ref, *, n_prompt):
    i = pl.program_id(0)

    @pl.when(i < n_prompt)
    def _():
        h = _norm_mod(xp_ref[...], g_ref[...], shp_ref[...], scp_ref[...])
        o_ref[...] = h.reshape(o_ref.shape).astype(BF16)

    @pl.when(i >= n_prompt)
    def _():
        h = _norm_mod(xs_ref[...], g_ref[...], shs_ref[...], scs_ref[...])
        o_ref[...] = h.reshape(o_ref.shape).astype(BF16)


def _norm1(x_prompt, x_sample, g, mp, ms):
    tl = _Tiles(256)
    xp, xs, mps, mss = tl.specs()
    return pl.pallas_call(
        functools.partial(_norm1_kernel, n_prompt=tl.n_prompt),
        out_shape=jax.ShapeDtypeStruct((T, D), BF16),
        grid=(tl.n,),
        in_specs=[xp, xs, pl.BlockSpec((1, 1, D), lambda i: (0, 0, 0)), mps, mps, mss, mss],
        out_specs=pl.BlockSpec((tl.tm, D), lambda i: (i, 0)),
        compiler_params=_cparams(("parallel",)),
        name="norm1_adaln",
    )(x_prompt, x_sample, g.reshape(1, 1, D), mp[0], mp[1], ms[0], ms[1])


def _in_proj_kernel(*refs, mode, out_scale):
    if mode == "glu":
        h_ref, wa_ref, wb_ref, ba_ref, bb_ref, o_ref = refs
    else:
        h_ref, wa_ref, ba_ref, o_ref = refs
    h = h_ref[...]
    a = jnp.dot(h, wa_ref[...].astype(BF16), preferred_element_type=F32) + ba_ref[...]
    if mode == "glu":
        b = jnp.dot(h, wb_ref[...].astype(BF16), preferred_element_type=F32) + bb_ref[...]
        a = a * _sigmoid(b)
    elif mode == "sigmoid":
        a = _sigmoid(a)
    elif out_scale != 1.0:
        a = a * out_scale
    o_ref[...] = a.astype(o_ref.dtype)


def _in_proj(h, w_b, b_in2, col_a, col_b, n_cols, mode, out_dtype, out_scale=1.0):
    tm, tn = (1536 if mode == "glu" else 3072), 512
    ca, cb = col_a // tn, (col_b // tn if col_b is not None else 0)
    in_specs = [pl.BlockSpec((tm, D), lambda i, j: (i, 0)),
                pl.BlockSpec((D, tn), lambda i, j: (0, ca + j))]
    args = [h, w_b]
    if mode == "glu":
        in_specs.append(pl.BlockSpec((D, tn), lambda i, j: (0, cb + j)))
        args.append(w_b)
    in_specs.append(pl.BlockSpec((1, tn), lambda i, j: (0, ca + j)))
    args.append(b_in2)
    if mode == "glu":
        in_specs.append(pl.BlockSpec((1, tn), lambda i, j: (0, cb + j)))
        args.append(b_in2)
    return pl.pallas_call(
        functools.partial(_in_proj_kernel, mode=mode, out_scale=out_scale),
        out_shape=jax.ShapeDtypeStruct((T, n_cols), out_dtype),
        grid=(T // tm, n_cols // tn),
        in_specs=in_specs,
        out_specs=pl.BlockSpec((tm, tn), lambda i, j: (i, j)),
        compiler_params=_cparams(("parallel", "arbitrary")),
        name="in_proj_" + mode,
    )(*args)


CONV_TL = 128
HALO = 32
CONV_STEPS = TP // CONV_TL


def _ln_swish(z, g, b):
    mu = jnp.mean(z, axis=-1, keepdims=True)
    d = z - mu
    var = jnp.mean(d * d, axis=-1, keepdims=True)
    zn = d * lax.rsqrt(var + EPS) * g + b
    return zn * _sigmoid(zn)


def _conv_prompt_kernel(u_ref, halo_ref, w_ref, dwb_ref, g_ref, b_ref, o_ref, win, z_scr):
    i = pl.program_id(0)

    @pl.when(i < CONV_STEPS)
    def _():
        first = (i % (SP // CONV_TL)) == 0
        base = HALO - (CW - 1)
        for s in range(D // 128):
            cs = slice(s * 128, (s + 1) * 128)
            win[s, 0:HALO, :] = jnp.where(first, 0.0, halo_ref[:, cs])
            win[s, HALO:HALO + CONV_TL, :] = u_ref[:, cs]
        for s in range(D // 128):
            cs = slice(s * 128, (s + 1) * 128)
            acc = None
            for rho in range(8):
                offs = [base + j for j in range(CW) if (base + j) % 8 == rho]
                seg = win[s, offs[0]:offs[-1] + CONV_TL, :]
                for o in offs:
                    j = o - base
                    term = seg[o - offs[0]:o - offs[0] + CONV_TL] * w_ref[j:j + 1, cs]
                    acc = term if acc is None else acc + term
            z_scr[:, cs] = acc + dwb_ref[:, cs]
        o_ref[...] = _ln_swish(z_scr[...], g_ref[...], b_ref[...]).astype(BF16)

    @pl.when(i >= CONV_STEPS)
    def _():
        o_ref[...] = jnp.zeros_like(o_ref)


def _conv_prompt(u_all, dw_w, dw_b, ln_g, ln_b):
    hb = CONV_TL // HALO
    last = CONV_STEPS - 1
    const = lambda i: (0, 0)
    return pl.pallas_call(
        _conv_prompt_kernel,
        out_shape=jax.ShapeDtypeStruct((T, D), BF16),
        grid=(T // CONV_TL,),
        in_specs=[pl.BlockSpec((CONV_TL, D), lambda i: (jnp.minimum(i, last), 0)),
                  pl.BlockSpec((HALO, D), lambda i: (jnp.maximum(jnp.minimum(i, last) * hb - 1, 0), 0)),
                  pl.BlockSpec((CW, D), const),
                  pl.BlockSpec((1, D), const),
                  pl.BlockSpec((1, D), const),
                  pl.BlockSpec((1, D), const)],
        out_specs=pl.BlockSpec((CONV_TL, D), lambda i: (i, 0)),
        scratch_shapes=[pltpu.VMEM((D // 128, HALO + CONV_TL, 128), F32), pltpu.VMEM((CONV_TL, D), F32)],
        compiler_params=_cparams(("parallel",)),
        name="conv_prompt",
    )(u_all, u_all, dw_w, dw_b.reshape(1, D), ln_g.reshape(1, D), ln_b.reshape(1, D))


CONV_BB = 16


def _conv_sample_kernel(st_ref, u_ref, w_ref, dwb_ref, g_ref, b_ref, o_ref, ns_ref, win, z_scr):
    win[0:CW - 1] = st_ref[...]
    win[CW - 1:CW - 1 + SS] = u_ref[...]
    ns_ref[0:CW - 1 - SS] = st_ref[SS:CW - 1]
    ns_ref[CW - 1 - SS:CW - 1] = u_ref[...]
    for s in range(D // 128):
        cs = slice(s * 128, (s + 1) * 128)
        acc = None
        for j in range(CW):
            term = win[j:j + SS, :, cs] * w_ref[j:j + 1, cs][None]
            acc = term if acc is None else acc + term
        z_scr[:, :, cs] = acc + dwb_ref[:, cs][None]
    o_ref[...] = _ln_swish(z_scr[...], g_ref[...][None], b_ref[...][None]).astype(BF16)


def _conv_sample(state_t, u_t, dw_w, dw_b, ln_g, ln_b):
    const = lambda i: (0, 0)
    batch = lambda i: (0, i, 0)
    return pl.pallas_call(
        _conv_sample_kernel,
        out_shape=(jax.ShapeDtypeStruct((SS, BS, D), BF16), jax.ShapeDtypeStruct((CW - 1, BS, D), F32)),
        grid=(BS // CONV_BB,),
        in_specs=[pl.BlockSpec((CW - 1, CONV_BB, D), batch),
                  pl.BlockSpec((SS, CONV_BB, D), batch),
                  pl.BlockSpec((CW, D), const),
                  pl.BlockSpec((1, D), const),
                  pl.BlockSpec((1, D), const),
                  pl.BlockSpec((1, D), const)],
        out_specs=(pl.BlockSpec((SS, CONV_BB, D), batch),
                   pl.BlockSpec((CW - 1, CONV_BB, D), batch)),
        scratch_shapes=[pltpu.VMEM((CW - 1 + SS, CONV_BB, D), F32), pltpu.VMEM((SS, CONV_BB, D), F32)],
        compiler_params=_cparams(("parallel",)),
        name="conv_sample",
    )(state_t, u_t, dw_w, dw_b.reshape(1, D), ln_g.reshape(1, D), ln_b.reshape(1, D))


ATT_STEPS = TP // WIN


def _attn_prompt_kernel(sink_ref, q_ref, kvc_ref, kvp_ref, o_ref):
    i = pl.program_id(0)

    @pl.when(i < ATT_STEPS)
    def _():
        n = i % (SP // WIN)
        row = lax.broadcasted_iota(jnp.int32, (WIN, 2 * WIN), 0)
        col = lax.broadcasted_iota(jnp.int32, (WIN, 2 * WIN), 1)
        diff = row + WIN - col
        first_key = jnp.where(n > 0, 0, WIN)
        mask = (diff >= 0) & (diff < WIN) & (col >= first_key)
        kvc = kvc_ref[...].astype(BF16)
        kvp = kvp_ref[...].astype(BF16)
        outs = []
        for kh in range(NKV):
            k2 = jnp.concatenate([kvp[:, kh * HD:(kh + 1) * HD], kvc[:, kh * HD:(kh + 1) * HD]], axis=0)
            v2 = jnp.concatenate([kvp[:, KVW + kh * HD:KVW + (kh + 1) * HD],
                                  kvc[:, KVW + kh * HD:KVW + (kh + 1) * HD]], axis=0)
            for g in range(GRP):
                h = kh * GRP + g
                sink = sink_ref[h]
                qh = q_ref[:, h * HD:(h + 1) * HD]
                s = lax.dot_general(qh, k2, (((1,), (1,)), ((), ())), preferred_element_type=F32)
                s = jnp.where(mask, s, -jnp.inf)
                m = jnp.maximum(jnp.max(s, axis=-1, keepdims=True), sink)
                p = jnp.exp(s - m)
                den = jnp.sum(p, axis=-1, keepdims=True) + jnp.exp(sink - m)
                o = jnp.dot(p.astype(BF16), v2, preferred_element_type=F32)
                outs.append(o / den)
        o_ref[...] = jnp.concatenate(outs, axis=-1).astype(BF16)

    @pl.when(i >= ATT_STEPS)
    def _():
        o_ref[...] = jnp.zeros_like(o_ref)


def _attn_prompt(sinks, q_all, kv_all):
    last = ATT_STEPS - 1
    nblk = SP // WIN

    def prev_block(i):
        ic = jnp.minimum(i, last)
        return (jnp.where(ic % nblk == 0, ic, ic - 1), 0)

    return pl.pallas_call(
        _attn_prompt_kernel,
        out_shape=jax.ShapeDtypeStruct((T, D), BF16),
        grid=(T // WIN,),
        in_specs=[pl.BlockSpec(memory_space=pltpu.SMEM),
                  pl.BlockSpec((WIN, D), lambda i: (jnp.minimum(i, last), 0)),
                  pl.BlockSpec((WIN, 2 * KVW), lambda i: (jnp.minimum(i, last), 0)),
                  pl.BlockSpec((WIN, 2 * KVW), prev_block)],
        out_specs=pl.BlockSpec((WIN, D), lambda i: (i, 0)),
        compiler_params=_cparams(("parallel",)),
        name="attn_prompt",
    )(sinks, q_all, kv_all, kv_all)


ATT_BB = 8
QG = SS * GRP


def _attn_sample_kernel(sink_ref, q_ref, kvn_ref, ck_ref, cv_ref, o_ref):
    rr = lax.broadcasted_iota(jnp.int32, (QG, WIN), 0)
    qi_c = rr % SS
    key_c = lax.broadcasted_iota(jnp.int32, (QG, WIN), 1)
    mask_c = (key_c > qi_c)[None]
    rr_n = lax.broadcasted_iota(jnp.int32, (QG, SS), 0)
    key_n = lax.broadcasted_iota(jnp.int32, (QG, SS), 1)
    mask_n = (key_n <= rr_n % SS)[None]
    grow = lax.broadcasted_iota(jnp.int32, (QG, 1), 0) // SS
    for kh in range(NKV):
        sink = jnp.zeros((QG, 1), F32)
        for g in range(GRP):
            sink = jnp.where(grow == g, sink_ref[kh * GRP + g], sink)
        sink = sink[None]
        q = q_ref[:, kh]
        kc = ck_ref[:, kh].astype(BF16)
        vc = cv_ref[:, kh].astype(BF16)
        kn = kvn_ref[:, :, kh * HD:(kh + 1) * HD].astype(BF16)
        vn = kvn_ref[:, :, KVW + kh * HD:KVW + (kh + 1) * HD].astype(BF16)
        s1 = jnp.einsum("bqd,bds->bqs", q, kc, preferred_element_type=F32)
        s2 = jnp.einsum("bqd,bsd->bqs", q, kn, preferred_element_type=F32)
        s1 = jnp.where(mask_c, s1, -jnp.inf)
        s2 = jnp.where(mask_n, s2, -jnp.inf)
        m = jnp.maximum(jnp.maximum(jnp.max(s1, axis=-1, keepdims=True),
                                    jnp.max(s2, axis=-1, keepdims=True)), sink)
        p1 = jnp.exp(s1 - m)
        p2 = jnp.exp(s2 - m)
        den = (jnp.sum(p1, axis=-1, keepdims=True) + jnp.sum(p2, axis=-1, keepdims=True)
               + jnp.exp(sink - m))
        o = (jnp.einsum("bqs,bds->bqd", p1.astype(BF16), vc, preferred_element_type=F32)
             + jnp.einsum("bqs,bsd->bqd", p2.astype(BF16), vn, preferred_element_type=F32))
        o_ref[:, kh] = (o / den).astype(BF16)


def _attn_sample(sinks, q_g, kv_new, ck, cv):
    return pl.pallas_call(
        _attn_sample_kernel,
        out_shape=jax.ShapeDtypeStruct((BS, NKV, QG, HD), BF16),
        grid=(BS // ATT_BB,),
        in_specs=[pl.BlockSpec(memory_space=pltpu.SMEM),
                  pl.BlockSpec((ATT_BB, NKV, QG, HD), lambda i: (i, 0, 0, 0)),
                  pl.BlockSpec((ATT_BB, SS, 2 * KVW), lambda i: (i, 0, 0)),
                  pl.BlockSpec((ATT_BB, NKV, HD, WIN), lambda i: (i, 0, 0, 0)),
                  pl.BlockSpec((ATT_BB, NKV, HD, WIN), lambda i: (i, 0, 0, 0))],
        out_specs=pl.BlockSpec((ATT_BB, NKV, QG, HD), lambda i: (i, 0, 0, 0)),
        compiler_params=_cparams(("parallel",)),
        name="attn_sample",
    )(sinks, q_g, kv_new, ck, cv)


def _merge_kernel(zs_ref, at_ref, wc_ref, wa_ref, bc_ref, ba_ref, gc_ref, ga_ref, o_ref):
    conv = jnp.dot(zs_ref[...], wc_ref[...].astype(BF16), preferred_element_type=F32) + bc_ref[...]
    attn = jnp.dot(at_ref[...], wa_ref[...].astype(BF16), preferred_element_type=F32) + ba_ref[...]
    o_ref[...] = (gc_ref[...] * conv + ga_ref[...] * attn).astype(BF16)


def _merge(zs, attn, w_co, w_ao, b_co, b_ao, gates):
    tm, tn = 1024, 512
    ga_off = D // tn
    return pl.pallas_call(
        _merge_kernel,
        out_shape=jax.ShapeDtypeStruct((T, D), BF16),
        grid=(T // tm, D // tn),
        in_specs=[pl.BlockSpec((tm, D), lambda i, j: (i, 0)),
                  pl.BlockSpec((tm, D), lambda i, j: (i, 0)),
                  pl.BlockSpec((D, tn), lambda i, j: (0, j)),
                  pl.BlockSpec((D, tn), lambda i, j: (0, j)),
                  pl.BlockSpec((1, tn), lambda i, j: (0, j)),
                  pl.BlockSpec((1, tn), lambda i, j: (0, j)),
                  pl.BlockSpec((tm, tn), lambda i, j: (i, j)),
                  pl.BlockSpec((tm, tn), lambda i, j: (i, ga_off + j))],
        out_specs=pl.BlockSpec((tm, tn), lambda i, j: (i, j)),
        compiler_params=_cparams(("parallel", "arbitrary")),
        name="mixer_merge",
    )(zs, attn, w_co, w_ao, b_co.reshape(1, D), b_ao.reshape(1, D), gates, gates)


def _router_logits(h, w, b):
    h_hi = h.astype(BF16)
    h_lo = (h - h_hi.astype(F32)).astype(BF16)
    w_hi = w.astype(BF16)
    w_lo = (w - w_hi.astype(F32)).astype(BF16)
    dn = (((1,), (1,)), ((), ()))
    return (lax.dot_general(w_hi, h_hi, dn, preferred_element_type=F32)
            + lax.dot_general(w_hi, h_lo, dn, preferred_element_type=F32)
            + lax.dot_general(w_lo, h_hi, dn, preferred_element_type=F32)) + b


def _out_proj_kernel(mg_ref, xp_ref, xs_ref, w_ref, g_ref, m2p_ref, m3p_ref, m4p_ref,
                     m2s_ref, m3s_ref, m4s_ref, wr_ref, br_ref, x1_ref, h2_ref, lg_ref, *, n_prompt):
    i = pl.program_id(0)
    y = jnp.dot(mg_ref[...], w_ref[...], preferred_element_type=F32)

    def finish(x_ref, m2_ref, m3_ref, m4_ref):
        x1 = x_ref[...] + m2_ref[...] * y.reshape(x_ref.shape)
        x1_ref[...] = x1.reshape(x1_ref.shape)
        h2 = _norm_mod(x1, g_ref[...], m3_ref[...], m4_ref[...]).reshape(x1_ref.shape)
        h2_ref[...] = h2
        lg_ref[...] = _router_logits(h2, wr_ref[...], br_ref[...])

    @pl.when(i < n_prompt)
    def _():
        finish(xp_ref, m2p_ref, m3p_ref, m4p_ref)

    @pl.when(i >= n_prompt)
    def _():
        finish(xs_ref, m2s_ref, m3s_ref, m4s_ref)


def _out_proj(merged, x_prompt, x_sample, w_out_b, g2, mp, ms, w_router, b_router):
    tl = _Tiles(256)
    xp, xs, mps, mss = tl.specs()
    row = pl.BlockSpec((tl.tm, D), lambda i: (i, 0))
    return pl.pallas_call(
        functools.partial(_out_proj_kernel, n_prompt=tl.n_prompt),
        out_shape=(jax.ShapeDtypeStruct((T, D), F32),
                   jax.ShapeDtypeStruct((T, D), F32),
                   jax.ShapeDtypeStruct((NE, T), F32)),
        grid=(tl.n,),
        in_specs=[row, xp, xs, pl.BlockSpec((D, D), lambda i: (0, 0)),
                  pl.BlockSpec((1, 1, D), lambda i: (0, 0, 0)), mps, mps, mps, mss, mss, mss,
                  pl.BlockSpec((NE, D), lambda i: (0, 0)),
                  pl.BlockSpec((NE, 1), lambda i: (0, 0))],
        out_specs=(row, row, pl.BlockSpec((NE, tl.tm), lambda i: (0, i))),
        compiler_params=_cparams(("parallel",)),
        name="out_proj_norm2",
    )(merged, x_prompt, x_sample, w_out_b, g2.reshape(1, 1, D), mp[2], mp[3], mp[4], ms[2], ms[3], ms[4],
      w_router.T, b_router.reshape(NE, 1))


ROUTE_TR = 512


def _router_kernel(lg_ref, tri_ref, idx_ref, gate_ref, rank_ref, cnt_ref, carry):
    @pl.when(pl.program_id(0) == 0)
    def _():
        carry[...] = jnp.zeros_like(carry)

    eid = lax.broadcasted_iota(jnp.int32, (NE, ROUTE_TR), 0).astype(F32)
    work = lg_ref[...]
    vals, ids = [], []
    onehot = jnp.zeros((NE, ROUTE_TR), F32)
    for _ in range(TOPK):
        m = jnp.max(work, axis=0, keepdims=True)
        sel = jnp.min(jnp.where(work == m, eid, float(NE)), axis=0, keepdims=True)
        hit = eid == sel
        work = jnp.where(hit, -jnp.inf, work)
        onehot = jnp.where(hit, 1.0, onehot)
        vals.append(m)
        ids.append(sel)
    es = [jnp.exp(v - vals[0]) for v in vals]
    den = es[0] + es[1] + es[2] + es[3]
    before = jnp.dot(onehot.astype(BF16), tri_ref[...], preferred_element_type=F32) + carry[:, 0:1]
    for k in range(TOPK):
        idx_ref[k:k + 1, :] = ids[k].astype(jnp.int32)
        gate_ref[k:k + 1, :] = es[k] / den
        rank_ref[k:k + 1, :] = jnp.sum(jnp.where(eid == ids[k], before, 0.0), axis=0,
                                       keepdims=True).astype(jnp.int32)
    carry[...] = carry[...] + jnp.sum(onehot, axis=1, keepdims=True)
    cnt_ref[...] = carry[...]


def _router(logits):
    tri = jnp.triu(jnp.ones((ROUTE_TR, ROUTE_TR), F32), 1).astype(BF16)
    row = lambda i: (0, i)
    return pl.pallas_call(
        _router_kernel,
        out_shape=(jax.ShapeDtypeStruct((TOPK, T), jnp.int32),
                   jax.ShapeDtypeStruct((TOPK, T), F32),
                   jax.ShapeDtypeStruct((TOPK, T), jnp.int32),
                   jax.ShapeDtypeStruct((NE, 128), F32)),
        grid=(T // ROUTE_TR,),
        in_specs=[pl.BlockSpec((NE, ROUTE_TR), row),
                  pl.BlockSpec((ROUTE_TR, ROUTE_TR), lambda i: (0, 0))],
        out_specs=(pl.BlockSpec((TOPK, ROUTE_TR), row),
                   pl.BlockSpec((TOPK, ROUTE_TR), row),
                   pl.BlockSpec((TOPK, ROUTE_TR), row),
                   pl.BlockSpec((NE, 128), lambda i: (0, 0))),
        scratch_shapes=[pltpu.VMEM((NE, 128), F32)],
        compiler_params=_cparams(("arbitrary",)),
        name="router_top4",
    )(logits, tri)


DISPATCH_ROWS = SUB
DISPATCH_STEPS = P_MAX // DISPATCH_ROWS
SCALAR_UNROLL = 8


def _dispatch_kernel(slot_ref, nt_ref, pad_lo_ref, pad_hi_ref, h_hbm, o_ref, tok_ref, buf, sem):
    i = pl.program_id(0)
    nt = nt_ref[0] * SUB // DISPATCH_ROWS

    @pl.when(i == 0)
    def _():
        for e in range(NE):
            def zero(j, c):
                tok_ref[j] = 0
                return c
            lax.fori_loop(pad_lo_ref[e], pad_hi_ref[e], zero, 0)
        for k in range(TOPK):
            def put(j, c, k=k):
                for u in range(SCALAR_UNROLL):
                    t = j * SCALAR_UNROLL + u
                    tok_ref[slot_ref[k * T + t]] = t
                return c
            lax.fori_loop(0, T // SCALAR_UNROLL, put, 0)

    def issue(step, slot):
        for r in range(DISPATCH_ROWS):
            pltpu.make_async_copy(h_hbm.at[pl.ds(tok_ref[step * DISPATCH_ROWS + r], 1)],
                                  buf.at[slot, pl.ds(r, 1)], sem.at[slot]).start()

    @pl.when(i == 0)
    def _():
        issue(0, 0)

    @pl.when(i + 1 < nt)
    def _():
        issue(i + 1, (i + 1) % 2)

    @pl.when(i < nt)
    def _():
        slot = i % 2
        pltpu.make_async_copy(h_hbm.at[pl.ds(0, DISPATCH_ROWS)], buf.at[slot], sem.at[slot]).wait()
        o_ref[...] = buf[slot].astype(BF16)

    @pl.when(i >= nt)
    def _():
        o_ref[...] = jnp.zeros_like(o_ref)


def _dispatch(slots, nt_used, pad_lo, pad_hi, h2):
    return pl.pallas_call(
        _dispatch_kernel,
        out_shape=jax.ShapeDtypeStruct((P_MAX, D), BF16),
        grid_spec=pltpu.PrefetchScalarGridSpec(
            num_scalar_prefetch=4,
            grid=(DISPATCH_STEPS,),
            in_specs=[pl.BlockSpec(memory_space=pl.ANY)],
            out_specs=pl.BlockSpec((DISPATCH_ROWS, D), lambda i, *_: (i, 0)),
            scratch_shapes=[pltpu.SMEM((P_MAX,), jnp.int32),
                            pltpu.VMEM((2, DISPATCH_ROWS, D), F32),
                            pltpu.SemaphoreType.DMA((2,))]),
        compiler_params=_cparams(("arbitrary",)),
        name="moe_dispatch",
    )(slots, nt_used, pad_lo, pad_hi, h2)


def _moe_kernel(ge_ref, gs_ref, gn_ref, nt_ref, w1_ref, b1_ref, w2_ref, b2_ref, sel_ref, xs_hbm, y_hbm,
                x_buf, acc, h_scr, sem_in, sem_out):
    del ge_ref
    g = pl.program_id(0)
    c = pl.program_id(1)
    nsub = gn_ref[g]
    start = gs_ref[g]

    def rows_of(r):
        return pl.ds(pl.multiple_of(r * SUB, SUB), SUB)

    def x_copy(r):
        return pltpu.make_async_copy(xs_hbm.at[pl.ds((start + r) * SUB, SUB)], x_buf.at[rows_of(r)],
                                     sem_in.at[r])

    def y_copy(r):
        return pltpu.make_async_copy(acc.at[rows_of(r)], y_hbm.at[pl.ds((start + r) * SUB, SUB)], sem_out)

    def for_subtiles(fn, n):
        def body(r, carry):
            fn(r)
            return carry
        lax.fori_loop(0, n, body, 0)

    def up_proj(r):
        return (jnp.dot(x_buf[rows_of(r), :], w1_ref[0].astype(BF16), preferred_element_type=F32)
                + b1_ref[0])

    def down_proj(r, h):
        hn = jnp.concatenate(
            [pltpu.roll(h[:, k * 128:(k + 1) * 128], 127, 1) for k in range(2 * TF // 128)], axis=1)
        glu = jnp.minimum(h, LIMIT)
        lin = jnp.clip(hn, -LIMIT, LIMIT)
        act = (glu * _sigmoid(ALPHA * glu) * (lin + 1.0)).astype(BF16)
        sel = sel_ref[...]
        act = jnp.concatenate(
            [jnp.dot(act[:, k * MXU_N:(k + 1) * MXU_N], sel, preferred_element_type=F32)
             for k in range(2 * TF // MXU_N)], axis=1).astype(BF16)
        rows = rows_of(r)
        acc[rows, :] = acc[rows, :] + jnp.dot(act, w2_ref[0].astype(BF16), preferred_element_type=F32)

    @pl.when(nsub > 0)
    def _():
        @pl.when(c == 0)
        def _():
            for_subtiles(lambda r: x_copy(r).start(), nsub)

            def init(r):
                acc[rows_of(r), :] = jnp.broadcast_to(b2_ref[0], (SUB, D))
            for_subtiles(init, nsub)
            x_copy(0).wait()

        h_scr[0] = up_proj(0)

        def step(r):
            @pl.when(c == 0)
            def _():
                x_copy(r + 1).wait()

            h = h_scr[r % 2]
            h_scr[(r + 1) % 2] = up_proj(r + 1)
            down_proj(r, h)

            @pl.when(c == NC - 1)
            def _():
                y_copy(r).start()
        for_subtiles(step, nsub - 1)
        down_proj(nsub - 1, h_scr[(nsub - 1) % 2])

        @pl.when(c == NC - 1)
        def _():
            y_copy(nsub - 1).start()
            for_subtiles(lambda r: y_copy(r).wait(), nsub)

    @pl.when((g == G_MAX - 1) & (c == NC - 1))
    def _():
        acc[0:SUB, :] = jnp.zeros((SUB, D), F32)

        def fill(tile, carry):
            cp = pltpu.make_async_copy(acc.at[pl.ds(0, SUB)], y_hbm.at[pl.ds(tile * SUB, SUB)], sem_out)
            cp.start()
            cp.wait()
            return carry
        lax.fori_loop(nt_ref[0], NT_MAX, fill, 0)


def _moe(g_expert, g_start, g_nsub, nt_used, w1, b1, w2, b2, xs):
    sel = (jnp.arange(MXU_N)[:, None] == 2 * jnp.arange(MXU_N // 2)[None, :]).astype(BF16)

    def cidx(g, c, gn):
        return jnp.where(gn[g] > 0, c, NC - 1)

    return pl.pallas_call(
        _moe_kernel,
        out_shape=jax.ShapeDtypeStruct((P_MAX, D), F32),
        grid_spec=pltpu.PrefetchScalarGridSpec(
            num_scalar_prefetch=4,
            grid=(G_MAX, NC),
            in_specs=[pl.BlockSpec((1, D, 2 * TF), lambda g, c, ge, gs, gn, nt: (ge[g], 0, cidx(g, c, gn))),
                      pl.BlockSpec((1, 1, 2 * TF), lambda g, c, ge, gs, gn, nt: (ge[g], 0, cidx(g, c, gn))),
                      pl.BlockSpec((1, TF, D), lambda g, c, ge, gs, gn, nt: (ge[g], cidx(g, c, gn), 0)),
                      pl.BlockSpec((1, 1, D), lambda g, c, ge, gs, gn, nt: (ge[g], 0, 0)),
                      pl.BlockSpec((MXU_N, MXU_N // 2), lambda g, c, ge, gs, gn, nt: (0, 0)),
                      pl.BlockSpec(memory_space=pl.ANY)],
            out_specs=pl.BlockSpec(memory_space=pl.ANY),
            scratch_shapes=[pltpu.VMEM((GSUB * SUB, D), BF16),
                            pltpu.VMEM((GSUB * SUB, D), F32),
                            pltpu.VMEM((2, SUB, 2 * TF), F32),
                            pltpu.SemaphoreType.DMA((GSUB,)),
                            pltpu.SemaphoreType.DMA]),
        compiler_params=_cparams(("arbitrary", "arbitrary")),
        name="moe_experts",
    )(g_expert, g_start, g_nsub, nt_used, w1, b1.reshape(NE, 1, 2 * DFF), w2, b2.reshape(NE, 1, D), sel, xs)


COMB_TM = 128


def _combine_kernel(slot_ref, x1_ref, gate_ref, m5_ref, g_ref, y_hbm, o_ref, buf, sem, *, tok0):
    i = pl.program_id(0)
    n = pl.num_programs(0)

    def issue(tile, slot):
        for k in range(TOPK):
            for r in range(COMB_TM):
                s = slot_ref[k * T + tok0 + tile * COMB_TM + r]
                pltpu.make_async_copy(y_hbm.at[pl.ds(s, 1)], buf.at[slot, k, pl.ds(r, 1)],
                                      sem.at[slot]).start()

    @pl.when(i == 0)
    def _():
        issue(0, 0)

    @pl.when(i + 1 < n)
    def _():
        issue(i + 1, (i + 1) % 2)

    slot = i % 2
    for k in range(TOPK):
        pltpu.make_async_copy(y_hbm.at[pl.ds(0, COMB_TM)], buf.at[slot, k], sem.at[slot]).wait()
    gates = gate_ref[...]
    f = None
    for k in range(TOPK):
        term = gates[:, k:k + 1] * buf[slot, k]
        f = term if f is None else f + term
    x2 = x1_ref[...].reshape(o_ref.shape) + m5_ref[...] * f.reshape(o_ref.shape)
    ms = jnp.mean(x2 * x2, axis=-1, keepdims=True)
    o_ref[...] = x2 * lax.rsqrt(ms + EPS) * g_ref[...]


def _combine(prompt, slots, x1, gates_t, m5, gf, y):
    tm = COMB_TM
    if prompt:
        nb, r, steps, rb0, shape3 = 1, tm, TP // tm, 0, (BP, SP, D)
        per_b = SP // tm
        x_index = lambda i, s: (i // per_b, i % per_b, 0)
        m_index = lambda i, s: (i // per_b, 0, 0)
    else:
        nb, r, steps, rb0, shape3 = tm // SS, SS, TS // tm, TP // tm, (BS, SS, D)
        x_index = lambda i, s: (i, 0, 0)
        m_index = x_index
    return pl.pallas_call(
        functools.partial(_combine_kernel, tok0=rb0 * tm),
        out_shape=jax.ShapeDtypeStruct(shape3, F32),
        grid_spec=pltpu.PrefetchScalarGridSpec(
            num_scalar_prefetch=1,
            grid=(steps,),
            in_specs=[pl.BlockSpec((tm, D), lambda i, s: (rb0 + i, 0)),
                      pl.BlockSpec((tm, TOPK), lambda i, s: (rb0 + i, 0)),
                      pl.BlockSpec((nb, 1, D), m_index),
                      pl.BlockSpec((1, 1, D), lambda i, s: (0, 0, 0)),
                      pl.BlockSpec(memory_space=pl.ANY)],
            out_specs=pl.BlockSpec((nb, r, D), x_index),
            scratch_shapes=[pltpu.VMEM((2, TOPK, tm, D), F32), pltpu.SemaphoreType.DMA((2,))]),
        compiler_params=_cparams(("arbitrary",)),
        name="moe_combine_norm",
    )(slots, x1, gates_t, m5, gf.reshape(1, 1, D), y)


def _routing_tables(idx_t, rank_t, counts):
    tiles = (counts + SUB - 1) // SUB
    tile_end = jnp.cumsum(tiles)
    tile_start = tile_end - tiles
    nt_used = tile_end[-1:]
    experts = jnp.arange(NE, dtype=jnp.int32)
    base = jnp.sum(jnp.where(idx_t[..., None] == experts, tile_start * SUB, 0), axis=-1)
    slot = base + rank_t
    ngrp = (tiles + GSUB - 1) // GSUB
    grp_end = jnp.cumsum(ngrp)
    grp_start = grp_end - ngrp
    n_groups = grp_end[-1]
    gid = jnp.arange(G_MAX, dtype=jnp.int32)
    gclamp = jnp.minimum(gid, jnp.maximum(n_groups - 1, 0))
    g_expert = jnp.sum(gclamp[:, None] >= grp_end[None, :], axis=-1).astype(jnp.int32)
    pick = lambda v: jnp.sum(jnp.where(g_expert[:, None] == experts[None, :], v[None, :], 0), axis=-1)
    local = gclamp - pick(grp_start)
    g_start = pick(tile_start) + local * GSUB
    g_nsub = jnp.where(gid < n_groups, jnp.minimum(GSUB, pick(tiles) - local * GSUB), 0)
    pad_lo = (tile_start * SUB + counts).astype(jnp.int32)
    pad_hi = (tile_end * SUB).astype(jnp.int32)
    return (slot.reshape(-1).astype(jnp.int32), nt_used.astype(jnp.int32), pad_lo, pad_hi,
            g_expert, g_start.astype(jnp.int32), g_nsub.astype(jnp.int32))


def kernel(x_prompt, x_sample, c_prompt, c_sample, cache_k, cache_v, state_conv, w_ada, b_ada, norm1_g,
           w_in, b_in, conv_dw_w, conv_dw_b, conv_ln_g, conv_ln_b, w_conv_out, b_conv_out, sinks,
           w_attn_out, b_attn_out, w_out, norm2_g, w_router, b_router, w_mlp1, b_mlp1, w_mlp2, b_mlp2,
           norm_f_g):
    mods = _modulations(jnp.concatenate([c_prompt, c_sample], axis=0), w_ada, b_ada)
    mp = [mods[:BP, k * D:(k + 1) * D].reshape(BP, 1, D) for k in range(6)]
    ms = [mods[BP:, k * D:(k + 1) * D].reshape(BS, 1, D) for k in range(6)]

    h1 = _norm1(x_prompt, x_sample, norm1_g, mp, ms)
    w_in_b = w_in
    b_in2 = b_in.reshape(1, IN_W)
    u_all = _in_proj(h1, w_in_b, b_in2, OFF_GLU_A, OFF_GLU_B, D, "glu", F32)
    q_all = _in_proj(h1, w_in_b, b_in2, OFF_Q, None, D, "plain", BF16, HD ** -0.5)
    kv_all = _in_proj(h1, w_in_b, b_in2, OFF_KV, None, 2 * KVW, "plain", F32)
    gates = _in_proj(h1, w_in_b, b_in2, OFF_GC, None, 2 * D, "sigmoid", F32)

    zs = _conv_prompt(u_all, conv_dw_w, conv_dw_b, conv_ln_g, conv_ln_b)
    u_s_t = u_all[TP:].reshape(BS, SS, D).transpose(1, 0, 2)
    zs_s_t, conv_s_t = _conv_sample(state_conv.transpose(1, 0, 2), u_s_t, conv_dw_w, conv_dw_b,
                                    conv_ln_g, conv_ln_b)
    zs = lax.dynamic_update_slice(zs, zs_s_t.transpose(1, 0, 2).reshape(TS, D), (TP, 0))
    conv_s = conv_s_t.transpose(1, 0, 2)
    conv_p = jnp.stack([u_all[(b + 1) * SP - (CW - 1):(b + 1) * SP] for b in range(BP)])

    attn = _attn_prompt(sinks, q_all, kv_all)
    q_g = (q_all[TP:].reshape(BS, SS, NKV, GRP, HD).transpose(0, 2, 3, 1, 4)
           .reshape(BS, NKV, QG, HD))
    kv_s = kv_all[TP:].reshape(BS, SS, 2 * KVW)
    attn_s = _attn_sample(sinks, q_g, kv_s, cache_k.transpose(0, 2, 3, 1), cache_v.transpose(0, 2, 3, 1))
    attn_s = (attn_s.reshape(BS, NKV, GRP, SS, HD).transpose(0, 3, 1, 2, 4).reshape(TS, D))
    attn = lax.dynamic_update_slice(attn, attn_s, (TP, 0))
    kv_tail = jnp.stack([kv_all[(b + 1) * SP - WIN:(b + 1) * SP] for b in range(BP)])
    k_p = kv_tail[:, :, :KVW].reshape(BP, WIN, NKV, HD)
    v_p = kv_tail[:, :, KVW:].reshape(BP, WIN, NKV, HD)
    k_s = jnp.concatenate([cache_k[:, SS:], kv_s[:, :, :KVW].reshape(BS, SS, NKV, HD)], axis=1)
    v_s = jnp.concatenate([cache_v[:, SS:], kv_s[:, :, KVW:].reshape(BS, SS, NKV, HD)], axis=1)

    merged = _merge(zs, attn, w_conv_out, w_attn_out, b_conv_out, b_attn_out,
                    gates)
    x1, h2, logits = _out_proj(merged, x_prompt, x_sample, w_out.astype(BF16), norm2_g, mp, ms,
                               w_router, b_router)

    idx_t, gate_t, rank_t, cnt = _router(logits)
    counts = cnt[:, 0].astype(jnp.int32)
    slots, nt_used, pad_lo, pad_hi, g_expert, g_start, g_nsub = _routing_tables(idx_t, rank_t, counts)

    xs = _dispatch(slots, nt_used, pad_lo, pad_hi, h2)
    y = _moe(g_expert, g_start, g_nsub, nt_used, w_mlp1, b_mlp1, w_mlp2, b_mlp2, xs)

    gates_tok = gate_t.T
    y_prompt = _combine(True, slots, x1, gates_tok, mp[5], norm_f_g, y)
    y_sample = _combine(False, slots, x1, gates_tok, ms[5], norm_f_g, y)
    return (y_prompt, y_sample, k_p, v_p, conv_p, k_s, v_s, conv_s)
```

```python
import functools

import jax
import jax.numpy as jnp
from jax import lax
from jax.experimental import pallas as pl
from jax.experimental.pallas import tpu as pltpu

F32 = jnp.float32
BF16 = jnp.bfloat16

D = 2048
BP, SP = 4, 2048
BS, SS = 128, 8
TP = BP * SP
TS = BS * SS
T = TP + TS
HD = 64
NH = 32
NKV = 4
GRP = NH // NKV
WIN = 128
KVW = NKV * HD
CW = 31
NE = 32
TOPK = 4
DFF = 2048
ALPHA = 1.702
LIMIT = 7.0
EPS = 1e-5
OFF_GLU_A, OFF_GLU_B, OFF_Q, OFF_KV, OFF_GC = 0, D, 2 * D, 3 * D, 3 * D + 2 * KVW
IN_W = 5 * D + 2 * KVW

VMEM_LIMIT = 56 * 1024 * 1024

SUB = 256
GSUB = 8
NT_MAX = (T * TOPK) // SUB + NE
P_MAX = NT_MAX * SUB
G_MAX = -(-((T * TOPK) // SUB) // GSUB) + NE
TF = 512
NC = DFF // TF
MXU_N = 256


def _cparams(sem):
    return pltpu.CompilerParams(dimension_semantics=sem, vmem_limit_bytes=VMEM_LIMIT)


def _sigmoid(x):
    return 1.0 / (1.0 + jnp.exp(-x))


def _norm_mod(x, g, shift, scale):
    ms = jnp.mean(x * x, axis=-1, keepdims=True)
    return (x * lax.rsqrt(ms + EPS) * g) * (1.0 + scale) + shift


class _Tiles:
    def __init__(self, tm):
        self.tm = tm
        self.per_b = SP // tm
        self.n_prompt = TP // tm
        self.n = T // tm
        self.sb = tm // SS

    def prompt_x(self, i):
        ip = jnp.minimum(i, self.n_prompt - 1)
        return (ip // self.per_b, ip % self.per_b, 0)

    def prompt_mod(self, i):
        return (jnp.minimum(i, self.n_prompt - 1) // self.per_b, 0, 0)

    def sample_x(self, i):
        return (jnp.maximum(i - self.n_prompt, 0), 0, 0)

    def specs(self, tail=()):
        wrap = lambda f: (lambda i, *_: f(i))
        xp = pl.BlockSpec((1, self.tm, D), wrap(self.prompt_x))
        xs = pl.BlockSpec((self.sb, SS, D), wrap(self.sample_x))
        mp = pl.BlockSpec((1, 1, D), wrap(self.prompt_mod))
        ms = pl.BlockSpec((self.sb, 1, D), wrap(self.sample_x))
        return xp, xs, mp, ms


def _mod_kernel(c_ref, w_ref, b_ref, o_ref):
    c = c_ref[...]
    sc = (c * _sigmoid(c)).astype(BF16)
    o_ref[...] = jnp.dot(sc, w_ref[...].astype(BF16), preferred_element_type=F32) + b_ref[...]


def _modulations(c_all, w_ada, b_ada):
    nb = c_all.shape[0]
    tn = 1024
    return pl.pallas_call(
        _mod_kernel,
        out_shape=jax.ShapeDtypeStruct((nb, 6 * D), F32),
        grid=(6 * D // tn,),
        in_specs=[pl.BlockSpec((nb, D), lambda j: (0, 0)),
                  pl.BlockSpec((D, tn), lambda j: (0, j)),
                  pl.BlockSpec((1, tn), lambda j: (0, j))],
        out_specs=pl.BlockSpec((nb, tn), lambda j: (0, j)),
        compiler_params=_cparams(("parallel",)),
        name="modulations",
    )(c_all, w_ada, b_ada.reshape(1, -1))


def _norm1_kernel(xp_ref, xs_ref, g_ref, shp_ref, scp_ref, shs_ref, scs_ref, o_ref, *, n_prompt):
    i = pl.program_id(0)

    @pl.when(i < n_prompt)
    def _():
        h = _norm_mod(xp_ref[...], g_ref[...], shp_ref[...], scp_ref[...])
        o_ref[...] = h.reshape(o_ref.shape).astype(BF16)

    @pl.when(i >= n_prompt)
    def _():
        h = _norm_mod(xs_ref[...], g_ref[...], shs_ref[...], scs_ref[...])
        o_ref[...] = h.reshape(o_ref.shape).astype(BF16)


def _norm1(x_prompt, x_sample, g, mp, ms):
    tl = _Tiles(256)
    xp, xs, mps, mss = tl.specs()
    return pl.pallas_call(
        functools.partial(_norm1_kernel, n_prompt=tl.n_prompt),
        out_shape=jax.ShapeDtypeStruct((T, D), BF16),
        grid=(tl.n,),
        in_specs=[xp, xs, pl.BlockSpec((1, 1, D), lambda i: (0, 0, 0)), mps, mps, mss, mss],
        out_specs=pl.BlockSpec((tl.tm, D), lambda i: (i, 0)),
        compiler_params=_cparams(("parallel",)),
        name="norm1_adaln",
    )(x_prompt, x_sample, g.reshape(1, 1, D), mp[0], mp[1], ms[0], ms[1])


def _in_proj_kernel(*refs, mode, out_scale):
    if mode == "glu":
        h_ref, wa_ref, wb_ref, ba_ref, bb_ref, o_ref = refs
    else:
        h_ref, wa_ref, ba_ref, o_ref = refs
    h = h_ref[...]
    a = jnp.dot(h, wa_ref[...].astype(BF16), preferred_element_type=F32) + ba_ref[...]
    if mode == "glu":
        b = jnp.dot(h, wb_ref[...].astype(BF16), preferred_element_type=F32) + bb_ref[...]
        a = a * _sigmoid(b)
    elif mode == "sigmoid":
        a = _sigmoid(a)
    elif out_scale != 1.0:
        a = a * out_scale
    o_ref[...] = a.astype(o_ref.dtype)


def _in_proj(h, w_b, b_in2, col_a, col_b, n_cols, mode, out_dtype, out_scale=1.0):
    tm, tn = (1536 if mode == "glu" else 3072), 512
    ca, cb = col_a // tn, (col_b // tn if col_b is not None else 0)
    in_specs = [pl.BlockSpec((tm, D), lambda i, j: (i, 0)),
                pl.BlockSpec((D, tn), lambda i, j: (0, ca + j))]
    args = [h, w_b]
    if mode == "glu":
        in_specs.append(pl.BlockSpec((D, tn), lambda i, j: (0, cb + j)))
        args.append(w_b)
    in_specs.append(pl.BlockSpec((1, tn), lambda i, j: (0, ca + j)))
    args.append(b_in2)
    if mode == "glu":
        in_specs.append(pl.BlockSpec((1, tn), lambda i, j: (0, cb + j)))
        args.append(b_in2)
    return pl.pallas_call(
        functools.partial(_in_proj_kernel, mode=mode, out_scale=out_scale),
        out_shape=jax.ShapeDtypeStruct((T, n_cols), out_dtype),
        grid=(T // tm, n_cols // tn),
        in_specs=in_specs,
        out_specs=pl.BlockSpec((tm, tn), lambda i, j: (i, j)),
        compiler_params=_cparams(("parallel", "arbitrary")),
        name="in_proj_" + mode,
    )(*args)


CONV_TL = 128
HALO = 32
CONV_STEPS = TP // CONV_TL


def _ln_swish(z, g, b):
    mu = jnp.mean(z, axis=-1, keepdims=True)
    d = z - mu
    var = jnp.mean(d * d, axis=-1, keepdims=True)
    zn = d * lax.rsqrt(var + EPS) * g + b
    return zn * _sigmoid(zn)


def _conv_prompt_kernel(u_ref, halo_ref, w_ref, dwb_ref, g_ref, b_ref, o_ref, win, z_scr):
    i = pl.program_id(0)

    @pl.when(i < CONV_STEPS)
    def _():
        first = (i % (SP // CONV_TL)) == 0
        base = HALO - (CW - 1)
        for s in range(D // 128):
            cs = slice(s * 128, (s + 1) * 128)
            win[s, 0:HALO, :] = jnp.where(first, 0.0, halo_ref[:, cs])
            win[s, HALO:HALO + CONV_TL, :] = u_ref[:, cs]
        for s in range(D // 128):
            cs = slice(s * 128, (s + 1) * 128)
            acc = None
            for rho in range(8):
                offs = [base + j for j in range(CW) if (base + j) % 8 == rho]
                seg = win[s, offs[0]:offs[-1] + CONV_TL, :]
                for o in offs:
                    j = o - base
                    term = seg[o - offs[0]:o - offs[0] + CONV_TL] * w_ref[j:j + 1, cs]
                    acc = term if acc is None else acc + term
            z_scr[:, cs] = acc + dwb_ref[:, cs]
        o_ref[...] = _ln_swish(z_scr[...], g_ref[...], b_ref[...]).astype(BF16)

    @pl.when(i >= CONV_STEPS)
    def _():
        o_ref[...] = jnp.zeros_like(o_ref)


def _conv_prompt(u_all, dw_w, dw_b, ln_g, ln_b):
    hb = CONV_TL // HALO
    last = CONV_STEPS - 1
    const = lambda i: (0, 0)
    return pl.pallas_call(
        _conv_prompt_kernel,
        out_shape=jax.ShapeDtypeStruct((T, D), BF16),
        grid=(T // CONV_TL,),
        in_specs=[pl.BlockSpec((CONV_TL, D), lambda i: (jnp.minimum(i, last), 0)),
                  pl.BlockSpec((HALO, D), lambda i: (jnp.maximum(jnp.minimum(i, last) * hb - 1, 0), 0)),
                  pl.BlockSpec((CW, D), const),
                  pl.BlockSpec((1, D), const),
                  pl.BlockSpec((1, D), const),
                  pl.BlockSpec((1, D), const)],
        out_specs=pl.BlockSpec((CONV_TL, D), lambda i: (i, 0)),
        scratch_shapes=[pltpu.VMEM((D // 128, HALO + CONV_TL, 128), F32), pltpu.VMEM((CONV_TL, D), F32)],
        compiler_params=_cparams(("parallel",)),
        name="conv_prompt",
    )(u_all, u_all, dw_w, dw_b.reshape(1, D), ln_g.reshape(1, D), ln_b.reshape(1, D))


CONV_BB = 16


def _conv_sample_kernel(st_ref, u_ref, w_ref, dwb_ref, g_ref, b_ref, o_ref, ns_ref, win, z_scr):
    win[0:CW - 1] = st_ref[...]
    win[CW - 1:CW - 1 + SS] = u_ref[...]
    ns_ref[0:CW - 1 - SS] = st_ref[SS:CW - 1]
    ns_ref[CW - 1 - SS:CW - 1] = u_ref[...]
    for s in range(D // 128):
        cs = slice(s * 128, (s + 1) * 128)
        acc = None
        for j in range(CW):
            term = win[j:j + SS, :, cs] * w_ref[j:j + 1, cs][None]
            acc = term if acc is None else acc + term
        z_scr[:, :, cs] = acc + dwb_ref[:, cs][None]
    o_ref[...] = _ln_swish(z_scr[...], g_ref[...][None], b_ref[...][None]).astype(BF16)


def _conv_sample(state_t, u_t, dw_w, dw_b, ln_g, ln_b):
    const = lambda i: (0, 0)
    batch = lambda i: (0, i, 0)
    return pl.pallas_call(
        _conv_sample_kernel,
        out_shape=(jax.ShapeDtypeStruct((SS, BS, D), BF16), jax.ShapeDtypeStruct((CW - 1, BS, D), F32)),
        grid=(BS // CONV_BB,),
        in_specs=[pl.BlockSpec((CW - 1, CONV_BB, D), batch),
                  pl.BlockSpec((SS, CONV_BB, D), batch),
                  pl.BlockSpec((CW, D), const),
                  pl.BlockSpec((1, D), const),
                  pl.BlockSpec((1, D), const),
                  pl.BlockSpec((1, D), const)],
        out_specs=(pl.BlockSpec((SS, CONV_BB, D), batch),
                   pl.BlockSpec((CW - 1, CONV_BB, D), batch)),
        scratch_shapes=[pltpu.VMEM((CW - 1 + SS, CONV_BB, D), F32), pltpu.VMEM((SS, CONV_BB, D), F32)],
        compiler_params=_cparams(("parallel",)),
        name="conv_sample",
    )(state_t, u_t, dw_w, dw_b.reshape(1, D), ln_g.reshape(1, D), ln_b.reshape(1, D))


ATT_STEPS = TP // WIN


def _attn_prompt_kernel(sink_ref, q_ref, kvc_ref, kvp_ref, o_ref):
    i = pl.program_id(0)
    n = i % (SP // WIN)

    def kv_head(kh):
        row = lax.broadcasted_iota(jnp.int32, (WIN, 2 * WIN), 0)
        col = lax.broadcasted_iota(jnp.int32, (WIN, 2 * WIN), 1)
        diff = row + WIN - col
        first_key = jnp.where(n > 0, 0, WIN)
        mask = (diff >= 0) & (diff < WIN) & (col >= first_key)
        ks = slice(kh * HD, (kh + 1) * HD)
        vs = slice(KVW + kh * HD, KVW + (kh + 1) * HD)
        k2 = jnp.concatenate([kvp_ref[:, ks], kvc_ref[:, ks]], axis=0).astype(BF16)
        v2 = jnp.concatenate([kvp_ref[:, vs], kvc_ref[:, vs]], axis=0).astype(BF16)
        outs = []
        for g in range(GRP):
            h = kh * GRP + g
            sink = sink_ref[h]
            qh = q_ref[:, h * HD:(h + 1) * HD]
            s = lax.dot_general(qh, k2, (((1,), (1,)), ((), ())), preferred_element_type=F32)
            s = jnp.where(mask, s, -jnp.inf)
            m = jnp.maximum(jnp.max(s, axis=-1, keepdims=True), sink)
            p = jnp.exp(s - m)
            den = jnp.sum(p, axis=-1, keepdims=True) + jnp.exp(sink - m)
            o = jnp.dot(p.astype(BF16), v2, preferred_element_type=F32)
            outs.append(o / den)
        o_ref[:, kh * GRP * HD:(kh + 1) * GRP * HD] = jnp.concatenate(outs, axis=-1).astype(BF16)

    for kh in range(NKV):
        pl.when(i < ATT_STEPS)(functools.partial(kv_head, kh))

    @pl.when(i >= ATT_STEPS)
    def _():
        o_ref[...] = jnp.zeros_like(o_ref)


def _attn_prompt(sinks, q_all, kv_all):
    last = ATT_STEPS - 1
    nblk = SP // WIN

    def prev_block(i):
        ic = jnp.minimum(i, last)
        return (jnp.where(ic % nblk == 0, ic, ic - 1), 0)

    return pl.pallas_call(
        _attn_prompt_kernel,
        out_shape=jax.ShapeDtypeStruct((T, D), BF16),
        grid=(T // WIN,),
        in_specs=[pl.BlockSpec(memory_space=pltpu.SMEM),
                  pl.BlockSpec((WIN, D), lambda i: (jnp.minimum(i, last), 0)),
                  pl.BlockSpec((WIN, 2 * KVW), lambda i: (jnp.minimum(i, last), 0)),
                  pl.BlockSpec((WIN, 2 * KVW), prev_block)],
        out_specs=pl.BlockSpec((WIN, D), lambda i: (i, 0)),
        compiler_params=_cparams(("parallel",)),
        name="attn_prompt",
    )(sinks, q_all, kv_all, kv_all)


ATT_BB = 8
QG = SS * GRP


def _attn_sample_kernel(sink_ref, q_ref, kvn_ref, ck_ref, cv_ref, o_ref):
    rr = lax.broadcasted_iota(jnp.int32, (QG, WIN), 0)
    qi_c = rr % SS
    key_c = lax.broadcasted_iota(jnp.int32, (QG, WIN), 1)
    mask_c = (key_c > qi_c)[None]
    rr_n = lax.broadcasted_iota(jnp.int32, (QG, SS), 0)
    key_n = lax.broadcasted_iota(jnp.int32, (QG, SS), 1)
    mask_n = (key_n <= rr_n % SS)[None]
    grow = lax.broadcasted_iota(jnp.int32, (QG, 1), 0) // SS
    for kh in range(NKV):
        sink = jnp.zeros((QG, 1), F32)
        for g in range(GRP):
            sink = jnp.where(grow == g, sink_ref[kh * GRP + g], sink)
        sink = sink[None]
        q = q_ref[:, kh]
        kc = ck_ref[:, kh].astype(BF16)
        vc = cv_ref[:, kh].astype(BF16)
        kn = kvn_ref[:, :, kh * HD:(kh + 1) * HD].astype(BF16)
        vn = kvn_ref[:, :, KVW + kh * HD:KVW + (kh + 1) * HD].astype(BF16)
        s1 = jnp.einsum("bqd,bds->bqs", q, kc, preferred_element_type=F32)
        s2 = jnp.einsum("bqd,bsd->bqs", q, kn, preferred_element_type=F32)
        s1 = jnp.where(mask_c, s1, -jnp.inf)
        s2 = jnp.where(mask_n, s2, -jnp.inf)
        m = jnp.maximum(jnp.maximum(jnp.max(s1, axis=-1, keepdims=True),
                                    jnp.max(s2, axis=-1, keepdims=True)), sink)
        p1 = jnp.exp(s1 - m)
        p2 = jnp.exp(s2 - m)
        den = (jnp.sum(p1, axis=-1, keepdims=True) + jnp.sum(p2, axis=-1, keepdims=True)
               + jnp.exp(sink - m))
        o = (jnp.einsum("bqs,bds->bqd", p1.astype(BF16), vc, preferred_element_type=F32)
             + jnp.einsum("bqs,bsd->bqd", p2.astype(BF16), vn, preferred_element_type=F32))
        o_ref[:, kh] = (o / den).astype(BF16)


def _attn_sample(sinks, q_g, kv_new, ck, cv):
    return pl.pallas_call(
        _attn_sample_kernel,
        out_shape=jax.ShapeDtypeStruct((BS, NKV, QG, HD), BF16),
        grid=(BS // ATT_BB,),
        in_specs=[pl.BlockSpec(memory_space=pltpu.SMEM),
                  pl.BlockSpec((ATT_BB, NKV, QG, HD), lambda i: (i, 0, 0, 0)),
                  pl.BlockSpec((ATT_BB, SS, 2 * KVW), lambda i: (i, 0, 0)),
                  pl.BlockSpec((ATT_BB, NKV, HD, WIN), lambda i: (i, 0, 0, 0)),
                  pl.BlockSpec((ATT_BB, NKV, HD, WIN), lambda i: (i, 0, 0, 0))],
        out_specs=pl.BlockSpec((ATT_BB, NKV, QG, HD), lambda i: (i, 0, 0, 0)),
        compiler_params=_cparams(("parallel",)),
        name="attn_sample",
    )(sinks, q_g, kv_new, ck, cv)


def _merge_kernel(zs_ref, at_ref, wc_ref, wa_ref, bc_ref, ba_ref, gc_ref, ga_ref, o_ref):
    conv = jnp.dot(zs_ref[...], wc_ref[...].astype(BF16), preferred_element_type=F32) + bc_ref[...]
    attn = jnp.dot(at_ref[...], wa_ref[...].astype(BF16), preferred_element_type=F32) + ba_ref[...]
    o_ref[...] = (gc_ref[...] * conv + ga_ref[...] * attn).astype(BF16)


def _merge(zs, attn, w_co, w_ao, b_co, b_ao, gates):
    tm, tn = 1024, 512
    ga_off = D // tn
    return pl.pallas_call(
        _merge_kernel,
        out_shape=jax.ShapeDtypeStruct((T, D), BF16),
        grid=(T // tm, D // tn),
        in_specs=[pl.BlockSpec((tm, D), lambda i, j: (i, 0)),
                  pl.BlockSpec((tm, D), lambda i, j: (i, 0)),
                  pl.BlockSpec((D, tn), lambda i, j: (0, j)),
                  pl.BlockSpec((D, tn), lambda i, j: (0, j)),
                  pl.BlockSpec((1, tn), lambda i, j: (0, j)),
                  pl.BlockSpec((1, tn), lambda i, j: (0, j)),
                  pl.BlockSpec((tm, tn), lambda i, j: (i, j)),
                  pl.BlockSpec((tm, tn), lambda i, j: (i, ga_off + j))],
        out_specs=pl.BlockSpec((tm, tn), lambda i, j: (i, j)),
        compiler_params=_cparams(("parallel", "arbitrary")),
        name="mixer_merge",
    )(zs, attn, w_co, w_ao, b_co.reshape(1, D), b_ao.reshape(1, D), gates, gates)


def _router_logits(h, w, b):
    h_hi = h.astype(BF16)
    h_lo = (h - h_hi.astype(F32)).astype(BF16)
    w_hi = w.astype(BF16)
    w_lo = (w - w_hi.astype(F32)).astype(BF16)
    dn = (((1,), (1,)), ((), ()))
    return (lax.dot_general(w_hi, h_hi, dn, preferred_element_type=F32)
            + lax.dot_general(w_hi, h_lo, dn, preferred_element_type=F32)
            + lax.dot_general(w_lo, h_hi, dn, preferred_element_type=F32)) + b


def _out_proj_kernel(mg_ref, xp_ref, xs_ref, w_ref, g_ref, m2p_ref, m3p_ref, m4p_ref,
                     m2s_ref, m3s_ref, m4s_ref, wr_ref, br_ref, x1_ref, h2_ref, lg_ref, *, n_prompt):
    i = pl.program_id(0)
    y = jnp.dot(mg_ref[...], w_ref[...], preferred_element_type=F32)

    def finish(x_ref, m2_ref, m3_ref, m4_ref):
        x1 = x_ref[...] + m2_ref[...] * y.reshape(x_ref.shape)
        x1_ref[...] = x1.reshape(x1_ref.shape)
        h2 = _norm_mod(x1, g_ref[...], m3_ref[...], m4_ref[...]).reshape(x1_ref.shape)
        h2_ref[...] = h2
        lg_ref[...] = _router_logits(h2, wr_ref[...], br_ref[...])

    @pl.when(i < n_prompt)
    def _():
        finish(xp_ref, m2p_ref, m3p_ref, m4p_ref)

    @pl.when(i >= n_prompt)
    def _():
        finish(xs_ref, m2s_ref, m3s_ref, m4s_ref)


def _out_proj(merged, x_prompt, x_sample, w_out_b, g2, mp, ms, w_router, b_router):
    tl = _Tiles(256)
    xp, xs, mps, mss = tl.specs()
    row = pl.BlockSpec((tl.tm, D), lambda i: (i, 0))
    return pl.pallas_call(
        functools.partial(_out_proj_kernel, n_prompt=tl.n_prompt),
        out_shape=(jax.ShapeDtypeStruct((T, D), F32),
                   jax.ShapeDtypeStruct((T, D), F32),
                   jax.ShapeDtypeStruct((NE, T), F32)),
        grid=(tl.n,),
        in_specs=[row, xp, xs, pl.BlockSpec((D, D), lambda i: (0, 0)),
                  pl.BlockSpec((1, 1, D), lambda i: (0, 0, 0)), mps, mps, mps, mss, mss, mss,
                  pl.BlockSpec((NE, D), lambda i: (0, 0)),
                  pl.BlockSpec((NE, 1), lambda i: (0, 0))],
        out_specs=(row, row, pl.BlockSpec((NE, tl.tm), lambda i: (0, i))),
        compiler_params=_cparams(("parallel",)),
        name="out_proj_norm2",
    )(merged, x_prompt, x_sample, w_out_b, g2.reshape(1, 1, D), mp[2], mp[3], mp[4], ms[2], ms[3], ms[4],
      w_router.T, b_router.reshape(NE, 1))


ROUTE_TR = 512


def _router_kernel(lg_ref, tri_ref, idx_ref, gate_ref, rank_ref, cnt_ref, carry):
    @pl.when(pl.program_id(0) == 0)
    def _():
        carry[...] = jnp.zeros_like(carry)

    eid = lax.broadcasted_iota(jnp.int32, (NE, ROUTE_TR), 0).astype(F32)
    work = lg_ref[...]
    vals, ids = [], []
    onehot = jnp.zeros((NE, ROUTE_TR), F32)
    for _ in range(TOPK):
        m = jnp.max(work, axis=0, keepdims=True)
        sel = jnp.min(jnp.where(work == m, eid, float(NE)), axis=0, keepdims=True)
        hit = eid == sel
        work = jnp.where(hit, -jnp.inf, work)
        onehot = jnp.where(hit, 1.0, onehot)
        vals.append(m)
        ids.append(sel)
    es = [jnp.exp(v - vals[0]) for v in vals]
    den = es[0] + es[1] + es[2] + es[3]
    before = jnp.dot(onehot.astype(BF16), tri_ref[...], preferred_element_type=F32) + carry[:, 0:1]
    for k in range(TOPK):
        idx_ref[k:k + 1, :] = ids[k].astype(jnp.int32)
        gate_ref[k:k + 1, :] = es[k] / den
        rank_ref[k:k + 1, :] = jnp.sum(jnp.where(eid == ids[k], before, 0.0), axis=0,
                                       keepdims=True).astype(jnp.int32)
    carry[...] = carry[...] + jnp.sum(onehot, axis=1, keepdims=True)
    cnt_ref[...] = carry[...]


def _router(logits):
    tri = jnp.triu(jnp.ones((ROUTE_TR, ROUTE_TR), F32), 1).astype(BF16)
    row = lambda i: (0, i)
    return pl.pallas_call(
        _router_kernel,
        out_shape=(jax.ShapeDtypeStruct((TOPK, T), jnp.int32),
                   jax.ShapeDtypeStruct((TOPK, T), F32),
                   jax.ShapeDtypeStruct((TOPK, T), jnp.int32),
                   jax.ShapeDtypeStruct((NE, 128), F32)),
        grid=(T // ROUTE_TR,),
        in_specs=[pl.BlockSpec((NE, ROUTE_TR), row),
                  pl.BlockSpec((ROUTE_TR, ROUTE_TR), lambda i: (0, 0))],
        out_specs=(pl.BlockSpec((TOPK, ROUTE_TR), row),
                   pl.BlockSpec((TOPK, ROUTE_TR), row),
                   pl.BlockSpec((TOPK, ROUTE_TR), row),
                   pl.BlockSpec((NE, 128), lambda i: (0, 0))),
        scratch_shapes=[pltpu.VMEM((NE, 128), F32)],
        compiler_params=_cparams(("arbitrary",)),
        name="router_top4",
    )(logits, tri)


DISPATCH_ROWS = SUB
DISPATCH_STEPS = P_MAX // DISPATCH_ROWS
SCALAR_UNROLL = 8


def _dispatch_kernel(slot_ref, nt_ref, pad_lo_ref, pad_hi_ref, h_hbm, o_ref, tok_ref, buf, sem):
    i = pl.program_id(0)
    nt = nt_ref[0] * SUB // DISPATCH_ROWS

    @pl.when(i == 0)
    def _():
        for e in range(NE):
            def zero(j, c):
                tok_ref[j] = 0
                return c
            lax.fori_loop(pad_lo_ref[e], pad_hi_ref[e], zero, 0)
        for k in range(TOPK):
            def put(j, c, k=k):
                for u in range(SCALAR_UNROLL):
                    t = j * SCALAR_UNROLL + u
                    tok_ref[slot_ref[k * T + t]] = t
                return c
            lax.fori_loop(0, T // SCALAR_UNROLL, put, 0)

    def issue(step, slot):
        for r in range(DISPATCH_ROWS):
            pltpu.make_async_copy(h_hbm.at[pl.ds(tok_ref[step * DISPATCH_ROWS + r], 1)],
                                  buf.at[slot, pl.ds(r, 1)], sem.at[slot]).start()

    @pl.when(i == 0)
    def _():
        issue(0, 0)

    @pl.when(i + 1 < nt)
    def _():
        issue(i + 1, (i + 1) % 2)

    @pl.when(i < nt)
    def _():
        slot = i % 2
        pltpu.make_async_copy(h_hbm.at[pl.ds(0, DISPATCH_ROWS)], buf.at[slot], sem.at[slot]).wait()
        o_ref[...] = buf[slot].astype(BF16)

    @pl.when(i >= nt)
    def _():
        o_ref[...] = jnp.zeros_like(o_ref)


def _dispatch(slots, nt_used, pad_lo, pad_hi, h2):
    return pl.pallas_call(
        _dispatch_kernel,
        out_shape=jax.ShapeDtypeStruct((P_MAX, D), BF16),
        grid_spec=pltpu.PrefetchScalarGridSpec(
            num_scalar_prefetch=4,
            grid=(DISPATCH_STEPS,),
            in_specs=[pl.BlockSpec(memory_space=pl.ANY)],
            out_specs=pl.BlockSpec((DISPATCH_ROWS, D), lambda i, *_: (i, 0)),
            scratch_shapes=[pltpu.SMEM((P_MAX,), jnp.int32),
                            pltpu.VMEM((2, DISPATCH_ROWS, D), F32),
                            pltpu.SemaphoreType.DMA((2,))]),
        compiler_params=_cparams(("arbitrary",)),
        name="moe_dispatch",
    )(slots, nt_used, pad_lo, pad_hi, h2)


def _moe_kernel(ge_ref, gs_ref, gn_ref, nt_ref, w1_ref, b1_ref, w2_ref, b2_ref, sel_ref, xs_hbm, y_hbm,
                x_buf, acc, h_scr, sem_in, sem_out):
    del ge_ref
    g = pl.program_id(0)
    c = pl.program_id(1)
    nsub = gn_ref[g]
    start = gs_ref[g]

    def rows_of(r):
        return pl.ds(pl.multiple_of(r * SUB, SUB), SUB)

    def x_copy(r):
        return pltpu.make_async_copy(xs_hbm.at[pl.ds((start + r) * SUB, SUB)], x_buf.at[rows_of(r)],
                                     sem_in.at[r])

    def y_copy(r):
        return pltpu.make_async_copy(acc.at[rows_of(r)], y_hbm.at[pl.ds((start + r) * SUB, SUB)], sem_out)

    def for_subtiles(fn, n):
        def body(r, carry):
            fn(r)
            return carry
        lax.fori_loop(0, n, body, 0)

    def up_proj(r):
        return (jnp.dot(x_buf[rows_of(r), :], w1_ref[0].astype(BF16), preferred_element_type=F32)
                + b1_ref[0])

    def down_proj(r, h):
        hn = jnp.concatenate(
            [pltpu.roll(h[:, k * 128:(k + 1) * 128], 127, 1) for k in range(2 * TF // 128)], axis=1)
        glu = jnp.minimum(h, LIMIT)
        lin = jnp.clip(hn, -LIMIT, LIMIT)
        act = (glu * _sigmoid(ALPHA * glu) * (lin + 1.0)).astype(BF16)
        sel = sel_ref[...]
        act = jnp.concatenate(
            [jnp.dot(act[:, k * MXU_N:(k + 1) * MXU_N], sel, preferred_element_type=F32)
             for k in range(2 * TF // MXU_N)], axis=1).astype(BF16)
        rows = rows_of(r)
        acc[rows, :] = acc[rows, :] + jnp.dot(act, w2_ref[0].astype(BF16), preferred_element_type=F32)

    @pl.when(nsub > 0)
    def _():
        @pl.when(c == 0)
        def _():
            for_subtiles(lambda r: x_copy(r).start(), nsub)

            def init(r):
                acc[rows_of(r), :] = jnp.broadcast_to(b2_ref[0], (SUB, D))
            for_subtiles(init, nsub)
            x_copy(0).wait()

        h_scr[0] = up_proj(0)

        def step(r):
            @pl.when(c == 0)
            def _():
                x_copy(r + 1).wait()

            h = h_scr[r % 2]
            h_scr[(r + 1) % 2] = up_proj(r + 1)
            down_proj(r, h)

            @pl.when(c == NC - 1)
            def _():
                y_copy(r).start()
        for_subtiles(step, nsub - 1)
        down_proj(nsub - 1, h_scr[(nsub - 1) % 2])

        @pl.when(c == NC - 1)
        def _():
            y_copy(nsub - 1).start()
            for_subtiles(lambda r: y_copy(r).wait(), nsub)

    @pl.when((g == G_MAX - 1) & (c == NC - 1))
    def _():
        acc[0:SUB, :] = jnp.zeros((SUB, D), F32)

        def fill(tile, carry):
            cp = pltpu.make_async_copy(acc.at[pl.ds(0, SUB)], y_hbm.at[pl.ds(tile * SUB, SUB)], sem_out)
            cp.start()
            cp.wait()
            return carry
        lax.fori_loop(nt_ref[0], NT_MAX, fill, 0)


def _moe(g_expert, g_start, g_nsub, nt_used, w1, b1, w2, b2, xs):
    sel = (jnp.arange(MXU_N)[:, None] == 2 * jnp.arange(MXU_N // 2)[None, :]).astype(BF16)

    def cidx(g, c, gn):
        return jnp.where(gn[g] > 0, c, NC - 1)

    return pl.pallas_call(
        _moe_kernel,
        out_shape=jax.ShapeDtypeStruct((P_MAX, D), F32),
        grid_spec=pltpu.PrefetchScalarGridSpec(
            num_scalar_prefetch=4,
            grid=(G_MAX, NC),
            in_specs=[pl.BlockSpec((1, D, 2 * TF), lambda g, c, ge, gs, gn, nt: (ge[g], 0, cidx(g, c, gn))),
                      pl.BlockSpec((1, 1, 2 * TF), lambda g, c, ge, gs, gn, nt: (ge[g], 0, cidx(g, c, gn))),
                      pl.BlockSpec((1, TF, D), lambda g, c, ge, gs, gn, nt: (ge[g], cidx(g, c, gn), 0)),
                      pl.BlockSpec((1, 1, D), lambda g, c, ge, gs, gn, nt: (ge[g], 0, 0)),
                      pl.BlockSpec((MXU_N, MXU_N // 2), lambda g, c, ge, gs, gn, nt: (0, 0)),
                      pl.BlockSpec(memory_space=pl.ANY)],
            out_specs=pl.BlockSpec(memory_space=pl.ANY),
            scratch_shapes=[pltpu.VMEM((GSUB * SUB, D), BF16),
                            pltpu.VMEM((GSUB * SUB, D), F32),
                            pltpu.VMEM((2, SUB, 2 * TF), F32),
                            pltpu.SemaphoreType.DMA((GSUB,)),
                            pltpu.SemaphoreType.DMA]),
        compiler_params=_cparams(("arbitrary", "arbitrary")),
        name="moe_experts",
    )(g_expert, g_start, g_nsub, nt_used, w1, b1.reshape(NE, 1, 2 * DFF), w2, b2.reshape(NE, 1, D), sel, xs)


COMB_TM = 128


def _combine_kernel(slot_ref, x1_ref, gate_ref, m5_ref, g_ref, y_hbm, o_ref, buf, sem, *, tok0):
    i = pl.program_id(0)
    n = pl.num_programs(0)

    def issue(tile, slot):
        for k in range(TOPK):
            for r in range(COMB_TM):
                s = slot_ref[k * T + tok0 + tile * COMB_TM + r]
                pltpu.make_async_copy(y_hbm.at[pl.ds(s, 1)], buf.at[slot, k, pl.ds(r, 1)],
                                      sem.at[slot]).start()

    @pl.when(i == 0)
    def _():
        issue(0, 0)

    @pl.when(i + 1 < n)
    def _():
        issue(i + 1, (i + 1) % 2)

    slot = i % 2
    for k in range(TOPK):
        pltpu.make_async_copy(y_hbm.at[pl.ds(0, COMB_TM)], buf.at[slot, k], sem.at[slot]).wait()
    gates = gate_ref[...]
    f = None
    for k in range(TOPK):
        term = gates[:, k:k + 1] * buf[slot, k]
        f = term if f is None else f + term
    x2 = x1_ref[...].reshape(o_ref.shape) + m5_ref[...] * f.reshape(o_ref.shape)
    ms = jnp.mean(x2 * x2, axis=-1, keepdims=True)
    o_ref[...] = x2 * lax.rsqrt(ms + EPS) * g_ref[...]


def _combine(prompt, slots, x1, gates_t, m5, gf, y):
    tm = COMB_TM
    if prompt:
        nb, r, steps, rb0, shape3 = 1, tm, TP // tm, 0, (BP, SP, D)
        per_b = SP // tm
        x_index = lambda i, s: (i // per_b, i % per_b, 0)
        m_index = lambda i, s: (i // per_b, 0, 0)
    else:
        nb, r, steps, rb0, shape3 = tm // SS, SS, TS // tm, TP // tm, (BS, SS, D)
        x_index = lambda i, s: (i, 0, 0)
        m_index = x_index
    return pl.pallas_call(
        functools.partial(_combine_kernel, tok0=rb0 * tm),
        out_shape=jax.ShapeDtypeStruct(shape3, F32),
        grid_spec=pltpu.PrefetchScalarGridSpec(
            num_scalar_prefetch=1,
            grid=(steps,),
            in_specs=[pl.BlockSpec((tm, D), lambda i, s: (rb0 + i, 0)),
                      pl.BlockSpec((tm, TOPK), lambda i, s: (rb0 + i, 0)),
                      pl.BlockSpec((nb, 1, D), m_index),
                      pl.BlockSpec((1, 1, D), lambda i, s: (0, 0, 0)),
                      pl.BlockSpec(memory_space=pl.ANY)],
            out_specs=pl.BlockSpec((nb, r, D), x_index),
            scratch_shapes=[pltpu.VMEM((2, TOPK, tm, D), F32), pltpu.SemaphoreType.DMA((2,))]),
        compiler_params=_cparams(("arbitrary",)),
        name="moe_combine_norm",
    )(slots, x1, gates_t, m5, gf.reshape(1, 1, D), y)


def _routing_tables(idx_t, rank_t, counts):
    tiles = (counts + SUB - 1) // SUB
    tile_end = jnp.cumsum(tiles)
    tile_start = tile_end - tiles
    nt_used = tile_end[-1:]
    experts = jnp.arange(NE, dtype=jnp.int32)
    base = jnp.sum(jnp.where(idx_t[..., None] == experts, tile_start * SUB, 0), axis=-1)
    slot = base + rank_t
    ngrp = (tiles + GSUB - 1) // GSUB
    grp_end = jnp.cumsum(ngrp)
    grp_start = grp_end - ngrp
    n_groups = grp_end[-1]
    gid = jnp.arange(G_MAX, dtype=jnp.int32)
    gclamp = jnp.minimum(gid, jnp.maximum(n_groups - 1, 0))
    g_expert = jnp.sum(gclamp[:, None] >= grp_end[None, :], axis=-1).astype(jnp.int32)
    pick = lambda v: jnp.sum(jnp.where(g_expert[:, None] == experts[None, :], v[None, :], 0), axis=-1)
    local = gclamp - pick(grp_start)
    g_start = pick(tile_start) + local * GSUB
    g_nsub = jnp.where(gid < n_groups, jnp.minimum(GSUB, pick(tiles) - local * GSUB), 0)
    pad_lo = (tile_start * SUB + counts).astype(jnp.int32)
    pad_hi = (tile_end * SUB).astype(jnp.int32)
    return (slot.reshape(-1).astype(jnp.int32), nt_used.astype(jnp.int32), pad_lo, pad_hi,
            g_expert, g_start.astype(jnp.int32), g_nsub.astype(jnp.int32))


def kernel(x_prompt, x_sample, c_prompt, c_sample, cache_k, cache_v, state_conv, w_ada, b_ada, norm1_g,
           w_in, b_in, conv_dw_w, conv_dw_b, conv_ln_g, conv_ln_b, w_conv_out, b_conv_out, sinks,
           w_attn_out, b_attn_out, w_out, norm2_g, w_router, b_router, w_mlp1, b_mlp1, w_mlp2, b_mlp2,
           norm_f_g):
    mods = _modulations(jnp.concatenate([c_prompt, c_sample], axis=0), w_ada, b_ada)
    mp = [mods[:BP, k * D:(k + 1) * D].reshape(BP, 1, D) for k in range(6)]
    ms = [mods[BP:, k * D:(k + 1) * D].reshape(BS, 1, D) for k in range(6)]

    h1 = _norm1(x_prompt, x_sample, norm1_g, mp, ms)
    w_in_b = w_in
    b_in2 = b_in.reshape(1, IN_W)
    u_all = _in_proj(h1, w_in_b, b_in2, OFF_GLU_A, OFF_GLU_B, D, "glu", F32)
    q_all = _in_proj(h1, w_in_b, b_in2, OFF_Q, None, D, "plain", BF16, HD ** -0.5)
    kv_all = _in_proj(h1, w_in_b, b_in2, OFF_KV, None, 2 * KVW, "plain", F32)
    gates = _in_proj(h1, w_in_b, b_in2, OFF_GC, None, 2 * D, "sigmoid", F32)

    zs = _conv_prompt(u_all, conv_dw_w, conv_dw_b, conv_ln_g, conv_ln_b)
    u_s_t = u_all[TP:].reshape(BS, SS, D).transpose(1, 0, 2)
    zs_s_t, conv_s_t = _conv_sample(state_conv.transpose(1, 0, 2), u_s_t, conv_dw_w, conv_dw_b,
                                    conv_ln_g, conv_ln_b)
    zs = lax.dynamic_update_slice(zs, zs_s_t.transpose(1, 0, 2).reshape(TS, D), (TP, 0))
    conv_s = conv_s_t.transpose(1, 0, 2)
    conv_p = jnp.stack([u_all[(b + 1) * SP - (CW - 1):(b + 1) * SP] for b in range(BP)])

    attn = _attn_prompt(sinks, q_all, kv_all)
    q_g = (q_all[TP:].reshape(BS, SS, NKV, GRP, HD).transpose(0, 2, 3, 1, 4)
           .reshape(BS, NKV, QG, HD))
    kv_s = kv_all[TP:].reshape(BS, SS, 2 * KVW)
    attn_s = _attn_sample(sinks, q_g, kv_s, cache_k.transpose(0, 2, 3, 1), cache_v.transpose(0, 2, 3, 1))
    attn_s = (attn_s.reshape(BS, NKV, GRP, SS, HD).transpose(0, 3, 1, 2, 4).reshape(TS, D))
    attn = lax.dynamic_update_slice(attn, attn_s, (TP, 0))
    kv_tail = jnp.stack([kv_all[(b + 1) * SP - WIN:(b + 1) * SP] for b in range(BP)])
    k_p = kv_tail[:, :, :KVW].reshape(BP, WIN, NKV, HD)
    v_p = kv_tail[:, :, KVW:].reshape(BP, WIN, NKV, HD)
    k_s = jnp.concatenate([cache_k[:, SS:], kv_s[:, :, :KVW].reshape(BS, SS, NKV, HD)], axis=1)
    v_s = jnp.concatenate([cache_v[:, SS:], kv_s[:, :, KVW:].reshape(BS, SS, NKV, HD)], axis=1)

    merged = _merge(zs, attn, w_conv_out, w_attn_out, b_conv_out, b_attn_out,
                    gates)
    x1, h2, logits = _out_proj(merged, x_prompt, x_sample, w_out.astype(BF16), norm2_g, mp, ms,
                               w_router, b_router)

    idx_t, gate_t, rank_t, cnt = _router(logits)
    counts = cnt[:, 0].astype(jnp.int32)
    slots, nt_used, pad_lo, pad_hi, g_expert, g_start, g_nsub = _routing_tables(idx_t, rank_t, counts)

    xs = _dispatch(slots, nt_used, pad_lo, pad_hi, h2)
    y = _moe(g_expert, g_start, g_nsub, nt_used, w_mlp1, b_mlp1, w_mlp2, b_mlp2, xs)

    gates_tok = gate_t.T
    y_prompt = _combine(True, slots, x1, gates_tok, mp[5], norm_f_g, y)
    y_sample = _combine(False, slots, x1, gates_tok, ms[5], norm_f_g, y)
    return (y_prompt, y_sample, k_p, v_p, conv_p, k_s, v_s, conv_s)
```

```python
import functools

import jax
import jax.numpy as jnp
from jax import lax
from jax.experimental import pallas as pl
from jax.experimental.pallas import tpu as pltpu

F32 = jnp.float32
BF16 = jnp.bfloat16

D = 2048
BP, SP = 4, 2048
BS, SS = 128, 8
TP = BP * SP
TS = BS * SS
T = TP + TS
HD = 64
NH = 32
NKV = 4
GRP = NH // NKV
WIN = 128
KVW = NKV * HD
CW = 31
NE = 32
TOPK = 4
DFF = 2048
ALPHA = 1.702
LIMIT = 7.0
EPS = 1e-5
OFF_GLU_A, OFF_GLU_B, OFF_Q, OFF_KV, OFF_GC = 0, D, 2 * D, 3 * D, 3 * D + 2 * KVW
IN_W = 5 * D + 2 * KVW

VMEM_LIMIT = 56 * 1024 * 1024

SUB = 256
GSUB = 8
NT_MAX = (T * TOPK) // SUB + NE
P_MAX = NT_MAX * SUB
G_MAX = -(-((T * TOPK) // SUB) // GSUB) + NE
TF = 512
NC = DFF // TF
MXU_N = 256


def _cparams(sem):
    return pltpu.CompilerParams(dimension_semantics=sem, vmem_limit_bytes=VMEM_LIMIT)


def _sigmoid(x):
    return 1.0 / (1.0 + jnp.exp(-x))


def _norm_mod(x, g, shift, scale):
    ms = jnp.mean(x * x, axis=-1, keepdims=True)
    return (x * lax.rsqrt(ms + EPS) * g) * (1.0 + scale) + shift


class _Tiles:
    def __init__(self, tm):
        self.tm = tm
        self.per_b = SP // tm
        self.n_prompt = TP // tm
        self.n = T // tm
        self.sb = tm // SS

    def prompt_x(self, i):
        ip = jnp.minimum(i, self.n_prompt - 1)
        return (ip // self.per_b, ip % self.per_b, 0)

    def prompt_mod(self, i):
        return (jnp.minimum(i, self.n_prompt - 1) // self.per_b, 0, 0)

    def sample_x(self, i):
        return (jnp.maximum(i - self.n_prompt, 0), 0, 0)

    def specs(self, tail=()):
        wrap = lambda f: (lambda i, *_: f(i))
        xp = pl.BlockSpec((1, self.tm, D), wrap(self.prompt_x))
        xs = pl.BlockSpec((self.sb, SS, D), wrap(self.sample_x))
        mp = pl.BlockSpec((1, 1, D), wrap(self.prompt_mod))
        ms = pl.BlockSpec((self.sb, 1, D), wrap(self.sample_x))
        return xp, xs, mp, ms


def _mod_kernel(c_ref, w_ref, b_ref, o_ref):
    c = c_ref[...]
    sc = (c * _sigmoid(c)).astype(BF16)
    o_ref[...] = jnp.dot(sc, w_ref[...].astype(BF16), preferred_element_type=F32) + b_ref[...]


def _modulations(c_all, w_ada, b_ada):
    nb = c_all.shape[0]
    tn = 1024
    return pl.pallas_call(
        _mod_kernel,
        out_shape=jax.ShapeDtypeStruct((nb, 6 * D), F32),
        grid=(6 * D // tn,),
        in_specs=[pl.BlockSpec((nb, D), lambda j: (0, 0)),
                  pl.BlockSpec((D, tn), lambda j: (0, j)),
                  pl.BlockSpec((1, tn), lambda j: (0, j))],
        out_specs=pl.BlockSpec((nb, tn), lambda j: (0, j)),
        compiler_params=_cparams(("parallel",)),
        name="modulations",
    )(c_all, w_ada, b_ada.reshape(1, -1))


def _norm1_kernel(xp_ref, xs_ref, g_ref, shp_ref, scp_ref, shs_ref, scs_ref, o_ref, *, n_prompt):
    i = pl.program_id(0)

    @pl.when(i < n_prompt)
    def _():
        h = _norm_mod(xp_ref[...], g_ref[...], shp_ref[...], scp_ref[...])
        o_ref[...] = h.reshape(o_ref.shape).astype(BF16)

    @pl.when(i >= n_prompt)
    def _():
        h = _norm_mod(xs_ref[...], g_ref[...], shs_ref[...], scs_ref[...])
        o_ref[...] = h.reshape(o_ref.shape).astype(BF16)


def _norm1(x_prompt, x_sample, g, mp, ms):
    tl = _Tiles(256)
    xp, xs, mps, mss = tl.specs()
    return pl.pallas_call(
        functools.partial(_norm1_kernel, n_prompt=tl.n_prompt),
        out_shape=jax.ShapeDtypeStruct((T, D), BF16),
        grid=(tl.n,),
        in_specs=[xp, xs, pl.BlockSpec((1, 1, D), lambda i: (0, 0, 0)), mps, mps, mss, mss],
        out_specs=pl.BlockSpec((tl.tm, D), lambda i: (i, 0)),
        compiler_params=_cparams(("parallel",)),
        name="norm1_adaln",
    )(x_prompt, x_sample, g.reshape(1, 1, D), mp[0], mp[1], ms[0], ms[1])


def _in_proj_kernel(*refs, mode, out_scale):
    if mode == "glu":
        h_ref, wa_ref, wb_ref, ba_ref, bb_ref, o_ref = refs
    else:
        h_ref, wa_ref, ba_ref, o_ref = refs
    h = h_ref[...]
    a = jnp.dot(h, wa_ref[...].astype(BF16), preferred_element_type=F32) + ba_ref[...]
    if mode == "glu":
        b = jnp.dot(h, wb_ref[...].astype(BF16), preferred_element_type=F32) + bb_ref[...]
        a = a * _sigmoid(b)
    elif mode == "sigmoid":
        a = _sigmoid(a)
    elif out_scale != 1.0:
        a = a * out_scale
    o_ref[...] = a.astype(o_ref.dtype)


def _in_proj(h, w_b, b_in2, col_a, col_b, n_cols, mode, out_dtype, out_scale=1.0):
    tm, tn = (1536 if mode == "glu" else 3072), 512
    ca, cb = col_a // tn, (col_b // tn if col_b is not None else 0)
    in_specs = [pl.BlockSpec((tm, D), lambda i, j: (i, 0)),
                pl.BlockSpec((D, tn), lambda i, j: (0, ca + j))]
    args = [h, w_b]
    if mode == "glu":
        in_specs.append(pl.BlockSpec((D, tn), lambda i, j: (0, cb + j)))
        args.append(w_b)
    in_specs.append(pl.BlockSpec((1, tn), lambda i, j: (0, ca + j)))
    args.append(b_in2)
    if mode == "glu":
        in_specs.append(pl.BlockSpec((1, tn), lambda i, j: (0, cb + j)))
        args.append(b_in2)
    return pl.pallas_call(
        functools.partial(_in_proj_kernel, mode=mode, out_scale=out_scale),
        out_shape=jax.ShapeDtypeStruct((T, n_cols), out_dtype),
        grid=(T // tm, n_cols // tn),
        in_specs=in_specs,
        out_specs=pl.BlockSpec((tm, tn), lambda i, j: (i, j)),
        compiler_params=_cparams(("parallel", "arbitrary")),
        name="in_proj_" + mode,
    )(*args)


CONV_TL = 128
HALO = 32
CONV_STEPS = TP // CONV_TL


def _ln_swish(z, g, b):
    mu = jnp.mean(z, axis=-1, keepdims=True)
    d = z - mu
    var = jnp.mean(d * d, axis=-1, keepdims=True)
    zn = d * lax.rsqrt(var + EPS) * g + b
    return zn * _sigmoid(zn)


def _conv_prompt_kernel(u_ref, halo_ref, w_ref, dwb_ref, g_ref, b_ref, o_ref, win, z_scr):
    i = pl.program_id(0)

    @pl.when(i < CONV_STEPS)
    def _():
        first = (i % (SP // CONV_TL)) == 0
        base = HALO - (CW - 1)
        for s in range(D // 128):
            cs = slice(s * 128, (s + 1) * 128)
            win[s, 0:HALO, :] = jnp.where(first, 0.0, halo_ref[:, cs])
            win[s, HALO:HALO + CONV_TL, :] = u_ref[:, cs]
        for s in range(D // 128):
            cs = slice(s * 128, (s + 1) * 128)
            acc = None
            for rho in range(8):
                offs = [base + j for j in range(CW) if (base + j) % 8 == rho]
                seg = win[s, offs[0]:offs[-1] + CONV_TL, :]
                for o in offs:
                    j = o - base
                    term = seg[o - offs[0]:o - offs[0] + CONV_TL] * w_ref[j:j + 1, cs]
                    acc = term if acc is None else acc + term
            z_scr[:, cs] = acc + dwb_ref[:, cs]
        o_ref[...] = _ln_swish(z_scr[...], g_ref[...], b_ref[...]).astype(BF16)

    @pl.when(i >= CONV_STEPS)
    def _():
        o_ref[...] = jnp.zeros_like(o_ref)


def _conv_prompt(u_all, dw_w, dw_b, ln_g, ln_b):
    hb = CONV_TL // HALO
    last = CONV_STEPS - 1
    const = lambda i: (0, 0)
    return pl.pallas_call(
        _conv_prompt_kernel,
        out_shape=jax.ShapeDtypeStruct((T, D), BF16),
        grid=(T // CONV_TL,),
        in_specs=[pl.BlockSpec((CONV_TL, D), lambda i: (jnp.minimum(i, last), 0)),
                  pl.BlockSpec((HALO, D), lambda i: (jnp.maximum(jnp.minimum(i, last) * hb - 1, 0), 0)),
                  pl.BlockSpec((CW, D), const),
                  pl.BlockSpec((1, D), const),
                  pl.BlockSpec((1, D), const),
                  pl.BlockSpec((1, D), const)],
        out_specs=pl.BlockSpec((CONV_TL, D), lambda i: (i, 0)),
        scratch_shapes=[pltpu.VMEM((D // 128, HALO + CONV_TL, 128), F32), pltpu.VMEM((CONV_TL, D), F32)],
        compiler_params=_cparams(("parallel",)),
        name="conv_prompt",
    )(u_all, u_all, dw_w, dw_b.reshape(1, D), ln_g.reshape(1, D), ln_b.reshape(1, D))


CONV_BB = 16


def _conv_sample_kernel(st_ref, u_ref, w_ref, dwb_ref, g_ref, b_ref, o_ref, ns_ref, win, z_scr):
    win[0:CW - 1] = st_ref[...]
    win[CW - 1:CW - 1 + SS] = u_ref[...]
    ns_ref[0:CW - 1 - SS] = st_ref[SS:CW - 1]
    ns_ref[CW - 1 - SS:CW - 1] = u_ref[...]
    for s in range(D // 128):
        cs = slice(s * 128, (s + 1) * 128)
        acc = None
        for j in range(CW):
            term = win[j:j + SS, :, cs] * w_ref[j:j + 1, cs][None]
            acc = term if acc is None else acc + term
        z_scr[:, :, cs] = acc + dwb_ref[:, cs][None]
    o_ref[...] = _ln_swish(z_scr[...], g_ref[...][None], b_ref[...][None]).astype(BF16)


def _conv_sample(state_t, u_t, dw_w, dw_b, ln_g, ln_b):
    const = lambda i: (0, 0)
    batch = lambda i: (0, i, 0)
    return pl.pallas_call(
        _conv_sample_kernel,
        out_shape=(jax.ShapeDtypeStruct((SS, BS, D), BF16), jax.ShapeDtypeStruct((CW - 1, BS, D), F32)),
        grid=(BS // CONV_BB,),
        in_specs=[pl.BlockSpec((CW - 1, CONV_BB, D), batch),
                  pl.BlockSpec((SS, CONV_BB, D), batch),
                  pl.BlockSpec((CW, D), const),
                  pl.BlockSpec((1, D), const),
                  pl.BlockSpec((1, D), const),
                  pl.BlockSpec((1, D), const)],
        out_specs=(pl.BlockSpec((SS, CONV_BB, D), batch),
                   pl.BlockSpec((CW - 1, CONV_BB, D), batch)),
        scratch_shapes=[pltpu.VMEM((CW - 1 + SS, CONV_BB, D), F32), pltpu.VMEM((SS, CONV_BB, D), F32)],
        compiler_params=_cparams(("parallel",)),
        name="conv_sample",
    )(state_t, u_t, dw_w, dw_b.reshape(1, D), ln_g.reshape(1, D), ln_b.reshape(1, D))


ATT_STEPS = TP // WIN


def _attn_prompt_kernel(sink_ref, q_ref, kvc_ref, kvp_ref, o_ref):
    i = pl.program_id(0)
    n = i % (SP // WIN)

    def kv_head(kh):
        row = lax.broadcasted_iota(jnp.int32, (WIN, 2 * WIN), 0)
        col = lax.broadcasted_iota(jnp.int32, (WIN, 2 * WIN), 1)
        diff = row + WIN - col
        first_key = jnp.where(n > 0, 0, WIN)
        mask = (diff >= 0) & (diff < WIN) & (col >= first_key)
        ks = slice(kh * HD, (kh + 1) * HD)
        vs = slice(KVW + kh * HD, KVW + (kh + 1) * HD)
        k2 = jnp.concatenate([kvp_ref[:, ks], kvc_ref[:, ks]], axis=0).astype(BF16)
        v2 = jnp.concatenate([kvp_ref[:, vs], kvc_ref[:, vs]], axis=0).astype(BF16)
        outs = []
        for g in range(GRP):
            h = kh * GRP + g
            sink = sink_ref[h]
            qh = q_ref[:, h * HD:(h + 1) * HD]
            s = lax.dot_general(qh, k2, (((1,), (1,)), ((), ())), preferred_element_type=F32)
            s = jnp.where(mask, s, -jnp.inf)
            m = jnp.maximum(jnp.max(s, axis=-1, keepdims=True), sink)
            p = jnp.exp(s - m)
            den = jnp.sum(p, axis=-1, keepdims=True) + jnp.exp(sink - m)
            o = jnp.dot(p.astype(BF16), v2, preferred_element_type=F32)
            outs.append(o / den)
        o_ref[:, kh * GRP * HD:(kh + 1) * GRP * HD] = jnp.concatenate(outs, axis=-1).astype(BF16)

    for kh in range(NKV):
        pl.when(i < ATT_STEPS)(functools.partial(kv_head, kh))

    @pl.when(i >= ATT_STEPS)
    def _():
        o_ref[...] = jnp.zeros_like(o_ref)


def _attn_prompt(sinks, q_all, kv_all):
    last = ATT_STEPS - 1
    nblk = SP // WIN

    def prev_block(i):
        ic = jnp.minimum(i, last)
        return (jnp.where(ic % nblk == 0, ic, ic - 1), 0)

    return pl.pallas_call(
        _attn_prompt_kernel,
        out_shape=jax.ShapeDtypeStruct((T, D), BF16),
        grid=(T // WIN,),
        in_specs=[pl.BlockSpec(memory_space=pltpu.SMEM),
                  pl.BlockSpec((WIN, D), lambda i: (jnp.minimum(i, last), 0)),
                  pl.BlockSpec((WIN, 2 * KVW), lambda i: (jnp.minimum(i, last), 0)),
                  pl.BlockSpec((WIN, 2 * KVW), prev_block)],
        out_specs=pl.BlockSpec((WIN, D), lambda i: (i, 0)),
        compiler_params=_cparams(("parallel",)),
        name="attn_prompt",
    )(sinks, q_all, kv_all, kv_all)


ATT_BB = 8
QG = SS * GRP


def _attn_sample_kernel(sink_ref, q_ref, kvn_ref, ck_ref, cv_ref, o_ref):
    rr = lax.broadcasted_iota(jnp.int32, (QG, WIN), 0)
    qi_c = rr % SS
    key_c = lax.broadcasted_iota(jnp.int32, (QG, WIN), 1)
    mask_c = (key_c > qi_c)[None]
    rr_n = lax.broadcasted_iota(jnp.int32, (QG, SS), 0)
    key_n = lax.broadcasted_iota(jnp.int32, (QG, SS), 1)
    mask_n = (key_n <= rr_n % SS)[None]
    grow = lax.broadcasted_iota(jnp.int32, (QG, 1), 0) // SS
    for kh in range(NKV):
        sink = jnp.zeros((QG, 1), F32)
        for g in range(GRP):
            sink = jnp.where(grow == g, sink_ref[kh * GRP + g], sink)
        sink = sink[None]
        q = q_ref[:, kh]
        kc = ck_ref[:, kh].astype(BF16)
        vc = cv_ref[:, kh].astype(BF16)
        kn = kvn_ref[:, :, kh * HD:(kh + 1) * HD].astype(BF16)
        vn = kvn_ref[:, :, KVW + kh * HD:KVW + (kh + 1) * HD].astype(BF16)
        s1 = jnp.einsum("bqd,bds->bqs", q, kc, preferred_element_type=F32)
        s2 = jnp.einsum("bqd,bsd->bqs", q, kn, preferred_element_type=F32)
        s1 = jnp.where(mask_c, s1, -jnp.inf)
        s2 = jnp.where(mask_n, s2, -jnp.inf)
        m = jnp.maximum(jnp.maximum(jnp.max(s1, axis=-1, keepdims=True),
                                    jnp.max(s2, axis=-1, keepdims=True)), sink)
        p1 = jnp.exp(s1 - m)
        p2 = jnp.exp(s2 - m)
        den = (jnp.sum(p1, axis=-1, keepdims=True) + jnp.sum(p2, axis=-1, keepdims=True)
               + jnp.exp(sink - m))
        o = (jnp.einsum("bqs,bds->bqd", p1.astype(BF16), vc, preferred_element_type=F32)
             + jnp.einsum("bqs,bsd->bqd", p2.astype(BF16), vn, preferred_element_type=F32))
        o_ref[:, kh] = (o / den).astype(BF16)


def _attn_sample(sinks, q_g, kv_new, ck, cv):
    return pl.pallas_call(
        _attn_sample_kernel,
        out_shape=jax.ShapeDtypeStruct((BS, NKV, QG, HD), BF16),
        grid=(BS // ATT_BB,),
        in_specs=[pl.BlockSpec(memory_space=pltpu.SMEM),
                  pl.BlockSpec((ATT_BB, NKV, QG, HD), lambda i: (i, 0, 0, 0)),
                  pl.BlockSpec((ATT_BB, SS, 2 * KVW), lambda i: (i, 0, 0)),
                  pl.BlockSpec((ATT_BB, NKV, HD, WIN), lambda i: (i, 0, 0, 0)),
                  pl.BlockSpec((ATT_BB, NKV, HD, WIN), lambda i: (i, 0, 0, 0))],
        out_specs=pl.BlockSpec((ATT_BB, NKV, QG, HD), lambda i: (i, 0, 0, 0)),
        compiler_params=_cparams(("parallel",)),
        name="attn_sample",
    )(sinks, q_g, kv_new, ck, cv)


def _merge_kernel(zs_ref, at_ref, wc_ref, wa_ref, bc_ref, ba_ref, gc_ref, ga_ref, o_ref):
    conv = jnp.dot(zs_ref[...], wc_ref[...].astype(BF16), preferred_element_type=F32) + bc_ref[...]
    attn = jnp.dot(at_ref[...], wa_ref[...].astype(BF16), preferred_element_type=F32) + ba_ref[...]
    o_ref[...] = (gc_ref[...] * conv + ga_ref[...] * attn).astype(BF16)


def _merge(zs, attn, w_co, w_ao, b_co, b_ao, gates):
    tm, tn = 1024, 512
    ga_off = D // tn
    return pl.pallas_call(
        _merge_kernel,
        out_shape=jax.ShapeDtypeStruct((T, D), BF16),
        grid=(T // tm, D // tn),
        in_specs=[pl.BlockSpec((tm, D), lambda i, j: (i, 0)),
                  pl.BlockSpec((tm, D), lambda i, j: (i, 0)),
                  pl.BlockSpec((D, tn), lambda i, j: (0, j)),
                  pl.BlockSpec((D, tn), lambda i, j: (0, j)),
                  pl.BlockSpec((1, tn), lambda i, j: (0, j)),
                  pl.BlockSpec((1, tn), lambda i, j: (0, j)),
                  pl.BlockSpec((tm, tn), lambda i, j: (i, j)),
                  pl.BlockSpec((tm, tn), lambda i, j: (i, ga_off + j))],
        out_specs=pl.BlockSpec((tm, tn), lambda i, j: (i, j)),
        compiler_params=_cparams(("parallel", "arbitrary")),
        name="mixer_merge",
    )(zs, attn, w_co, w_ao, b_co.reshape(1, D), b_ao.reshape(1, D), gates, gates)


def _router_logits(h, w, b):
    h_hi = h.astype(BF16)
    h_lo = (h - h_hi.astype(F32)).astype(BF16)
    w_hi = w.astype(BF16)
    w_lo = (w - w_hi.astype(F32)).astype(BF16)
    dn = (((1,), (1,)), ((), ()))
    return (lax.dot_general(w_hi, h_hi, dn, preferred_element_type=F32)
            + lax.dot_general(w_hi, h_lo, dn, preferred_element_type=F32)
            + lax.dot_general(w_lo, h_hi, dn, preferred_element_type=F32)) + b


def _out_proj_kernel(mg_ref, xp_ref, xs_ref, w_ref, g_ref, m2p_ref, m3p_ref, m4p_ref,
                     m2s_ref, m3s_ref, m4s_ref, wr_ref, br_ref, x1_ref, h2_ref, lg_ref, *, n_prompt):
    i = pl.program_id(0)
    y = jnp.dot(mg_ref[...], w_ref[...], preferred_element_type=F32)

    def finish(x_ref, m2_ref, m3_ref, m4_ref):
        x1 = x_ref[...] + m2_ref[...] * y.reshape(x_ref.shape)
        x1_ref[...] = x1.reshape(x1_ref.shape)
        h2 = _norm_mod(x1, g_ref[...], m3_ref[...], m4_ref[...]).reshape(x1_ref.shape)
        h2_ref[...] = h2
        lg_ref[...] = _router_logits(h2, wr_ref[...], br_ref[...])

    @pl.when(i < n_prompt)
    def _():
        finish(xp_ref, m2p_ref, m3p_ref, m4p_ref)

    @pl.when(i >= n_prompt)
    def _():
        finish(xs_ref, m2s_ref, m3s_ref, m4s_ref)


def _out_proj(merged, x_prompt, x_sample, w_out_b, g2, mp, ms, w_router, b_router):
    tl = _Tiles(512)
    xp, xs, mps, mss = tl.specs()
    row = pl.BlockSpec((tl.tm, D), lambda i: (i, 0))
    once = pl.Buffered(1)
    return pl.pallas_call(
        functools.partial(_out_proj_kernel, n_prompt=tl.n_prompt),
        out_shape=(jax.ShapeDtypeStruct((T, D), F32),
                   jax.ShapeDtypeStruct((T, D), F32),
                   jax.ShapeDtypeStruct((NE, T), F32)),
        grid=(tl.n,),
        in_specs=[row, xp, xs, pl.BlockSpec((D, D), lambda i: (0, 0), pipeline_mode=once),
                  pl.BlockSpec((1, 1, D), lambda i: (0, 0, 0)), mps, mps, mps, mss, mss, mss,
                  pl.BlockSpec((NE, D), lambda i: (0, 0), pipeline_mode=once),
                  pl.BlockSpec((NE, 1), lambda i: (0, 0))],
        out_specs=(row, row, pl.BlockSpec((NE, tl.tm), lambda i: (0, i))),
        compiler_params=_cparams(("parallel",)),
        name="out_proj_norm2",
    )(merged, x_prompt, x_sample, w_out_b, g2.reshape(1, 1, D), mp[2], mp[3], mp[4], ms[2], ms[3], ms[4],
      w_router.T, b_router.reshape(NE, 1))


ROUTE_TR = 512


def _router_kernel(lg_ref, tri_ref, idx_ref, gate_ref, rank_ref, cnt_ref, carry):
    @pl.when(pl.program_id(0) == 0)
    def _():
        carry[...] = jnp.zeros_like(carry)

    eid = lax.broadcasted_iota(jnp.int32, (NE, ROUTE_TR), 0).astype(F32)
    work = lg_ref[...]
    vals, ids = [], []
    onehot = jnp.zeros((NE, ROUTE_TR), F32)
    for _ in range(TOPK):
        m = jnp.max(work, axis=0, keepdims=True)
        sel = jnp.min(jnp.where(work == m, eid, float(NE)), axis=0, keepdims=True)
        hit = eid == sel
        work = jnp.where(hit, -jnp.inf, work)
        onehot = jnp.where(hit, 1.0, onehot)
        vals.append(m)
        ids.append(sel)
    es = [jnp.exp(v - vals[0]) for v in vals]
    den = es[0] + es[1] + es[2] + es[3]
    before = jnp.dot(onehot.astype(BF16), tri_ref[...], preferred_element_type=F32) + carry[:, 0:1]
    for k in range(TOPK):
        idx_ref[k:k + 1, :] = ids[k].astype(jnp.int32)
        gate_ref[k:k + 1, :] = es[k] / den
        rank_ref[k:k + 1, :] = jnp.sum(jnp.where(eid == ids[k], before, 0.0), axis=0,
                                       keepdims=True).astype(jnp.int32)
    carry[...] = carry[...] + jnp.sum(onehot, axis=1, keepdims=True)
    cnt_ref[...] = carry[...]


def _router(logits):
    tri = jnp.triu(jnp.ones((ROUTE_TR, ROUTE_TR), F32), 1).astype(BF16)
    row = lambda i: (0, i)
    return pl.pallas_call(
        _router_kernel,
        out_shape=(jax.ShapeDtypeStruct((TOPK, T), jnp.int32),
                   jax.ShapeDtypeStruct((TOPK, T), F32),
                   jax.ShapeDtypeStruct((TOPK, T), jnp.int32),
                   jax.ShapeDtypeStruct((NE, 128), F32)),
        grid=(T // ROUTE_TR,),
        in_specs=[pl.BlockSpec((NE, ROUTE_TR), row),
                  pl.BlockSpec((ROUTE_TR, ROUTE_TR), lambda i: (0, 0))],
        out_specs=(pl.BlockSpec((TOPK, ROUTE_TR), row),
                   pl.BlockSpec((TOPK, ROUTE_TR), row),
                   pl.BlockSpec((TOPK, ROUTE_TR), row),
                   pl.BlockSpec((NE, 128), lambda i: (0, 0))),
        scratch_shapes=[pltpu.VMEM((NE, 128), F32)],
        compiler_params=_cparams(("arbitrary",)),
        name="router_top4",
    )(logits, tri)


DISPATCH_ROWS = SUB
DISPATCH_STEPS = P_MAX // DISPATCH_ROWS
SCALAR_UNROLL = 8


def _dispatch_kernel(slot_ref, nt_ref, pad_lo_ref, pad_hi_ref, h_hbm, o_ref, tok_ref, buf, sem):
    i = pl.program_id(0)
    nt = nt_ref[0] * SUB // DISPATCH_ROWS

    @pl.when(i == 0)
    def _():
        for e in range(NE):
            def zero(j, c):
                tok_ref[j] = 0
                return c
            lax.fori_loop(pad_lo_ref[e], pad_hi_ref[e], zero, 0)
        for k in range(TOPK):
            def put(j, c, k=k):
                for u in range(SCALAR_UNROLL):
                    t = j * SCALAR_UNROLL + u
                    tok_ref[slot_ref[k * T + t]] = t
                return c
            lax.fori_loop(0, T // SCALAR_UNROLL, put, 0)

    def issue(step, slot):
        for r in range(DISPATCH_ROWS):
            pltpu.make_async_copy(h_hbm.at[pl.ds(tok_ref[step * DISPATCH_ROWS + r], 1)],
                                  buf.at[slot, pl.ds(r, 1)], sem.at[slot]).start()

    @pl.when(i == 0)
    def _():
        issue(0, 0)

    @pl.when(i + 1 < nt)
    def _():
        issue(i + 1, (i + 1) % 2)

    @pl.when(i < nt)
    def _():
        slot = i % 2
        pltpu.make_async_copy(h_hbm.at[pl.ds(0, DISPATCH_ROWS)], buf.at[slot], sem.at[slot]).wait()
        o_ref[...] = buf[slot].astype(BF16)

    @pl.when(i >= nt)
    def _():
        o_ref[...] = jnp.zeros_like(o_ref)


def _dispatch(slots, nt_used, pad_lo, pad_hi, h2):
    return pl.pallas_call(
        _dispatch_kernel,
        out_shape=jax.ShapeDtypeStruct((P_MAX, D), BF16),
        grid_spec=pltpu.PrefetchScalarGridSpec(
            num_scalar_prefetch=4,
            grid=(DISPATCH_STEPS,),
            in_specs=[pl.BlockSpec(memory_space=pl.ANY)],
            out_specs=pl.BlockSpec((DISPATCH_ROWS, D), lambda i, *_: (i, 0)),
            scratch_shapes=[pltpu.SMEM((P_MAX,), jnp.int32),
                            pltpu.VMEM((2, DISPATCH_ROWS, D), F32),
                            pltpu.SemaphoreType.DMA((2,))]),
        compiler_params=_cparams(("arbitrary",)),
        name="moe_dispatch",
    )(slots, nt_used, pad_lo, pad_hi, h2)


def _moe_kernel(ge_ref, gs_ref, gn_ref, nt_ref, w1_ref, b1_ref, w2_ref, b2_ref, sel_ref, xs_hbm, y_hbm,
                x_buf, acc, h_scr, sem_in, sem_out):
    del ge_ref
    g = pl.program_id(0)
    c = pl.program_id(1)
    nsub = gn_ref[g]
    start = gs_ref[g]

    def rows_of(r):
        return pl.ds(pl.multiple_of(r * SUB, SUB), SUB)

    def x_copy(r):
        return pltpu.make_async_copy(xs_hbm.at[pl.ds((start + r) * SUB, SUB)], x_buf.at[rows_of(r)],
                                     sem_in.at[r])

    def y_copy(r):
        return pltpu.make_async_copy(acc.at[rows_of(r)], y_hbm.at[pl.ds((start + r) * SUB, SUB)], sem_out)

    def for_subtiles(fn, n):
        def body(r, carry):
            fn(r)
            return carry
        lax.fori_loop(0, n, body, 0)

    def up_proj(r):
        return (jnp.dot(x_buf[rows_of(r), :], w1_ref[0].astype(BF16), preferred_element_type=F32)
                + b1_ref[0])

    def down_proj(r, h):
        hn = jnp.concatenate(
            [pltpu.roll(h[:, k * 128:(k + 1) * 128], 127, 1) for k in range(2 * TF // 128)], axis=1)
        glu = jnp.minimum(h, LIMIT)
        lin = jnp.clip(hn, -LIMIT, LIMIT)
        act = (glu * _sigmoid(ALPHA * glu) * (lin + 1.0)).astype(BF16)
        sel = sel_ref[...]
        act = jnp.concatenate(
            [jnp.dot(act[:, k * MXU_N:(k + 1) * MXU_N], sel, preferred_element_type=F32)
             for k in range(2 * TF // MXU_N)], axis=1).astype(BF16)
        rows = rows_of(r)
        acc[rows, :] = acc[rows, :] + jnp.dot(act, w2_ref[0].astype(BF16), preferred_element_type=F32)

    @pl.when(nsub > 0)
    def _():
        @pl.when(c == 0)
        def _():
            for_subtiles(lambda r: x_copy(r).start(), nsub)

            def init(r):
                acc[rows_of(r), :] = jnp.broadcast_to(b2_ref[0], (SUB, D))
            for_subtiles(init, nsub)
            x_copy(0).wait()

        h_scr[0] = up_proj(0)

        def step(r):
            @pl.when(c == 0)
            def _():
                x_copy(r + 1).wait()

            h = h_scr[r % 2]
            h_scr[(r + 1) % 2] = up_proj(r + 1)
            down_proj(r, h)

            @pl.when(c == NC - 1)
            def _():
                y_copy(r).start()
        for_subtiles(step, nsub - 1)
        down_proj(nsub - 1, h_scr[(nsub - 1) % 2])

        @pl.when(c == NC - 1)
        def _():
            y_copy(nsub - 1).start()
            for_subtiles(lambda r: y_copy(r).wait(), nsub)

    @pl.when((g == G_MAX - 1) & (c == NC - 1))
    def _():
        acc[0:SUB, :] = jnp.zeros((SUB, D), F32)

        def fill(tile, carry):
            cp = pltpu.make_async_copy(acc.at[pl.ds(0, SUB)], y_hbm.at[pl.ds(tile * SUB, SUB)], sem_out)
            cp.start()
            cp.wait()
            return carry
        lax.fori_loop(nt_ref[0], NT_MAX, fill, 0)


def _moe(g_expert, g_start, g_nsub, nt_used, w1, b1, w2, b2, xs):
    sel = (jnp.arange(MXU_N)[:, None] == 2 * jnp.arange(MXU_N // 2)[None, :]).astype(BF16)

    def cidx(g, c, gn):
        return jnp.where(gn[g] > 0, c, NC - 1)

    return pl.pallas_call(
        _moe_kernel,
        out_shape=jax.ShapeDtypeStruct((P_MAX, D), F32),
        grid_spec=pltpu.PrefetchScalarGridSpec(
            num_scalar_prefetch=4,
            grid=(G_MAX, NC),
            in_specs=[pl.BlockSpec((1, D, 2 * TF), lambda g, c, ge, gs, gn, nt: (ge[g], 0, cidx(g, c, gn))),
                      pl.BlockSpec((1, 1, 2 * TF), lambda g, c, ge, gs, gn, nt: (ge[g], 0, cidx(g, c, gn))),
                      pl.BlockSpec((1, TF, D), lambda g, c, ge, gs, gn, nt: (ge[g], cidx(g, c, gn), 0)),
                      pl.BlockSpec((1, 1, D), lambda g, c, ge, gs, gn, nt: (ge[g], 0, 0)),
                      pl.BlockSpec((MXU_N, MXU_N // 2), lambda g, c, ge, gs, gn, nt: (0, 0)),
                      pl.BlockSpec(memory_space=pl.ANY)],
            out_specs=pl.BlockSpec(memory_space=pl.ANY),
            scratch_shapes=[pltpu.VMEM((GSUB * SUB, D), BF16),
                            pltpu.VMEM((GSUB * SUB, D), F32),
                            pltpu.VMEM((2, SUB, 2 * TF), F32),
                            pltpu.SemaphoreType.DMA((GSUB,)),
                            pltpu.SemaphoreType.DMA]),
        compiler_params=_cparams(("arbitrary", "arbitrary")),
        name="moe_experts",
    )(g_expert, g_start, g_nsub, nt_used, w1, b1.reshape(NE, 1, 2 * DFF), w2, b2.reshape(NE, 1, D), sel, xs)


COMB_TM = 128


def _combine_kernel(slot_ref, x1_ref, gate_ref, m5_ref, g_ref, y_hbm, o_ref, buf, sem, *, tok0):
    i = pl.program_id(0)
    n = pl.num_programs(0)

    def issue(tile, slot):
        for k in range(TOPK):
            for r in range(COMB_TM):
                s = slot_ref[k * T + tok0 + tile * COMB_TM + r]
                pltpu.make_async_copy(y_hbm.at[pl.ds(s, 1)], buf.at[slot, k, pl.ds(r, 1)],
                                      sem.at[slot]).start()

    @pl.when(i == 0)
    def _():
        issue(0, 0)

    @pl.when(i + 1 < n)
    def _():
        issue(i + 1, (i + 1) % 2)

    slot = i % 2
    for k in range(TOPK):
        pltpu.make_async_copy(y_hbm.at[pl.ds(0, COMB_TM)], buf.at[slot, k], sem.at[slot]).wait()
    gates = gate_ref[...]
    f = None
    for k in range(TOPK):
        term = gates[:, k:k + 1] * buf[slot, k]
        f = term if f is None else f + term
    x2 = x1_ref[...].reshape(o_ref.shape) + m5_ref[...] * f.reshape(o_ref.shape)
    ms = jnp.mean(x2 * x2, axis=-1, keepdims=True)
    o_ref[...] = x2 * lax.rsqrt(ms + EPS) * g_ref[...]


def _combine(prompt, slots, x1, gates_t, m5, gf, y):
    tm = COMB_TM
    if prompt:
        nb, r, steps, rb0, shape3 = 1, tm, TP // tm, 0, (BP, SP, D)
        per_b = SP // tm
        x_index = lambda i, s: (i // per_b, i % per_b, 0)
        m_index = lambda i, s: (i // per_b, 0, 0)
    else:
        nb, r, steps, rb0, shape3 = tm // SS, SS, TS // tm, TP // tm, (BS, SS, D)
        x_index = lambda i, s: (i, 0, 0)
        m_index = x_index
    return pl.pallas_call(
        functools.partial(_combine_kernel, tok0=rb0 * tm),
        out_shape=jax.ShapeDtypeStruct(shape3, F32),
        grid_spec=pltpu.PrefetchScalarGridSpec(
            num_scalar_prefetch=1,
            grid=(steps,),
            in_specs=[pl.BlockSpec((tm, D), lambda i, s: (rb0 + i, 0)),
                      pl.BlockSpec((tm, TOPK), lambda i, s: (rb0 + i, 0)),
                      pl.BlockSpec((nb, 1, D), m_index),
                      pl.BlockSpec((1, 1, D), lambda i, s: (0, 0, 0)),
                      pl.BlockSpec(memory_space=pl.ANY)],
            out_specs=pl.BlockSpec((nb, r, D), x_index),
            scratch_shapes=[pltpu.VMEM((2, TOPK, tm, D), F32), pltpu.SemaphoreType.DMA((2,))]),
        compiler_params=_cparams(("arbitrary",)),
        name="moe_combine_norm",
    )(slots, x1, gates_t, m5, gf.reshape(1, 1, D), y)


def _routing_tables(idx_t, rank_t, counts):
    tiles = (counts + SUB - 1) // SUB
    tile_end = jnp.cumsum(tiles)
    tile_start = tile_end - tiles
    nt_used = tile_end[-1:]
    experts = jnp.arange(NE, dtype=jnp.int32)
    base = jnp.sum(jnp.where(idx_t[..., None] == experts, tile_start * SUB, 0), axis=-1)
    slot = base + rank_t
    ngrp = (tiles + GSUB - 1) // GSUB
    grp_end = jnp.cumsum(ngrp)
    grp_start = grp_end - ngrp
    n_groups = grp_end[-1]
    gid = jnp.arange(G_MAX, dtype=jnp.int32)
    gclamp = jnp.minimum(gid, jnp.maximum(n_groups - 1, 0))
    g_expert = jnp.sum(gclamp[:, None] >= grp_end[None, :], axis=-1).astype(jnp.int32)
    pick = lambda v: jnp.sum(jnp.where(g_expert[:, None] == experts[None, :], v[None, :], 0), axis=-1)
    local = gclamp - pick(grp_start)
    g_start = pick(tile_start) + local * GSUB
    g_nsub = jnp.where(gid < n_groups, jnp.minimum(GSUB, pick(tiles) - local * GSUB), 0)
    pad_lo = (tile_start * SUB + counts).astype(jnp.int32)
    pad_hi = (tile_end * SUB).astype(jnp.int32)
    return (slot.reshape(-1).astype(jnp.int32), nt_used.astype(jnp.int32), pad_lo, pad_hi,
            g_expert, g_start.astype(jnp.int32), g_nsub.astype(jnp.int32))


def kernel(x_prompt, x_sample, c_prompt, c_sample, cache_k, cache_v, state_conv, w_ada, b_ada, norm1_g,
           w_in, b_in, conv_dw_w, conv_dw_b, conv_ln_g, conv_ln_b, w_conv_out, b_conv_out, sinks,
           w_attn_out, b_attn_out, w_out, norm2_g, w_router, b_router, w_mlp1, b_mlp1, w_mlp2, b_mlp2,
           norm_f_g):
    mods = _modulations(jnp.concatenate([c_prompt, c_sample], axis=0), w_ada, b_ada)
    mp = [mods[:BP, k * D:(k + 1) * D].reshape(BP, 1, D) for k in range(6)]
    ms = [mods[BP:, k * D:(k + 1) * D].reshape(BS, 1, D) for k in range(6)]

    h1 = _norm1(x_prompt, x_sample, norm1_g, mp, ms)
    w_in_b = w_in
    b_in2 = b_in.reshape(1, IN_W)
    u_all = _in_proj(h1, w_in_b, b_in2, OFF_GLU_A, OFF_GLU_B, D, "glu", F32)
    q_all = _in_proj(h1, w_in_b, b_in2, OFF_Q, None, D, "plain", BF16, HD ** -0.5)
    kv_all = _in_proj(h1, w_in_b, b_in2, OFF_KV, None, 2 * KVW, "plain", F32)
    gates = _in_proj(h1, w_in_b, b_in2, OFF_GC, None, 2 * D, "sigmoid", F32)

    zs = _conv_prompt(u_all, conv_dw_w, conv_dw_b, conv_ln_g, conv_ln_b)
    u_s_t = u_all[TP:].reshape(BS, SS, D).transpose(1, 0, 2)
    zs_s_t, conv_s_t = _conv_sample(state_conv.transpose(1, 0, 2), u_s_t, conv_dw_w, conv_dw_b,
                                    conv_ln_g, conv_ln_b)
    zs = lax.dynamic_update_slice(zs, zs_s_t.transpose(1, 0, 2).reshape(TS, D), (TP, 0))
    conv_s = conv_s_t.transpose(1, 0, 2)
    conv_p = jnp.stack([u_all[(b + 1) * SP - (CW - 1):(b + 1) * SP] for b in range(BP)])

    attn = _attn_prompt(sinks, q_all, kv_all)
    q_g = (q_all[TP:].reshape(BS, SS, NKV, GRP, HD).transpose(0, 2, 3, 1, 4)
           .reshape(BS, NKV, QG, HD))
    kv_s = kv_all[TP:].reshape(BS, SS, 2 * KVW)
    attn_s = _attn_sample(sinks, q_g, kv_s, cache_k.transpose(0, 2, 3, 1), cache_v.transpose(0, 2, 3, 1))
    attn_s = (attn_s.reshape(BS, NKV, GRP, SS, HD).transpose(0, 3, 1, 2, 4).reshape(TS, D))
    attn = lax.dynamic_update_slice(attn, attn_s, (TP, 0))
    kv_tail = jnp.stack([kv_all[(b + 1) * SP - WIN:(b + 1) * SP] for b in range(BP)])
    k_p = kv_tail[:, :, :KVW].reshape(BP, WIN, NKV, HD)
    v_p = kv_tail[:, :, KVW:].reshape(BP, WIN, NKV, HD)
    k_s = jnp.concatenate([cache_k[:, SS:], kv_s[:, :, :KVW].reshape(BS, SS, NKV, HD)], axis=1)
    v_s = jnp.concatenate([cache_v[:, SS:], kv_s[:, :, KVW:].reshape(BS, SS, NKV, HD)], axis=1)

    merged = _merge(zs, attn, w_conv_out, w_attn_out, b_conv_out, b_attn_out,
                    gates)
    x1, h2, logits = _out_proj(merged, x_prompt, x_sample, w_out.astype(BF16), norm2_g, mp, ms,
                               w_router, b_router)

    idx_t, gate_t, rank_t, cnt = _router(logits)
    counts = cnt[:, 0].astype(jnp.int32)
    slots, nt_used, pad_lo, pad_hi, g_expert, g_start, g_nsub = _routing_tables(idx_t, rank_t, counts)

    xs = _dispatch(slots, nt_used, pad_lo, pad_hi, h2)
    y = _moe(g_expert, g_start, g_nsub, nt_used, w_mlp1, b_mlp1, w_mlp2, b_mlp2, xs)

    gates_tok = gate_t.T
    y_prompt = _combine(True, slots, x1, gates_tok, mp[5], norm_f_g, y)
    y_sample = _combine(False, slots, x1, gates_tok, ms[5], norm_f_g, y)
    return (y_prompt, y_sample, k_p, v_p, conv_p, k_s, v_s, conv_s)
```
